```python
import math
import jax, jax.numpy as jnp
from jax import lax
import numpy as np

D_MODEL = 4096
BATCH = 2
SEQ = 4096
DEPTH = 1

MEM_LEN = 256
MEM_HEADS = 4
MEM_HEAD_DIM = 128
MLA_HEADS = 16
MLA_Q_RANK = 896
MLA_KV_RANK = 512
MLA_NOPE = 128
MLA_ROPE = 64
MLA_V = 128
ROPE_THETA = 10000.0
SWA_HEADS = 32
SWA_KV_HEADS = 4
SWA_HEAD_DIM = 64
WINDOW = 128
REL_BUCKETS = 32
REL_MAX_DIST = 128
D_FF = 11008
BLOCK_Q = 128
EPS = 1e-6
NEG = -1e30

MLA_QK = MLA_NOPE + MLA_ROPE
MLA_WIDTH = MLA_HEADS * MLA_V
SWA_WIDTH = SWA_HEADS * SWA_HEAD_DIM
SWA_KV_WIDTH = SWA_KV_HEADS * SWA_HEAD_DIM
SWA_GROUP = SWA_HEADS // SWA_KV_HEADS
MIX_WIDTH = MLA_WIDTH + SWA_WIDTH
IN_CQ = MLA_Q_RANK
IN_CKV = IN_CQ + MLA_KV_RANK
IN_KR = IN_CKV + MLA_ROPE
IN_SQ = IN_KR + SWA_WIDTH
IN_SK = IN_SQ + SWA_KV_WIDTH
IN_WIDTH = IN_SK + SWA_KV_WIDTH

kernel_name = "hybrid_mla_swa_macaron_layer"


def rmsnorm(x, g):
    x32 = x.astype(jnp.float32)
    y = x32 * lax.rsqrt(jnp.mean(x32 * x32, axis=-1, keepdims=True) + EPS)
    return (y * g.astype(jnp.float32)).astype(x.dtype)


def swiglu(x, w_gate, w_up, w_down):
    return (jax.nn.silu(x @ w_gate) * (x @ w_up)) @ w_down


def rope(x, pos):
    half = x.shape[-1] // 2
    inv = ROPE_THETA ** (-jnp.arange(half, dtype=jnp.float32) / half)
    ang = pos.astype(jnp.float32)[:, None] * inv[None, :]
    cos = jnp.cos(ang)[:, None, :]
    sin = jnp.sin(ang)[:, None, :]
    x32 = x.astype(jnp.float32)
    x1, x2 = x32[..., :half], x32[..., half:]
    return jnp.concatenate([x1 * cos - x2 * sin, x2 * cos + x1 * sin], axis=-1).astype(x.dtype)


def t5_bucket(dist):
    max_exact = REL_BUCKETS // 2
    d = jnp.maximum(dist, 1).astype(jnp.float32)
    large = max_exact + (jnp.log(d / max_exact) / math.log(REL_MAX_DIST / max_exact)
                         * (REL_BUCKETS - max_exact)).astype(jnp.int32)
    large = jnp.minimum(large, REL_BUCKETS - 1)
    return jnp.where(dist < max_exact, dist, large)


def mla_group(c_q, c_kv, k_rope, q_a_norm, kv_a_norm, w_uq, w_ukv, q_norm, k_norm, pos):
    B, S, _ = c_q.shape
    q = (rmsnorm(c_q, q_a_norm) @ w_uq).reshape(B, S, MLA_HEADS, MLA_QK)
    kv = (rmsnorm(c_kv, kv_a_norm) @ w_ukv).reshape(B, S, MLA_HEADS, MLA_NOPE + MLA_V)
    k_nope, v = kv[..., :MLA_NOPE], kv[..., MLA_NOPE:]
    k_r = jnp.broadcast_to(k_rope[:, :, None, :], (B, S, MLA_HEADS, MLA_ROPE))
    k = jnp.concatenate([k_nope, k_r], axis=-1)
    q = rmsnorm(q, q_norm)
    k = rmsnorm(k, k_norm)
    q = jnp.concatenate([q[..., :MLA_NOPE], rope(q[..., MLA_NOPE:], pos)], axis=-1)
    k = jnp.concatenate([k[..., :MLA_NOPE], rope(k[..., MLA_NOPE:], pos)], axis=-1)
    nb = S // BLOCK_Q
    qb = q.reshape(B, nb, BLOCK_Q, MLA_HEADS, MLA_QK).transpose(1, 0, 2, 3, 4)
    scale = MLA_QK ** -0.5
    kpos = jnp.arange(S)

    def attend(args):
        qblk, bi = args
        s = jnp.einsum('bqhd,bkhd->bhqk', qblk, k).astype(jnp.float32) * scale
        qpos = bi * BLOCK_Q + jnp.arange(BLOCK_Q)
        mask = kpos[None, :] <= qpos[:, None]
        s = jnp.where(mask[None, None], s, NEG)
        p = jax.nn.softmax(s, axis=-1).astype(v.dtype)
        return jnp.einsum('bhqk,bkhd->bqhd', p, v)

    o = lax.map(attend, (qb, jnp.arange(nb)))
    return o.transpose(1, 0, 2, 3, 4).reshape(B, S, MLA_WIDTH)


def swa_group(q, k, v, q_norm, k_norm, sinks, rel_bias):
    B, S, _ = q.shape
    nb = S // BLOCK_Q
    q = rmsnorm(q.reshape(B, S, SWA_KV_HEADS, SWA_GROUP, SWA_HEAD_DIM), q_norm)
    k = rmsnorm(k.reshape(B, S, SWA_KV_HEADS, SWA_HEAD_DIM), k_norm)
    v = v.reshape(B, S, SWA_KV_HEADS, SWA_HEAD_DIM)
    qb = q.reshape(B, nb, BLOCK_Q, SWA_KV_HEADS, SWA_GROUP, SWA_HEAD_DIM)

    def band(t):
        tb = t.reshape(B, nb, BLOCK_Q, SWA_KV_HEADS, SWA_HEAD_DIM)
        prev = jnp.pad(tb, ((0, 0), (1, 0), (0, 0), (0, 0), (0, 0)))[:, :-1]
        return jnp.concatenate([prev, tb], axis=2)

    kb, vb = band(k), band(v)
    s = jnp.einsum('bnqkgd,bnckd->bnkgqc', qb, kb).astype(jnp.float32) * (SWA_HEAD_DIM ** -0.5)
    dist = jnp.arange(BLOCK_Q)[:, None] + BLOCK_Q - jnp.arange(2 * BLOCK_Q)[None, :]
    in_window = (dist >= 0) & (dist < WINDOW)
    key_exists = (jnp.arange(nb)[:, None] * BLOCK_Q + jnp.arange(2 * BLOCK_Q)[None, :] - BLOCK_Q) >= 0
    mask = in_window[None, :, :] & key_exists[:, None, :]
    bias = rel_bias.astype(jnp.float32)[t5_bucket(jnp.maximum(dist, 0))]
    bias = bias.transpose(2, 0, 1).reshape(SWA_KV_HEADS, SWA_GROUP, BLOCK_Q, 2 * BLOCK_Q)
    s = jnp.where(mask[None, :, None, None], s + bias[None, None], NEG)
    sink = sinks.astype(jnp.float32).reshape(SWA_KV_HEADS, SWA_GROUP)[None, None, :, :, None, None]
    m = jnp.maximum(jnp.max(s, axis=-1, keepdims=True), sink)
    e = jnp.exp(s - m)
    p = (e / (jnp.sum(e, axis=-1, keepdims=True) + jnp.exp(sink - m))).astype(v.dtype)
    o = jnp.einsum('bnkgqc,bnckd->bnqkgd', p, vb)
    return o.reshape(B, S, SWA_WIDTH)


def mem_cross_attn(h, mem_n, w_q, w_kv, q_norm, k_norm, w_o):
    B, S, _ = h.shape
    M = mem_n.shape[1]
    q = rmsnorm((h @ w_q).reshape(B, S, MEM_HEADS, MEM_HEAD_DIM), q_norm)
    kv = (mem_n @ w_kv).reshape(B, M, MEM_HEADS, 2 * MEM_HEAD_DIM)
    k = rmsnorm(kv[..., :MEM_HEAD_DIM], k_norm)
    v = kv[..., MEM_HEAD_DIM:]
    s = jnp.einsum('bshd,bmhd->bhsm', q, k).astype(jnp.float32) * (MEM_HEAD_DIM ** -0.5)
    p = jax.nn.softmax(s, axis=-1).astype(v.dtype)
    o = jnp.einsum('bhsm,bmhd->bshd', p, v).reshape(B, S, MEM_HEADS * MEM_HEAD_DIM)
    return o @ w_o


def hybrid_layer(h, mem, rel_bias, p):
    S = h.shape[1]
    pos = jnp.arange(S)
    h = h + 0.5 * swiglu(rmsnorm(h, p['ffn_a_norm']), p['ffn_a_gate'], p['ffn_a_up'], p['ffn_a_down'])
    u = rmsnorm(h, p['mix_norm']) @ p['w_in']
    o_a = mla_group(u[..., :IN_CQ], u[..., IN_CQ:IN_CKV], u[..., IN_CKV:IN_KR],
                    p['mla_q_a_norm'], p['mla_kv_a_norm'], p['mla_w_uq'], p['mla_w_ukv'],
                    p['mla_q_norm'], p['mla_k_norm'], pos)
    o_b = swa_group(u[..., IN_KR:IN_SQ], u[..., IN_SQ:IN_SK], u[..., IN_SK:IN_WIDTH],
                    p['swa_q_norm'], p['swa_k_norm'], p['swa_sinks'], rel_bias)
    o = jnp.concatenate([rmsnorm(o_a, p['out_norm_mla']), rmsnorm(o_b, p['out_norm_swa'])], axis=-1)
    h = h + o @ p['w_out']
    h = h + mem_cross_attn(rmsnorm(h, p['mem_attn_norm']), rmsnorm(mem, p['mem_norm']),
                           p['mem_w_q'], p['mem_w_kv'], p['mem_q_norm'], p['mem_k_norm'], p['mem_w_o'])
    h = h + 0.5 * swiglu(rmsnorm(h, p['ffn_b_norm']), p['ffn_b_gate'], p['ffn_b_up'], p['ffn_b_down'])
    return h


def setup_inputs(seed: int = 0) -> dict:
    key = jax.random.key(seed)
    ks = iter(jax.random.split(key, 40))
    f32 = jnp.float32

    def w(shape, fan_in):
        return jax.random.normal(next(ks), (DEPTH,) + shape, f32) * (fan_in ** -0.5)

    def g(n):
        return 1.0 + 0.02 * jax.random.normal(next(ks), (DEPTH, n), f32)

    inp = {}
    inp['x'] = jax.random.normal(next(ks), (BATCH, SEQ, D_MODEL), f32)
    inp['mem'] = jax.random.normal(next(ks), (BATCH, MEM_LEN, D_MODEL), f32)
    inp['rel_bias'] = 0.5 * jax.random.normal(next(ks), (REL_BUCKETS, SWA_HEADS), f32)
    inp['ffn_a_norm'] = g(D_MODEL)
    inp['ffn_a_gate'] = w((D_MODEL, D_FF), D_MODEL)
    inp['ffn_a_up'] = w((D_MODEL, D_FF), D_MODEL)
    inp['ffn_a_down'] = w((D_FF, D_MODEL), D_FF)
    inp['mix_norm'] = g(D_MODEL)
    inp['w_in'] = w((D_MODEL, IN_WIDTH), D_MODEL)
    inp['mla_q_a_norm'] = g(MLA_Q_RANK)
    inp['mla_kv_a_norm'] = g(MLA_KV_RANK)
    inp['mla_w_uq'] = w((MLA_Q_RANK, MLA_HEADS * MLA_QK), MLA_Q_RANK)
    inp['mla_w_ukv'] = w((MLA_KV_RANK, MLA_HEADS * (MLA_NOPE + MLA_V)), MLA_KV_RANK)
    inp['mla_q_norm'] = g(MLA_QK)
    inp['mla_k_norm'] = g(MLA_QK)
    inp['swa_q_norm'] = g(SWA_HEAD_DIM)
    inp['swa_k_norm'] = g(SWA_HEAD_DIM)
    inp['swa_sinks'] = jax.random.normal(next(ks), (DEPTH, SWA_HEADS), f32)
    inp['out_norm_mla'] = g(MLA_WIDTH)
    inp['out_norm_swa'] = g(SWA_WIDTH)
    inp['w_out'] = w((MIX_WIDTH, D_MODEL), MIX_WIDTH)
    inp['mem_attn_norm'] = g(D_MODEL)
    inp['mem_norm'] = g(D_MODEL)
    inp['mem_w_q'] = w((D_MODEL, MEM_HEADS * MEM_HEAD_DIM), D_MODEL)
    inp['mem_w_kv'] = w((D_MODEL, 2 * MEM_HEADS * MEM_HEAD_DIM), D_MODEL)
    inp['mem_q_norm'] = g(MEM_HEAD_DIM)
    inp['mem_k_norm'] = g(MEM_HEAD_DIM)
    inp['mem_w_o'] = w((MEM_HEADS * MEM_HEAD_DIM, D_MODEL), MEM_HEADS * MEM_HEAD_DIM)
    inp['ffn_b_norm'] = g(D_MODEL)
    inp['ffn_b_gate'] = w((D_MODEL, D_FF), D_MODEL)
    inp['ffn_b_up'] = w((D_MODEL, D_FF), D_MODEL)
    inp['ffn_b_down'] = w((D_FF, D_MODEL), D_FF)
    return inp


def reference(x, mem, rel_bias, ffn_a_norm, ffn_a_gate, ffn_a_up, ffn_a_down, mix_norm, w_in,
              mla_q_a_norm, mla_kv_a_norm, mla_w_uq, mla_w_ukv, mla_q_norm, mla_k_norm,
              swa_q_norm, swa_k_norm, swa_sinks, out_norm_mla, out_norm_swa, w_out,
              mem_attn_norm, mem_norm, mem_w_q, mem_w_kv, mem_q_norm, mem_k_norm, mem_w_o,
              ffn_b_norm, ffn_b_gate, ffn_b_up, ffn_b_down):
    h = x
    for l in range(DEPTH):
        p = dict(
            ffn_a_norm=ffn_a_norm[l], ffn_a_gate=ffn_a_gate[l], ffn_a_up=ffn_a_up[l], ffn_a_down=ffn_a_down[l],
            mix_norm=mix_norm[l], w_in=w_in[l],
            mla_q_a_norm=mla_q_a_norm[l], mla_kv_a_norm=mla_kv_a_norm[l],
            mla_w_uq=mla_w_uq[l], mla_w_ukv=mla_w_ukv[l],
            mla_q_norm=mla_q_norm[l], mla_k_norm=mla_k_norm[l],
            swa_q_norm=swa_q_norm[l], swa_k_norm=swa_k_norm[l], swa_sinks=swa_sinks[l],
            out_norm_mla=out_norm_mla[l], out_norm_swa=out_norm_swa[l], w_out=w_out[l],
            mem_attn_norm=mem_attn_norm[l], mem_norm=mem_norm[l],
            mem_w_q=mem_w_q[l], mem_w_kv=mem_w_kv[l],
            mem_q_norm=mem_q_norm[l], mem_k_norm=mem_k_norm[l], mem_w_o=mem_w_o[l],
            ffn_b_norm=ffn_b_norm[l], ffn_b_gate=ffn_b_gate[l], ffn_b_up=ffn_b_up[l], ffn_b_down=ffn_b_down[l],
        )
        h = hybrid_layer(h, mem, rel_bias, p)
    return h
```

```python
import functools

import jax
import jax.numpy as jnp
from jax import lax
from jax.experimental import pallas as pl
from jax.experimental.pallas import tpu as pltpu

F32 = jnp.float32
BF16 = jnp.bfloat16

MEM_HEADS = 4
MEM_HEAD_DIM = 128
MLA_HEADS = 16
MLA_Q_RANK = 896
MLA_KV_RANK = 512
MLA_NOPE = 128
MLA_ROPE = 64
MLA_V = 128
ROPE_THETA = 10000.0
SWA_HEADS = 32
SWA_KV_HEADS = 4
SWA_HEAD_DIM = 64
WINDOW = 128
REL_BUCKETS = 32
REL_MAX_DIST = 128
BLOCK_Q = 128
EPS = 1e-6
NEG = -1e30

MLA_QK = MLA_NOPE + MLA_ROPE
SWA_GROUP = SWA_HEADS // SWA_KV_HEADS
SWA_WIDTH = SWA_HEADS * SWA_HEAD_DIM
SWA_KV_WIDTH = SWA_KV_HEADS * SWA_HEAD_DIM
MLA_WIDTH = MLA_HEADS * MLA_V

LANES = 128
MLA_HEAD_PAD = 2 * LANES
ROPE_PAD = LANES

U_CKV = MLA_Q_RANK
U_KR = U_CKV + MLA_KV_RANK
U_SQ = U_KR + ROPE_PAD
U_SK = U_SQ + SWA_WIDTH
U_SV = U_SK + SWA_KV_WIDTH
U_WIDTH = U_SV + SWA_KV_WIDTH

V7X_SCOPED_VMEM_BYTES = 60000 * 1024


def _tile(n, pref, align=8):
    if n <= pref:
        return n
    t = (pref // align) * align
    while t >= align:
        if n % t == 0:
            return t
        t -= align
    return n


def _nbytes(shape, dtype):
    n = 1
    for s in shape:
        n *= s
    return n * jnp.dtype(dtype).itemsize


def _cparams(semantics, pipelined_bytes, temp_bytes=0):
    need = 2 * pipelined_bytes + temp_bytes + (4 << 20)
    return pltpu.CompilerParams(
        dimension_semantics=semantics,
        vmem_limit_bytes=int(min(max(need, 16 << 20), V7X_SCOPED_VMEM_BYTES)),
    )


def _rms_rows(x, g, n=None):
    n = x.shape[-1] if n is None else n
    r = lax.rsqrt(jnp.sum(x * x, axis=-1, keepdims=True) / float(n) + EPS)
    return x * r * g


def _norm_kernel(x_ref, g_ref, o_ref):
    o_ref[...] = _rms_rows(x_ref[...], g_ref[...]).astype(o_ref.dtype)


def _norm(x, g, out_dtype=BF16):
    rows, d = x.shape
    tr = _tile(rows, 512)
    blk = _nbytes((tr, d), F32) + _nbytes((tr, d), out_dtype)
    return pl.pallas_call(
        _norm_kernel,
        grid=(rows // tr,),
        in_specs=[pl.BlockSpec((tr, d), lambda i: (i, 0)), pl.BlockSpec((1, d), lambda i: (0, 0))],
        out_specs=pl.BlockSpec((tr, d), lambda i: (i, 0)),
        out_shape=jax.ShapeDtypeStruct((rows, d), out_dtype),
        compiler_params=_cparams(("parallel",), blk, _nbytes((tr, d), F32)),
        name="rmsnorm",
    )(x, g.reshape(1, d))


def _dual_norm_kernel(a_ref, b_ref, ga_ref, gb_ref, o_ref):
    wa = a_ref.shape[-1]
    o_ref[:, :wa] = _rms_rows(a_ref[...], ga_ref[...]).astype(o_ref.dtype)
    o_ref[:, wa:] = _rms_rows(b_ref[...], gb_ref[...]).astype(o_ref.dtype)


def _dual_norm(a, b, ga, gb):
    rows, wa = a.shape
    wb = b.shape[1]
    tr = _tile(rows, 512)
    blk = _nbytes((tr, wa + wb), F32) + _nbytes((tr, wa + wb), BF16)
    return pl.pallas_call(
        _dual_norm_kernel,
        grid=(rows // tr,),
        in_specs=[
            pl.BlockSpec((tr, wa), lambda i: (i, 0)),
            pl.BlockSpec((tr, wb), lambda i: (i, 0)),
            pl.BlockSpec((1, wa), lambda i: (0, 0)),
            pl.BlockSpec((1, wb), lambda i: (0, 0)),
        ],
        out_specs=pl.BlockSpec((tr, wa + wb), lambda i: (i, 0)),
        out_shape=jax.ShapeDtypeStruct((rows, wa + wb), BF16),
        compiler_params=_cparams(("parallel",), blk, _nbytes((tr, wa + wb), F32)),
        name="out_norms",
    )(a, b, ga.reshape(1, wa), gb.reshape(1, wb))


def _mm_kernel(a_ref, w_ref, o_ref):
    o_ref[...] = jnp.dot(a_ref[...], w_ref[...], preferred_element_type=F32).astype(o_ref.dtype)


def _mm_res_kernel(a_ref, w_ref, r_ref, o_ref):
    o_ref[...] = r_ref[...] + jnp.dot(a_ref[...], w_ref[...], preferred_element_type=F32)


def _mm(a, w, res=None, *, tm, tn, out_dtype=F32, name="matmul"):
    m, k = a.shape
    n = w.shape[1]
    tm, tn = _tile(m, tm), _tile(n, tn, LANES)
    blk = _nbytes((tm, k), a.dtype) + _nbytes((k, tn), w.dtype) + _nbytes((tm, tn), out_dtype)
    in_specs = [pl.BlockSpec((tm, k), lambda i, j: (i, 0)), pl.BlockSpec((k, tn), lambda i, j: (0, j))]
    args = [a, w]
    kern = _mm_kernel
    if res is not None:
        in_specs.append(pl.BlockSpec((tm, tn), lambda i, j: (i, j)))
        args.append(res)
        blk += _nbytes((tm, tn), F32)
        kern = _mm_res_kernel
    return pl.pallas_call(
        kern,
        grid=(m // tm, n // tn),
        in_specs=in_specs,
        out_specs=pl.BlockSpec((tm, tn), lambda i, j: (i, j)),
        out_shape=jax.ShapeDtypeStruct((m, n), out_dtype),
        compiler_params=_cparams(("parallel", "arbitrary"), blk, 2 * _nbytes((tm, tn), F32)),
        name=name,
    )(*args)


def _gated_kernel(x_ref, wg_ref, wu_ref, o_ref):
    x = x_ref[...]
    g = jnp.dot(x, wg_ref[...], preferred_element_type=F32)
    u = jnp.dot(x, wu_ref[...], preferred_element_type=F32)
    o_ref[...] = (0.5 * (g / (1.0 + jnp.exp(-g))) * u).astype(o_ref.dtype)


def _gated(x, wg, wu, *, tm, tn):
    m, k = x.shape
    n = wg.shape[1]
    tm, tn = _tile(m, tm), _tile(n, tn, LANES)
    blk = _nbytes((tm, k), BF16) + 2 * _nbytes((k, tn), BF16) + _nbytes((tm, tn), BF16)
    return pl.pallas_call(
        _gated_kernel,
        grid=(m // tm, n // tn),
        in_specs=[
            pl.BlockSpec((tm, k), lambda i, j: (i, 0)),
            pl.BlockSpec((k, tn), lambda i, j: (0, j)),
            pl.BlockSpec((k, tn), lambda i, j: (0, j)),
        ],
        out_specs=pl.BlockSpec((tm, tn), lambda i, j: (i, j)),
        out_shape=jax.ShapeDtypeStruct((m, n), BF16),
        compiler_params=_cparams(("parallel", "arbitrary"), blk, 4 * _nbytes((tm, tn), F32)),
        name="ffn_gate_up",
    )(x, wg, wu)


def _swiglu_half(h, norm_g, wg, wu, wd):
    xn = _norm(h, norm_g)
    act = _gated(xn, wg, wu, tm=1024, tn=256)
    return _mm(act, wd, h, tm=512, tn=512, name="ffn_down")


def _rope_chunk(y, cos, s1, s2):
    return y * cos + pltpu.roll(y, ROPE_PAD - MLA_ROPE // 2, 1) * s1 + pltpu.roll(y, MLA_ROPE // 2, 1) * s2


def _mla_q_kernel(u_ref, ga_ref, w_ref, gh_ref, cos_ref, s1_ref, s2_ref, o_ref, *, scale):
    cn = _rms_rows(u_ref[:, :MLA_Q_RANK], ga_ref[...]).astype(BF16)
    q = jnp.dot(cn, w_ref[...], preferred_element_type=F32)
    cos, s1, s2 = cos_ref[...], s1_ref[...], s2_ref[...]
    gh = gh_ref[...] * scale
    for h in range(MLA_HEADS):
        lo = h * MLA_HEAD_PAD
        y = _rms_rows(q[:, lo:lo + MLA_HEAD_PAD], gh, MLA_QK)
        o_ref[:, lo:lo + MLA_NOPE] = y[:, :MLA_NOPE].astype(o_ref.dtype)
        o_ref[:, lo + MLA_NOPE:lo + MLA_HEAD_PAD] = _rope_chunk(y[:, MLA_NOPE:], cos, s1, s2).astype(o_ref.dtype)


def _mla_kv_kernel(u_ref, ga_ref, w_ref, gn_ref, gr_ref, cos_ref, s1_ref, s2_ref, k_ref, v_ref):
    cn = _rms_rows(u_ref[:, U_CKV:U_KR], ga_ref[...]).astype(BF16)
    kv = jnp.dot(cn, w_ref[...], preferred_element_type=F32)
    v_ref[...] = kv[:, MLA_HEADS * MLA_NOPE:].astype(v_ref.dtype)
    kr = u_ref[:, U_KR:U_SQ]
    ss_r = jnp.sum(kr * kr, axis=-1, keepdims=True)
    cos, s1, s2 = cos_ref[...], s1_ref[...], s2_ref[...]
    gn, gr = gn_ref[...], gr_ref[...]
    for h in range(MLA_HEADS):
        kn = kv[:, h * MLA_NOPE:(h + 1) * MLA_NOPE]
        r = lax.rsqrt((jnp.sum(kn * kn, axis=-1, keepdims=True) + ss_r) / float(MLA_QK) + EPS)
        lo = h * MLA_HEAD_PAD
        k_ref[:, lo:lo + MLA_NOPE] = (kn * r * gn).astype(k_ref.dtype)
        k_ref[:, lo + MLA_NOPE:lo + MLA_HEAD_PAD] = _rope_chunk(kr * r * gr, cos, s1, s2).astype(k_ref.dtype)


def _rope_tables(seq):
    half = MLA_ROPE // 2
    inv = ROPE_THETA ** (-jnp.arange(half, dtype=F32) / half)
    ang = jnp.arange(seq).astype(F32)[:, None] * inv[None, :]
    cos, sin = jnp.cos(ang), jnp.sin(ang)
    z = jnp.zeros_like(cos)
    pad = jnp.zeros((seq, ROPE_PAD - MLA_ROPE), F32)
    return (jnp.concatenate([cos, cos, pad], 1), jnp.concatenate([-sin, z, pad], 1),
            jnp.concatenate([z, sin, pad], 1))


def _mla_prep(u, seq, q_a_norm, kv_a_norm, w_uq, w_ukv, q_norm, k_norm):
    t = u.shape[0]
    tm = _tile(seq, 256)
    spb = seq // tm
    cos, s1, s2 = _rope_tables(seq)
    tab_spec = pl.BlockSpec((tm, ROPE_PAD), lambda i: (i % spb, 0))
    zpad = jnp.zeros((MLA_HEAD_PAD - MLA_QK,), F32)

    wq = jnp.pad(w_uq.reshape(MLA_Q_RANK, MLA_HEADS, MLA_QK), ((0, 0), (0, 0), (0, MLA_HEAD_PAD - MLA_QK)))
    wq = wq.reshape(MLA_Q_RANK, MLA_HEADS * MLA_HEAD_PAD).astype(BF16)
    qw = MLA_HEADS * MLA_HEAD_PAD
    ublk = 8 * LANES
    blk = _nbytes((tm, ublk), F32) + _nbytes(wq.shape, BF16) + _nbytes((tm, qw), BF16) + 3 * _nbytes((tm, ROPE_PAD), F32)
    q = pl.pallas_call(
        functools.partial(_mla_q_kernel, scale=MLA_QK ** -0.5),
        grid=(t // tm,),
        in_specs=[
            pl.BlockSpec((tm, ublk), lambda i: (i, 0)),
            pl.BlockSpec((1, MLA_Q_RANK), lambda i: (0, 0)),
            pl.BlockSpec(wq.shape, lambda i: (0, 0)),
            pl.BlockSpec((1, MLA_HEAD_PAD), lambda i: (0, 0)),
            tab_spec, tab_spec, tab_spec,
        ],
        out_specs=pl.BlockSpec((tm, qw), lambda i: (i, 0)),
        out_shape=jax.ShapeDtypeStruct((t, qw), BF16),
        compiler_params=_cparams(("parallel",), blk, 3 * _nbytes((tm, qw), F32)),
        name="mla_q_prep",
    )(u, q_a_norm.reshape(1, -1), wq, jnp.concatenate([q_norm, zpad]).reshape(1, -1), cos, s1, s2)

    wkv = w_ukv.reshape(MLA_KV_RANK, MLA_HEADS, MLA_NOPE + MLA_V)
    wkv = jnp.concatenate([wkv[:, :, :MLA_NOPE].reshape(MLA_KV_RANK, -1), wkv[:, :, MLA_NOPE:].reshape(MLA_KV_RANK, -1)], 1)
    wkv = wkv.astype(BF16)
    ublk = U_SQ
    vw = MLA_HEADS * MLA_V
    blk = (_nbytes((tm, ublk), F32) + _nbytes(wkv.shape, BF16) + _nbytes((tm, qw), BF16) + _nbytes((tm, vw), BF16)
           + 3 * _nbytes((tm, ROPE_PAD), F32))
    k, v = pl.pallas_call(
        _mla_kv_kernel,
        grid=(t // tm,),
        in_specs=[
            pl.BlockSpec((tm, ublk), lambda i: (i, 0)),
            pl.BlockSpec((1, MLA_KV_RANK), lambda i: (0, 0)),
            pl.BlockSpec(wkv.shape, lambda i: (0, 0)),
            pl.BlockSpec((1, MLA_NOPE), lambda i: (0, 0)),
            pl.BlockSpec((1, ROPE_PAD), lambda i: (0, 0)),
            tab_spec, tab_spec, tab_spec,
        ],
        out_specs=[pl.BlockSpec((tm, qw), lambda i: (i, 0)), pl.BlockSpec((tm, vw), lambda i: (i, 0))],
        out_shape=[jax.ShapeDtypeStruct((t, qw), BF16), jax.ShapeDtypeStruct((t, vw), BF16)],
        compiler_params=_cparams(("parallel",), blk, 3 * _nbytes((tm, qw), F32)),
        name="mla_kv_prep",
    )(u, kv_a_norm.reshape(1, -1), wkv, k_norm[:MLA_NOPE].reshape(1, -1),
      jnp.concatenate([k_norm[MLA_NOPE:], jnp.zeros((ROPE_PAD - MLA_ROPE,), F32)]).reshape(1, -1), cos, s1, s2)
    return q, k, v


def _mla_attn_kernel(q_ref, k_ref, v_ref, o_ref, *, tq):
    qi = pl.program_id(2)
    q = q_ref[...]

    def step(kblk, vblk, carry, diagonal):
        m, l, acc = carry
        s = lax.dot_general(q, kblk, (((1,), (1,)), ((), ())), preferred_element_type=F32)
        if diagonal:
            row = lax.broadcasted_iota(jnp.int32, s.shape, 0)
            col = lax.broadcasted_iota(jnp.int32, s.shape, 1)
            s = jnp.where(col <= row, s, NEG)
        m_new = jnp.maximum(m, jnp.max(s, axis=-1, keepdims=True))
        alpha = jnp.exp(m - m_new)
        p = jnp.exp(s - m_new)
        l = alpha * l + jnp.sum(p, axis=-1, keepdims=True)
        acc = alpha * acc + jnp.dot(p.astype(BF16), vblk, preferred_element_type=F32)
        return m_new, l, acc

    def body(ki, carry):
        start = pl.multiple_of(ki * tq, tq)
        return step(k_ref[pl.ds(start, tq), :], v_ref[pl.ds(start, tq), :], carry, False)

    init = (jnp.full((tq, 1), NEG, F32), jnp.zeros((tq, 1), F32), jnp.zeros((tq, MLA_V), F32))
    carry = lax.fori_loop(0, qi, body, init)
    start = pl.multiple_of(qi * tq, tq)
    _, l, acc = step(k_ref[pl.ds(start, tq), :], v_ref[pl.ds(start, tq), :], carry, True)
    o_ref[...] = acc / l


def _mla_attention(q, k, v, batch, seq):
    t = q.shape[0]
    tq = _tile(seq, 512)
    nq = seq // tq
    blk = _nbytes((tq, MLA_HEAD_PAD), BF16) + _nbytes((seq, MLA_HEAD_PAD), BF16) + _nbytes((seq, MLA_V), BF16) + _nbytes((tq, MLA_V), F32)
    return pl.pallas_call(
        functools.partial(_mla_attn_kernel, tq=tq),
        grid=(batch, MLA_HEADS, nq),
        in_specs=[
            pl.BlockSpec((tq, MLA_HEAD_PAD), lambda b, h, i: (b * nq + i, h)),
            pl.BlockSpec((seq, MLA_HEAD_PAD), lambda b, h, i: (b, h)),
            pl.BlockSpec((seq, MLA_V), lambda b, h, i: (b, h)),
        ],
        out_specs=pl.BlockSpec((tq, MLA_V), lambda b, h, i: (b * nq + i, h)),
        out_shape=jax.ShapeDtypeStruct((t, MLA_WIDTH), F32),
        compiler_params=_cparams(("parallel", "parallel", "arbitrary"), blk, 6 * _nbytes((tq, tq), F32)),
        name="mla_attention",
    )(q, k, v)


def _t5_bucket(dist):
    max_exact = REL_BUCKETS // 2
    d = jnp.maximum(dist, 1).astype(F32)
    large = max_exact + (jnp.log(d / max_exact) / jnp.log(REL_MAX_DIST / max_exact)
                         * (REL_BUCKETS - max_exact)).astype(jnp.int32)
    large = jnp.minimum(large, REL_BUCKETS - 1)
    return jnp.where(dist < max_exact, dist, large)


def _swa_kernel(sink_ref, q0_ref, q1_ref, q2_ref, q3_ref, kp_ref, kc_ref, vp_ref, vc_ref, gq_ref, gk_ref,
                bias_ref, o_ref):
    n = pl.program_id(1)
    shape = (BLOCK_Q, 2 * BLOCK_Q)
    row = lax.broadcasted_iota(jnp.int32, shape, 0)
    col = lax.broadcasted_iota(jnp.int32, shape, 1)
    dist = row + BLOCK_Q - col
    valid = (dist >= 0) & (dist < WINDOW) & ((col >= BLOCK_Q) | (n > 0))
    k_all = jnp.concatenate([kp_ref[...], kc_ref[...]], axis=0)
    v_all = jnp.concatenate([vp_ref[...], vc_ref[...]], axis=0).astype(BF16)
    gq = gq_ref[...] * (SWA_HEAD_DIM ** -0.5)
    gk = gk_ref[...]
    q_refs = (q0_ref, q1_ref, q2_ref, q3_ref)
    outs = []
    for kh in range(SWA_KV_HEADS):
        lo = kh * SWA_HEAD_DIM
        kn = _rms_rows(k_all[:, lo:lo + SWA_HEAD_DIM], gk).astype(BF16)
        v = v_all[:, lo:lo + SWA_HEAD_DIM]
        qblk = q_refs[kh][...]
        for g in range(SWA_GROUP):
            h = kh * SWA_GROUP + g
            qn = _rms_rows(qblk[:, g * SWA_HEAD_DIM:(g + 1) * SWA_HEAD_DIM], gq).astype(BF16)
            s = lax.dot_general(qn, kn, (((1,), (1,)), ((), ())), preferred_element_type=F32)
            s = jnp.where(valid, s + bias_ref[h], NEG)
            sink = sink_ref[h]
            m = jnp.maximum(jnp.max(s, axis=-1, keepdims=True), sink)
            e = jnp.exp(s - m)
            den = jnp.sum(e, axis=-1, keepdims=True) + jnp.exp(sink - m)
            outs.append(jnp.dot((e / den).astype(BF16), v, preferred_element_type=F32))
    o_ref[...] = jnp.concatenate(outs, axis=-1)


def _swa_attention(u, batch, seq, q_norm, k_norm, sinks, rel_bias):
    t = u.shape[0]
    nb = seq // BLOCK_Q
    dist = jnp.arange(BLOCK_Q)[:, None] + BLOCK_Q - jnp.arange(2 * BLOCK_Q)[None, :]
    bias = rel_bias.astype(F32)[_t5_bucket(jnp.maximum(dist, 0))].transpose(2, 0, 1)
    gw = SWA_GROUP * SWA_HEAD_DIM
    q_specs = [pl.BlockSpec((BLOCK_Q, gw), functools.partial(lambda b, n, c: (b * nb + n, c), c=U_SQ // gw + kh))
               for kh in range(SWA_KV_HEADS)]
    cur = lambda c: pl.BlockSpec((BLOCK_Q, SWA_KV_WIDTH), lambda b, n: (b * nb + n, c))
    prev = lambda c: pl.BlockSpec((BLOCK_Q, SWA_KV_WIDTH), lambda b, n: (b * nb + jnp.maximum(n - 1, 0), c))
    ck, cv = U_SK // SWA_KV_WIDTH, U_SV // SWA_KV_WIDTH
    blk = (4 * _nbytes((BLOCK_Q, gw), F32) + 4 * _nbytes((BLOCK_Q, SWA_KV_WIDTH), F32) + _nbytes(bias.shape, F32)
           + _nbytes((BLOCK_Q, SWA_WIDTH), F32))
    return pl.pallas_call(
        _swa_kernel,
        grid=(batch, nb),
        in_specs=[pl.BlockSpec(memory_space=pltpu.SMEM)] + q_specs + [
            prev(ck), cur(ck), prev(cv), cur(cv),
            pl.BlockSpec((1, SWA_HEAD_DIM), lambda b, n: (0, 0)),
            pl.BlockSpec((1, SWA_HEAD_DIM), lambda b, n: (0, 0)),
            pl.BlockSpec(bias.shape, lambda b, n: (0, 0, 0)),
        ],
        out_specs=pl.BlockSpec((BLOCK_Q, SWA_WIDTH), lambda b, n: (b * nb + n, 0)),
        out_shape=jax.ShapeDtypeStruct((t, SWA_WIDTH), F32),
        compiler_params=_cparams(("parallel", "arbitrary"), blk, 4 << 20),
        name="swa_attention",
    )(sinks, u, u, u, u, u, u, u, u, q_norm.reshape(1, -1), k_norm.reshape(1, -1), bias)


def _mem_attn_kernel(q_ref, kv_ref, gq_ref, gk_ref, o_ref):
    gq = gq_ref[...] * (MEM_HEAD_DIM ** -0.5)
    gk = gk_ref[...]
    for h in range(MEM_HEADS):
        q = _rms_rows(q_ref[:, h * MEM_HEAD_DIM:(h + 1) * MEM_HEAD_DIM], gq).astype(BF16)
        lo = 2 * h * MEM_HEAD_DIM
        k = _rms_rows(kv_ref[:, lo:lo + MEM_HEAD_DIM], gk).astype(BF16)
        v = kv_ref[:, lo + MEM_HEAD_DIM:lo + 2 * MEM_HEAD_DIM].astype(BF16)
        s = lax.dot_general(q, k, (((1,), (1,)), ((), ())), preferred_element_type=F32)
        e = jnp.exp(s - jnp.max(s, axis=-1, keepdims=True))
        p = (e / jnp.sum(e, axis=-1, keepdims=True)).astype(BF16)
        o_ref[:, h * MEM_HEAD_DIM:(h + 1) * MEM_HEAD_DIM] = jnp.dot(p, v, preferred_element_type=F32).astype(o_ref.dtype)


def _mem_attention(qm, kvm, batch, seq, mem_len, q_norm, k_norm):
    t, w = qm.shape
    tq = _tile(seq, 512)
    nq = seq // tq
    blk = _nbytes((tq, w), F32) + _nbytes((mem_len, 2 * w), F32) + _nbytes((tq, w), BF16)
    return pl.pallas_call(
        _mem_attn_kernel,
        grid=(batch, nq),
        in_specs=[
            pl.BlockSpec((tq, w), lambda b, i: (b * nq + i, 0)),
            pl.BlockSpec((mem_len, 2 * w), lambda b, i: (b, 0)),
            pl.BlockSpec((1, MEM_HEAD_DIM), lambda b, i: (0, 0)),
            pl.BlockSpec((1, MEM_HEAD_DIM), lambda b, i: (0, 0)),
        ],
        out_specs=pl.BlockSpec((tq, w), lambda b, i: (b * nq + i, 0)),
        out_shape=jax.ShapeDtypeStruct((t, w), BF16),
        compiler_params=_cparams(("parallel", "arbitrary"), blk, 8 * _nbytes((tq, mem_len), F32)),
        name="mem_attention",
    )(qm, kvm, q_norm.reshape(1, -1), k_norm.reshape(1, -1))


def _layer(h, mem, rel_bias, p, batch, seq):
    d = h.shape[1]
    bf = lambda name: p[name].astype(BF16)

    h = _swiglu_half(h, p['ffn_a_norm'], bf('ffn_a_gate'), bf('ffn_a_up'), bf('ffn_a_down'))

    w_in = p['w_in']
    cut = U_KR + MLA_ROPE
    w_in = jnp.concatenate([w_in[:, :cut], jnp.zeros((d, ROPE_PAD - MLA_ROPE), w_in.dtype), w_in[:, cut:]], 1)
    u = _mm(_norm(h, p['mix_norm']), w_in.astype(BF16), tm=1024, tn=512, name="in_proj")

    q, k, v = _mla_prep(u, seq, p['mla_q_a_norm'], p['mla_kv_a_norm'], p['mla_w_uq'], p['mla_w_ukv'],
                        p['mla_q_norm'], p['mla_k_norm'])
    o_a = _mla_attention(q, k, v, batch, seq)
    o_b = _swa_attention(u, batch, seq, p['swa_q_norm'], p['swa_k_norm'], p['swa_sinks'], rel_bias)
    o = _dual_norm(o_a, o_b, p['out_norm_mla'], p['out_norm_swa'])
    h = _mm(o, bf('w_out'), h, tm=1024, tn=512, name="out_proj")

    mem_len = mem.shape[0] // batch
    qm = _mm(_norm(h, p['mem_attn_norm']), bf('mem_w_q'), tm=1024, tn=512, name="mem_q_proj")
    kvm = _mm(_norm(mem, p['mem_norm']), bf('mem_w_kv'), tm=512, tn=512, name="mem_kv_proj")
    om = _mem_attention(qm, kvm, batch, seq, mem_len, p['mem_q_norm'], p['mem_k_norm'])
    h = _mm(om, bf('mem_w_o'), h, tm=1024, tn=512, name="mem_o_proj")

    return _swiglu_half(h, p['ffn_b_norm'], bf('ffn_b_gate'), bf('ffn_b_up'), bf('ffn_b_down'))


_PARAM_NAMES = (
    'ffn_a_norm', 'ffn_a_gate', 'ffn_a_up', 'ffn_a_down', 'mix_norm', 'w_in',
    'mla_q_a_norm', 'mla_kv_a_norm', 'mla_w_uq', 'mla_w_ukv', 'mla_q_norm', 'mla_k_norm',
    'swa_q_norm', 'swa_k_norm', 'swa_sinks', 'out_norm_mla', 'out_norm_swa', 'w_out',
    'mem_attn_norm', 'mem_norm', 'mem_w_q', 'mem_w_kv', 'mem_q_norm', 'mem_k_norm', 'mem_w_o',
    'ffn_b_norm', 'ffn_b_gate', 'ffn_b_up', 'ffn_b_down',
)


def kernel(x, mem, rel_bias, ffn_a_norm, ffn_a_gate, ffn_a_up, ffn_a_down, mix_norm, w_in, mla_q_a_norm, mla_kv_a_norm, mla_w_uq, mla_w_ukv, mla_q_norm, mla_k_norm, swa_q_norm, swa_k_norm, swa_sinks, out_norm_mla, out_norm_swa, w_out, mem_attn_norm, mem_norm, mem_w_q, mem_w_kv, mem_q_norm, mem_k_norm, mem_w_o, ffn_b_norm, ffn_b_gate, ffn_b_up, ffn_b_down):
    stacked = (ffn_a_norm, ffn_a_gate, ffn_a_up, ffn_a_down, mix_norm, w_in, mla_q_a_norm, mla_kv_a_norm,
               mla_w_uq, mla_w_ukv, mla_q_norm, mla_k_norm, swa_q_norm, swa_k_norm, swa_sinks, out_norm_mla,
               out_norm_swa, w_out, mem_attn_norm, mem_norm, mem_w_q, mem_w_kv, mem_q_norm, mem_k_norm, mem_w_o,
               ffn_b_norm, ffn_b_gate, ffn_b_up, ffn_b_down)
    batch, seq, d = x.shape
    h = x.reshape(batch * seq, d)
    mem2 = mem.reshape(-1, d)
    for layer in range(ffn_a_norm.shape[0]):
        p = {name: arr[layer] for name, arr in zip(_PARAM_NAMES, stacked)}
        h = _layer(h, mem2, rel_bias, p, batch, seq)
    return h.reshape(batch, seq, d)
```

```python
import functools

import jax
import jax.numpy as jnp
from jax import lax
from jax.experimental import pallas as pl
from jax.experimental.pallas import tpu as pltpu

F32 = jnp.float32
BF16 = jnp.bfloat16

MEM_HEADS = 4
MEM_HEAD_DIM = 128
MLA_HEADS = 16
MLA_Q_RANK = 896
MLA_KV_RANK = 512
MLA_NOPE = 128
MLA_ROPE = 64
MLA_V = 128
ROPE_THETA = 10000.0
SWA_HEADS = 32
SWA_KV_HEADS = 4
SWA_HEAD_DIM = 64
WINDOW = 128
REL_BUCKETS = 32
REL_MAX_DIST = 128
BLOCK_Q = 128
EPS = 1e-6
NEG = -1e30

MLA_QK = MLA_NOPE + MLA_ROPE
SWA_GROUP = SWA_HEADS // SWA_KV_HEADS
SWA_WIDTH = SWA_HEADS * SWA_HEAD_DIM
SWA_KV_WIDTH = SWA_KV_HEADS * SWA_HEAD_DIM
MLA_WIDTH = MLA_HEADS * MLA_V

LANES = 128
MLA_HEAD_PAD = 2 * LANES
ROPE_PAD = LANES

U_CKV = MLA_Q_RANK
U_KR = U_CKV + MLA_KV_RANK
U_MLA = U_KR + ROPE_PAD
IN_SWA = U_KR + MLA_ROPE
S_K = SWA_WIDTH
S_V = S_K + SWA_KV_WIDTH

V7X_SCOPED_VMEM_BYTES = 60000 * 1024


def _tile(n, pref, align=8):
    if n <= pref:
        return n
    t = (pref // align) * align
    while t >= align:
        if n % t == 0:
            return t
        t -= align
    return n


def _nbytes(shape, dtype):
    n = 1
    for s in shape:
        n *= s
    return n * jnp.dtype(dtype).itemsize


def _cparams(semantics, pipelined_bytes, temp_bytes=0):
    need = 2 * pipelined_bytes + temp_bytes + (4 << 20)
    return pltpu.CompilerParams(
        dimension_semantics=semantics,
        vmem_limit_bytes=int(min(max(need, 16 << 20), V7X_SCOPED_VMEM_BYTES)),
    )


def _rms_rows(x, g, n=None):
    n = x.shape[-1] if n is None else n
    r = lax.rsqrt(jnp.sum(x * x, axis=-1, keepdims=True) / float(n) + EPS)
    return x * r * g


def _norm_kernel(x_ref, g_ref, o_ref):
    o_ref[...] = _rms_rows(x_ref[...], g_ref[...]).astype(o_ref.dtype)


def _norm(x, g, out_dtype=BF16):
    rows, d = x.shape
    tr = _tile(rows, 512)
    blk = _nbytes((tr, d), F32) + _nbytes((tr, d), out_dtype)
    return pl.pallas_call(
        _norm_kernel,
        grid=(rows // tr,),
        in_specs=[pl.BlockSpec((tr, d), lambda i: (i, 0)), pl.BlockSpec((1, d), lambda i: (0, 0))],
        out_specs=pl.BlockSpec((tr, d), lambda i: (i, 0)),
        out_shape=jax.ShapeDtypeStruct((rows, d), out_dtype),
        compiler_params=_cparams(("parallel",), blk, _nbytes((tr, d), F32)),
        name="rmsnorm",
    )(x, g.reshape(1, d))


def _dual_norm_kernel(a_ref, b_ref, ga_ref, gb_ref, o_ref):
    wa = a_ref.shape[-1]
    o_ref[:, :wa] = _rms_rows(a_ref[...], ga_ref[...]).astype(o_ref.dtype)
    o_ref[:, wa:] = _rms_rows(b_ref[...], gb_ref[...]).astype(o_ref.dtype)


def _dual_norm(a, b, ga, gb):
    rows, wa = a.shape
    wb = b.shape[1]
    tr = _tile(rows, 512)
    blk = _nbytes((tr, wa + wb), F32) + _nbytes((tr, wa + wb), BF16)
    return pl.pallas_call(
        _dual_norm_kernel,
        grid=(rows // tr,),
        in_specs=[
            pl.BlockSpec((tr, wa), lambda i: (i, 0)),
            pl.BlockSpec((tr, wb), lambda i: (i, 0)),
            pl.BlockSpec((1, wa), lambda i: (0, 0)),
            pl.BlockSpec((1, wb), lambda i: (0, 0)),
        ],
        out_specs=pl.BlockSpec((tr, wa + wb), lambda i: (i, 0)),
        out_shape=jax.ShapeDtypeStruct((rows, wa + wb), BF16),
        compiler_params=_cparams(("parallel",), blk, _nbytes((tr, wa + wb), F32)),
        name="out_norms",
    )(a, b, ga.reshape(1, wa), gb.reshape(1, wb))


def _mm_kernel(a_ref, w_ref, o_ref):
    o_ref[...] = jnp.dot(a_ref[...], w_ref[...], preferred_element_type=F32).astype(o_ref.dtype)


def _mm_res_kernel(a_ref, w_ref, r_ref, o_ref):
    o_ref[...] = r_ref[...] + jnp.dot(a_ref[...], w_ref[...], preferred_element_type=F32)


def _mm(a, w, res=None, *, tm, tn, out_dtype=F32, name="matmul"):
    m, k = a.shape
    n = w.shape[1]
    tm, tn = _tile(m, tm), _tile(n, tn, LANES)
    blk = _nbytes((tm, k), a.dtype) + _nbytes((k, tn), w.dtype) + _nbytes((tm, tn), out_dtype)
    in_specs = [pl.BlockSpec((tm, k), lambda i, j: (i, 0)), pl.BlockSpec((k, tn), lambda i, j: (0, j))]
    args = [a, w]
    kern = _mm_kernel
    if res is not None:
        in_specs.append(pl.BlockSpec((tm, tn), lambda i, j: (i, j)))
        args.append(res)
        blk += _nbytes((tm, tn), F32)
        kern = _mm_res_kernel
    return pl.pallas_call(
        kern,
        grid=(m // tm, n // tn),
        in_specs=in_specs,
        out_specs=pl.BlockSpec((tm, tn), lambda i, j: (i, j)),
        out_shape=jax.ShapeDtypeStruct((m, n), out_dtype),
        compiler_params=_cparams(("parallel", "arbitrary"), blk, 2 * _nbytes((tm, tn), F32)),
        name=name,
    )(*args)


def _mm_w32_kernel(a_ref, w_ref, *rest, has_res):
    if has_res:
        r_ref, o_ref, wb_ref = rest
    else:
        o_ref, wb_ref = rest

    @pl.when(pl.program_id(1) == 0)
    def _():
        wb_ref[...] = w_ref[...].astype(BF16)

    acc = jnp.dot(a_ref[...], wb_ref[...], preferred_element_type=F32)
    if has_res:
        acc = r_ref[...] + acc
    o_ref[...] = acc.astype(o_ref.dtype)


def _mm_w32(a, w, res=None, *, tm, tn, n=None, out_dtype=F32, name="matmul"):
    m, k = a.shape
    n = w.shape[1] if n is None else n
    tm, tn = _tile(m, tm), _tile(n, tn, LANES)
    blk = _nbytes((tm, k), a.dtype) + _nbytes((k, tn), F32) + _nbytes((tm, tn), out_dtype)
    in_specs = [pl.BlockSpec((tm, k), lambda j, i: (i, 0)), pl.BlockSpec((k, tn), lambda j, i: (0, j))]
    args = [a, w]
    if res is not None:
        in_specs.append(pl.BlockSpec((tm, tn), lambda j, i: (i, j)))
        args.append(res)
        blk += _nbytes((tm, tn), F32)
    return pl.pallas_call(
        functools.partial(_mm_w32_kernel, has_res=res is not None),
        grid=(n // tn, m // tm),
        in_specs=in_specs,
        out_specs=pl.BlockSpec((tm, tn), lambda j, i: (i, j)),
        out_shape=jax.ShapeDtypeStruct((m, n), out_dtype),
        scratch_shapes=[pltpu.VMEM((k, tn), BF16)],
        compiler_params=_cparams(("parallel", "arbitrary"), blk, _nbytes((k, tn), BF16) + 2 * _nbytes((tm, tn), F32)),
        name=name,
    )(*args)


def _gated_kernel(x_ref, wg_ref, wu_ref, wd_ref, o_ref, wdb_ref, wgb_ref, wub_ref):
    @pl.when(pl.program_id(1) == 0)
    def _():
        wgb_ref[...] = wg_ref[...].astype(BF16)
        wub_ref[...] = wu_ref[...].astype(BF16)
        wdb_ref[...] = wd_ref[...].astype(BF16)

    x = x_ref[...]
    g = jnp.dot(x, wgb_ref[...], preferred_element_type=F32)
    u = jnp.dot(x, wub_ref[...], preferred_element_type=F32)
    o_ref[...] = (0.5 * (g / (1.0 + jnp.exp(-g))) * u).astype(o_ref.dtype)


def _gated(x, wg, wu, wd, *, tm, tn):
    m, k = x.shape
    n = wg.shape[1]
    d_out = wd.shape[1]
    tm, tn = _tile(m, tm), _tile(n, tn, LANES)
    blk = (_nbytes((tm, k), BF16) + 2 * _nbytes((k, tn), F32) + _nbytes((tm, tn), BF16)
           + _nbytes((tn, d_out), F32) + _nbytes((tn, d_out), BF16))
    return pl.pallas_call(
        _gated_kernel,
        grid=(n // tn, m // tm),
        in_specs=[
            pl.BlockSpec((tm, k), lambda j, i: (i, 0)),
            pl.BlockSpec((k, tn), lambda j, i: (0, j)),
            pl.BlockSpec((k, tn), lambda j, i: (0, j)),
            pl.BlockSpec((tn, d_out), lambda j, i: (j, 0)),
        ],
        out_specs=[pl.BlockSpec((tm, tn), lambda j, i: (i, j)), pl.BlockSpec((tn, d_out), lambda j, i: (j, 0))],
        out_shape=[jax.ShapeDtypeStruct((m, n), BF16), jax.ShapeDtypeStruct((n, d_out), BF16)],
        scratch_shapes=[pltpu.VMEM((k, tn), BF16), pltpu.VMEM((k, tn), BF16)],
        compiler_params=_cparams(("parallel", "arbitrary"), blk, 2 * _nbytes((k, tn), BF16) + 4 * _nbytes((tm, tn), F32)),
        name="ffn_gate_up",
    )(x, wg, wu, wd)


def _swiglu_half(h, norm_g, wg, wu, wd):
    xn = _norm(h, norm_g)
    act, wd_bf = _gated(xn, wg, wu, wd, tm=1024, tn=256)
    return _mm(act, wd_bf, h, tm=512, tn=512, name="ffn_down")


def _rope_chunk(y, cos, s1, s2):
    return y * cos + pltpu.roll(y, ROPE_PAD - MLA_ROPE // 2, 1) * s1 + pltpu.roll(y, MLA_ROPE // 2, 1) * s2


def _mla_q_kernel(u_ref, ga_ref, w_ref, gh_ref, cos_ref, s1_ref, s2_ref, o_ref, *, scale):
    cn = _rms_rows(u_ref[:, :MLA_Q_RANK], ga_ref[...]).astype(BF16)
    q = jnp.dot(cn, w_ref[...], preferred_element_type=F32)
    cos, s1, s2 = cos_ref[...], s1_ref[...], s2_ref[...]
    gh = gh_ref[...] * scale
    for h in range(MLA_HEADS):
        lo = h * MLA_HEAD_PAD
        y = _rms_rows(q[:, lo:lo + MLA_HEAD_PAD], gh, MLA_QK)
        o_ref[:, lo:lo + MLA_NOPE] = y[:, :MLA_NOPE].astype(o_ref.dtype)
        o_ref[:, lo + MLA_NOPE:lo + MLA_HEAD_PAD] = _rope_chunk(y[:, MLA_NOPE:], cos, s1, s2).astype(o_ref.dtype)


def _mla_kv_kernel(u_ref, ga_ref, w_ref, gn_ref, gr_ref, cos_ref, s1_ref, s2_ref, k_ref, v_ref):
    cn = _rms_rows(u_ref[:, U_CKV:U_KR], ga_ref[...]).astype(BF16)
    kv = jnp.dot(cn, w_ref[...], preferred_element_type=F32)
    v_ref[...] = kv[:, MLA_HEADS * MLA_NOPE:].astype(v_ref.dtype)
    kr = u_ref[:, U_KR:U_MLA]
    kr = jnp.where(lax.broadcasted_iota(jnp.int32, kr.shape, 1) < MLA_ROPE, kr, 0.0)
    ss_r = jnp.sum(kr * kr, axis=-1, keepdims=True)
    cos, s1, s2 = cos_ref[...], s1_ref[...], s2_ref[...]
    gn, gr = gn_ref[...], gr_ref[...]
    for h in range(MLA_HEADS):
        kn = kv[:, h * MLA_NOPE:(h + 1) * MLA_NOPE]
        r = lax.rsqrt((jnp.sum(kn * kn, axis=-1, keepdims=True) + ss_r) / float(MLA_QK) + EPS)
        lo = h * MLA_HEAD_PAD
        k_ref[:, lo:lo + MLA_NOPE] = (kn * r * gn).astype(k_ref.dtype)
        k_ref[:, lo + MLA_NOPE:lo + MLA_HEAD_PAD] = _rope_chunk(kr * r * gr, cos, s1, s2).astype(k_ref.dtype)


def _rope_tables(seq):
    half = MLA_ROPE // 2
    inv = ROPE_THETA ** (-jnp.arange(half, dtype=F32) / half)
    ang = jnp.arange(seq).astype(F32)[:, None] * inv[None, :]
    cos, sin = jnp.cos(ang), jnp.sin(ang)
    z = jnp.zeros_like(cos)
    pad = jnp.zeros((seq, ROPE_PAD - MLA_ROPE), F32)
    return (jnp.concatenate([cos, cos, pad], 1), jnp.concatenate([-sin, z, pad], 1),
            jnp.concatenate([z, sin, pad], 1))


def _mla_prep(u, seq, q_a_norm, kv_a_norm, w_uq, w_ukv, q_norm, k_norm):
    t = u.shape[0]
    tm = _tile(seq, 256)
    spb = seq // tm
    cos, s1, s2 = _rope_tables(seq)
    tab_spec = pl.BlockSpec((tm, ROPE_PAD), lambda i: (i % spb, 0))
    zpad = jnp.zeros((MLA_HEAD_PAD - MLA_QK,), F32)

    wq = jnp.pad(w_uq.reshape(MLA_Q_RANK, MLA_HEADS, MLA_QK), ((0, 0), (0, 0), (0, MLA_HEAD_PAD - MLA_QK)))
    wq = wq.reshape(MLA_Q_RANK, MLA_HEADS * MLA_HEAD_PAD).astype(BF16)
    qw = MLA_HEADS * MLA_HEAD_PAD
    ublk = U_MLA
    blk = _nbytes((tm, ublk), F32) + _nbytes(wq.shape, BF16) + _nbytes((tm, qw), BF16) + 3 * _nbytes((tm, ROPE_PAD), F32)
    q = pl.pallas_call(
        functools.partial(_mla_q_kernel, scale=MLA_QK ** -0.5),
        grid=(t // tm,),
        in_specs=[
            pl.BlockSpec((tm, ublk), lambda i: (i, 0)),
            pl.BlockSpec((1, MLA_Q_RANK), lambda i: (0, 0)),
            pl.BlockSpec(wq.shape, lambda i: (0, 0)),
            pl.BlockSpec((1, MLA_HEAD_PAD), lambda i: (0, 0)),
            tab_spec, tab_spec, tab_spec,
        ],
        out_specs=pl.BlockSpec((tm, qw), lambda i: (i, 0)),
        out_shape=jax.ShapeDtypeStruct((t, qw), BF16),
        compiler_params=_cparams(("parallel",), blk, 3 * _nbytes((tm, qw), F32)),
        name="mla_q_prep",
    )(u, q_a_norm.reshape(1, -1), wq, jnp.concatenate([q_norm, zpad]).reshape(1, -1), cos, s1, s2)

    wkv = w_ukv.reshape(MLA_KV_RANK, MLA_HEADS, MLA_NOPE + MLA_V)
    wkv = jnp.concatenate([wkv[:, :, :MLA_NOPE].reshape(MLA_KV_RANK, -1), wkv[:, :, MLA_NOPE:].reshape(MLA_KV_RANK, -1)], 1)
    wkv = wkv.astype(BF16)
    vw = MLA_HEADS * MLA_V
    blk = (_nbytes((tm, ublk), F32) + _nbytes(wkv.shape, BF16) + _nbytes((tm, qw), BF16) + _nbytes((tm, vw), BF16)
           + 3 * _nbytes((tm, ROPE_PAD), F32))
    k, v = pl.pallas_call(
        _mla_kv_kernel,
        grid=(t // tm,),
        in_specs=[
            pl.BlockSpec((tm, ublk), lambda i: (i, 0)),
            pl.BlockSpec((1, MLA_KV_RANK), lambda i: (0, 0)),
            pl.BlockSpec(wkv.shape, lambda i: (0, 0)),
            pl.BlockSpec((1, MLA_NOPE), lambda i: (0, 0)),
            pl.BlockSpec((1, ROPE_PAD), lambda i: (0, 0)),
            tab_spec, tab_spec, tab_spec,
        ],
        out_specs=[pl.BlockSpec((tm, qw), lambda i: (i, 0)), pl.BlockSpec((tm, vw), lambda i: (i, 0))],
        out_shape=[jax.ShapeDtypeStruct((t, qw), BF16), jax.ShapeDtypeStruct((t, vw), BF16)],
        compiler_params=_cparams(("parallel",), blk, 3 * _nbytes((tm, qw), F32)),
        name="mla_kv_prep",
    )(u, kv_a_norm.reshape(1, -1), wkv, k_norm[:MLA_NOPE].reshape(1, -1),
      jnp.concatenate([k_norm[MLA_NOPE:], jnp.zeros((ROPE_PAD - MLA_ROPE,), F32)]).reshape(1, -1), cos, s1, s2)
    return q, k, v


def _mla_attn_kernel(q_ref, k_ref, v_ref, o_ref, *, tq):
    qi = pl.program_id(2)
    q = q_ref[...]

    def step(kblk, vblk, carry, diagonal):
        m, l, acc = carry
        s = lax.dot_general(q, kblk, (((1,), (1,)), ((), ())), preferred_element_type=F32)
        if diagonal:
            row = lax.broadcasted_iota(jnp.int32, s.shape, 0)
            col = lax.broadcasted_iota(jnp.int32, s.shape, 1)
            s = jnp.where(col <= row, s, NEG)
        m_new = jnp.maximum(m, jnp.max(s, axis=-1, keepdims=True))
        alpha = jnp.exp(m - m_new)
        p = jnp.exp(s - m_new)
        l = alpha * l + jnp.sum(p, axis=-1, keepdims=True)
        acc = alpha * acc + jnp.dot(p.astype(BF16), vblk, preferred_element_type=F32)
        return m_new, l, acc

    def body(ki, carry):
        start = pl.multiple_of(ki * tq, tq)
        return step(k_ref[pl.ds(start, tq), :], v_ref[pl.ds(start, tq), :], carry, False)

    init = (jnp.full((tq, 1), NEG, F32), jnp.zeros((tq, 1), F32), jnp.zeros((tq, MLA_V), F32))
    carry = lax.fori_loop(0, qi, body, init)
    start = pl.multiple_of(qi * tq, tq)
    _, l, acc = step(k_ref[pl.ds(start, tq), :], v_ref[pl.ds(start, tq), :], carry, True)
    o_ref[...] = acc / l


def _mla_attention(q, k, v, batch, seq):
    t = q.shape[0]
    tq = _tile(seq, 512)
    nq = seq // tq
    blk = _nbytes((tq, MLA_HEAD_PAD), BF16) + _nbytes((seq, MLA_HEAD_PAD), BF16) + _nbytes((seq, MLA_V), BF16) + _nbytes((tq, MLA_V), F32)
    return pl.pallas_call(
        functools.partial(_mla_attn_kernel, tq=tq),
        grid=(batch, MLA_HEADS, nq),
        in_specs=[
            pl.BlockSpec((tq, MLA_HEAD_PAD), lambda b, h, i: (b * nq + i, h)),
            pl.BlockSpec((seq, MLA_HEAD_PAD), lambda b, h, i: (b, h)),
            pl.BlockSpec((seq, MLA_V), lambda b, h, i: (b, h)),
        ],
        out_specs=pl.BlockSpec((tq, MLA_V), lambda b, h, i: (b * nq + i, h)),
        out_shape=jax.ShapeDtypeStruct((t, MLA_WIDTH), F32),
        compiler_params=_cparams(("parallel", "parallel", "arbitrary"), blk, 6 * _nbytes((tq, tq), F32)),
        name="mla_attention",
    )(q, k, v)


def _t5_bucket(dist):
    max_exact = REL_BUCKETS // 2
    d = jnp.maximum(dist, 1).astype(F32)
    large = max_exact + (jnp.log(d / max_exact) / jnp.log(REL_MAX_DIST / max_exact)
                         * (REL_BUCKETS - max_exact)).astype(jnp.int32)
    large = jnp.minimum(large, REL_BUCKETS - 1)
    return jnp.where(dist < max_exact, dist, large)


def _swa_kernel(sink_ref, q0_ref, q1_ref, q2_ref, q3_ref, kp_ref, kc_ref, vp_ref, vc_ref, gq_ref, gk_ref,
                bias_ref, o_ref):
    n = pl.program_id(1)
    shape = (BLOCK_Q, 2 * BLOCK_Q)
    row = lax.broadcasted_iota(jnp.int32, shape, 0)
    col = lax.broadcasted_iota(jnp.int32, shape, 1)
    dist = row + BLOCK_Q - col
    valid = (dist >= 0) & (dist < WINDOW) & ((col >= BLOCK_Q) | (n > 0))
    k_all = jnp.concatenate([kp_ref[...], kc_ref[...]], axis=0)
    v_all = jnp.concatenate([vp_ref[...], vc_ref[...]], axis=0).astype(BF16)
    gq = gq_ref[...] * (SWA_HEAD_DIM ** -0.5)
    gk = gk_ref[...]
    q_refs = (q0_ref, q1_ref, q2_ref, q3_ref)
    outs = []
    for kh in range(SWA_KV_HEADS):
        lo = kh * SWA_HEAD_DIM
        kn = _rms_rows(k_all[:, lo:lo + SWA_HEAD_DIM], gk).astype(BF16)
        v = v_all[:, lo:lo + SWA_HEAD_DIM]
        qblk = q_refs[kh][...]
        for g in range(SWA_GROUP):
            h = kh * SWA_GROUP + g
            qn = _rms_rows(qblk[:, g * SWA_HEAD_DIM:(g + 1) * SWA_HEAD_DIM], gq).astype(BF16)
            s = lax.dot_general(qn, kn, (((1,), (1,)), ((), ())), preferred_element_type=F32)
            s = jnp.where(valid, s + bias_ref[h], NEG)
            sink = sink_ref[h]
            m = jnp.maximum(jnp.max(s, axis=-1, keepdims=True), sink)
            e = jnp.exp(s - m)
            den = jnp.sum(e, axis=-1, keepdims=True) + jnp.exp(sink - m)
            outs.append(jnp.dot((e / den).astype(BF16), v, preferred_element_type=F32))
    o_ref[...] = jnp.concatenate(outs, axis=-1)


def _swa_attention(u, batch, seq, q_norm, k_norm, sinks, rel_bias):
    t = u.shape[0]
    nb = seq // BLOCK_Q
    by_dist = rel_bias.astype(F32)[_t5_bucket(jnp.arange(WINDOW))].T
    row = jnp.concatenate([jnp.zeros((SWA_HEADS, 1), F32), by_dist[:, ::-1], jnp.zeros((SWA_HEADS, BLOCK_Q), F32)], 1)
    bias = jnp.broadcast_to(row[:, None, :], (SWA_HEADS, BLOCK_Q, 2 * BLOCK_Q + 1)).reshape(SWA_HEADS, -1)
    bias = bias[:, :BLOCK_Q * 2 * BLOCK_Q].reshape(SWA_HEADS, BLOCK_Q, 2 * BLOCK_Q)
    gw = SWA_GROUP * SWA_HEAD_DIM
    q_specs = [pl.BlockSpec((BLOCK_Q, gw), functools.partial(lambda b, n, c: (b * nb + n, c), c=kh))
               for kh in range(SWA_KV_HEADS)]
    cur = lambda c: pl.BlockSpec((BLOCK_Q, SWA_KV_WIDTH), lambda b, n: (b * nb + n, c))
    prev = lambda c: pl.BlockSpec((BLOCK_Q, SWA_KV_WIDTH), lambda b, n: (b * nb + jnp.maximum(n - 1, 0), c))
    ck, cv = S_K // SWA_KV_WIDTH, S_V // SWA_KV_WIDTH
    blk = (4 * _nbytes((BLOCK_Q, gw), F32) + 4 * _nbytes((BLOCK_Q, SWA_KV_WIDTH), F32) + _nbytes(bias.shape, F32)
           + _nbytes((BLOCK_Q, SWA_WIDTH), F32))
    return pl.pallas_call(
        _swa_kernel,
        grid=(batch, nb),
        in_specs=[pl.BlockSpec(memory_space=pltpu.SMEM)] + q_specs + [
            prev(ck), cur(ck), prev(cv), cur(cv),
            pl.BlockSpec((1, SWA_HEAD_DIM), lambda b, n: (0, 0)),
            pl.BlockSpec((1, SWA_HEAD_DIM), lambda b, n: (0, 0)),
            pl.BlockSpec(bias.shape, lambda b, n: (0, 0, 0)),
        ],
        out_specs=pl.BlockSpec((BLOCK_Q, SWA_WIDTH), lambda b, n: (b * nb + n, 0)),
        out_shape=jax.ShapeDtypeStruct((t, SWA_WIDTH), F32),
        compiler_params=_cparams(("parallel", "arbitrary"), blk, 4 << 20),
        name="swa_attention",
    )(sinks, u, u, u, u, u, u, u, u, q_norm.reshape(1, -1), k_norm.reshape(1, -1), bias)


def _mem_attn_kernel(q_ref, kv_ref, gq_ref, gk_ref, o_ref):
    gq = gq_ref[...] * (MEM_HEAD_DIM ** -0.5)
    gk = gk_ref[...]
    for h in range(MEM_HEADS):
        q = _rms_rows(q_ref[:, h * MEM_HEAD_DIM:(h + 1) * MEM_HEAD_DIM], gq).astype(BF16)
        lo = 2 * h * MEM_HEAD_DIM
        k = _rms_rows(kv_ref[:, lo:lo + MEM_HEAD_DIM], gk).astype(BF16)
        v = kv_ref[:, lo + MEM_HEAD_DIM:lo + 2 * MEM_HEAD_DIM].astype(BF16)
        s = lax.dot_general(q, k, (((1,), (1,)), ((), ())), preferred_element_type=F32)
        e = jnp.exp(s - jnp.max(s, axis=-1, keepdims=True))
        p = (e / jnp.sum(e, axis=-1, keepdims=True)).astype(BF16)
        o_ref[:, h * MEM_HEAD_DIM:(h + 1) * MEM_HEAD_DIM] = jnp.dot(p, v, preferred_element_type=F32).astype(o_ref.dtype)


def _mem_attention(qm, kvm, batch, seq, mem_len, q_norm, k_norm):
    t, w = qm.shape
    tq = _tile(seq, 512)
    nq = seq // tq
    blk = _nbytes((tq, w), F32) + _nbytes((mem_len, 2 * w), F32) + _nbytes((tq, w), BF16)
    return pl.pallas_call(
        _mem_attn_kernel,
        grid=(batch, nq),
        in_specs=[
            pl.BlockSpec((tq, w), lambda b, i: (b * nq + i, 0)),
            pl.BlockSpec((mem_len, 2 * w), lambda b, i: (b, 0)),
            pl.BlockSpec((1, MEM_HEAD_DIM), lambda b, i: (0, 0)),
            pl.BlockSpec((1, MEM_HEAD_DIM), lambda b, i: (0, 0)),
        ],
        out_specs=pl.BlockSpec((tq, w), lambda b, i: (b * nq + i, 0)),
        out_shape=jax.ShapeDtypeStruct((t, w), BF16),
        compiler_params=_cparams(("parallel", "arbitrary"), blk, 8 * _nbytes((tq, mem_len), F32)),
        name="mem_attention",
    )(qm, kvm, q_norm.reshape(1, -1), k_norm.reshape(1, -1))


def _layer(h, mem, rel_bias, p, batch, seq):
    h = _swiglu_half(h, p['ffn_a_norm'], p['ffn_a_gate'], p['ffn_a_up'], p['ffn_a_down'])

    xn = _norm(h, p['mix_norm'])
    u_mla = _mm_w32(xn, p['w_in'], tm=1024, tn=512, n=U_MLA, name="in_proj_mla")
    u_swa = _mm_w32(xn, p['w_in'][:, IN_SWA:], tm=1024, tn=512, name="in_proj_swa")

    q, k, v = _mla_prep(u_mla, seq, p['mla_q_a_norm'], p['mla_kv_a_norm'], p['mla_w_uq'], p['mla_w_ukv'],
                        p['mla_q_norm'], p['mla_k_norm'])
    o_a = _mla_attention(q, k, v, batch, seq)
    o_b = _swa_attention(u_swa, batch, seq, p['swa_q_norm'], p['swa_k_norm'], p['swa_sinks'], rel_bias)
    o = _dual_norm(o_a, o_b, p['out_norm_mla'], p['out_norm_swa'])
    h = _mm_w32(o, p['w_out'], h, tm=1024, tn=512, name="out_proj")

    mem_len = mem.shape[0] // batch
    qm = _mm_w32(_norm(h, p['mem_attn_norm']), p['mem_w_q'], tm=1024, tn=512, name="mem_q_proj")
    kvm = _mm_w32(_norm(mem, p['mem_norm']), p['mem_w_kv'], tm=512, tn=512, name="mem_kv_proj")
    om = _mem_attention(qm, kvm, batch, seq, mem_len, p['mem_q_norm'], p['mem_k_norm'])
    h = _mm_w32(om, p['mem_w_o'], h, tm=1024, tn=512, name="mem_o_proj")

    return _swiglu_half(h, p['ffn_b_norm'], p['ffn_b_gate'], p['ffn_b_up'], p['ffn_b_down'])


_PARAM_NAMES = (
    'ffn_a_norm', 'ffn_a_gate', 'ffn_a_up', 'ffn_a_down', 'mix_norm', 'w_in',
    'mla_q_a_norm', 'mla_kv_a_norm', 'mla_w_uq', 'mla_w_ukv', 'mla_q_norm', 'mla_k_norm',
    'swa_q_norm', 'swa_k_norm', 'swa_sinks', 'out_norm_mla', 'out_norm_swa', 'w_out',
    'mem_attn_norm', 'mem_norm', 'mem_w_q', 'mem_w_kv', 'mem_q_norm', 'mem_k_norm', 'mem_w_o',
    'ffn_b_norm', 'ffn_b_gate', 'ffn_b_up', 'ffn_b_down',
)


def kernel(x, mem, rel_bias, ffn_a_norm, ffn_a_gate, ffn_a_up, ffn_a_down, mix_norm, w_in, mla_q_a_norm, mla_kv_a_norm, mla_w_uq, mla_w_ukv, mla_q_norm, mla_k_norm, swa_q_norm, swa_k_norm, swa_sinks, out_norm_mla, out_norm_swa, w_out, mem_attn_norm, mem_norm, mem_w_q, mem_w_kv, mem_q_norm, mem_k_norm, mem_w_o, ffn_b_norm, ffn_b_gate, ffn_b_up, ffn_b_down):
    stacked = (ffn_a_norm, ffn_a_gate, ffn_a_up, ffn_a_down, mix_norm, w_in, mla_q_a_norm, mla_kv_a_norm,
               mla_w_uq, mla_w_ukv, mla_q_norm, mla_k_norm, swa_q_norm, swa_k_norm, swa_sinks, out_norm_mla,
               out_norm_swa, w_out, mem_attn_norm, mem_norm, mem_w_q, mem_w_kv, mem_q_norm, mem_k_norm, mem_w_o,
               ffn_b_norm, ffn_b_gate, ffn_b_up, ffn_b_down)
    batch, seq, d = x.shape
    h = x.reshape(batch * seq, d)
    mem2 = mem.reshape(-1, d)
    for layer in range(ffn_a_norm.shape[0]):
        p = {name: arr[layer] for name, arr in zip(_PARAM_NAMES, stacked)}
        h = _layer(h, mem2, rel_bias, p, batch, seq)
    return h.reshape(batch, seq, d)
```

```python
import functools
import math

import jax
import jax.numpy as jnp
from jax import lax
from jax.experimental import pallas as pl
from jax.experimental.pallas import tpu as pltpu

F32 = jnp.float32
BF16 = jnp.bfloat16

MEM_HEADS = 4
MEM_HEAD_DIM = 128
MLA_HEADS = 16
MLA_Q_RANK = 896
MLA_KV_RANK = 512
MLA_NOPE = 128
MLA_ROPE = 64
MLA_V = 128
ROPE_THETA = 10000.0
SWA_HEADS = 32
SWA_KV_HEADS = 4
SWA_HEAD_DIM = 64
WINDOW = 128
REL_BUCKETS = 32
REL_MAX_DIST = 128
BLOCK_Q = 128
EPS = 1e-6
NEG = -1e30

MLA_QK = MLA_NOPE + MLA_ROPE
SWA_GROUP = SWA_HEADS // SWA_KV_HEADS
SWA_WIDTH = SWA_HEADS * SWA_HEAD_DIM
SWA_KV_WIDTH = SWA_KV_HEADS * SWA_HEAD_DIM
MLA_WIDTH = MLA_HEADS * MLA_V

LANES = 128
MLA_HEAD_PAD = 2 * LANES
ROPE_PAD = LANES

U_CKV = MLA_Q_RANK
U_KR = U_CKV + MLA_KV_RANK
U_MLA = U_KR + ROPE_PAD
IN_SWA = U_KR + MLA_ROPE
S_K = SWA_WIDTH
S_V = S_K + SWA_KV_WIDTH

KEY_BLOCK = 256

V7X_SCOPED_VMEM_BYTES = 60000 * 1024


def _tile(n, pref, align=8):
    if n <= pref:
        return n
    t = (pref // align) * align
    while t >= align:
        if n % t == 0:
            return t
        t -= align
    return n


def _nbytes(shape, dtype):
    n = 1
    for s in shape:
        n *= s
    return n * jnp.dtype(dtype).itemsize


def _cparams(semantics, pipelined_bytes, temp_bytes=0):
    need = 2 * pipelined_bytes + temp_bytes + (4 << 20)
    return pltpu.CompilerParams(
        dimension_semantics=semantics,
        vmem_limit_bytes=int(min(max(need, 16 << 20), V7X_SCOPED_VMEM_BYTES)),
    )


def _rms_rows(x, g, n=None):
    n = x.shape[-1] if n is None else n
    r = lax.rsqrt(jnp.sum(x * x, axis=-1, keepdims=True) / float(n) + EPS)
    return x * r * g


def _norm_kernel(x_ref, g_ref, o_ref):
    o_ref[...] = _rms_rows(x_ref[...], g_ref[...]).astype(o_ref.dtype)


def _norm(x, g, out_dtype=BF16):
    rows, d = x.shape
    tr = _tile(rows, 512)
    blk = _nbytes((tr, d), F32) + _nbytes((tr, d), out_dtype)
    return pl.pallas_call(
        _norm_kernel,
        grid=(rows // tr,),
        in_specs=[pl.BlockSpec((tr, d), lambda i: (i, 0)), pl.BlockSpec((1, d), lambda i: (0, 0))],
        out_specs=pl.BlockSpec((tr, d), lambda i: (i, 0)),
        out_shape=jax.ShapeDtypeStruct((rows, d), out_dtype),
        compiler_params=_cparams(("parallel",), blk, _nbytes((tr, d), F32)),
        name="rmsnorm",
    )(x, g.reshape(1, d))


def _dual_norm_kernel(a_ref, b_ref, ga_ref, gb_ref, o_ref):
    wa = a_ref.shape[-1]
    o_ref[:, :wa] = _rms_rows(a_ref[...], ga_ref[...]).astype(o_ref.dtype)
    o_ref[:, wa:] = _rms_rows(b_ref[...], gb_ref[...]).astype(o_ref.dtype)


def _dual_norm(a, b, ga, gb):
    rows, wa = a.shape
    wb = b.shape[1]
    tr = _tile(rows, 512)
    blk = _nbytes((tr, wa + wb), F32) + _nbytes((tr, wa + wb), BF16)
    return pl.pallas_call(
        _dual_norm_kernel,
        grid=(rows // tr,),
        in_specs=[
            pl.BlockSpec((tr, wa), lambda i: (i, 0)),
            pl.BlockSpec((tr, wb), lambda i: (i, 0)),
            pl.BlockSpec((1, wa), lambda i: (0, 0)),
            pl.BlockSpec((1, wb), lambda i: (0, 0)),
        ],
        out_specs=pl.BlockSpec((tr, wa + wb), lambda i: (i, 0)),
        out_shape=jax.ShapeDtypeStruct((rows, wa + wb), BF16),
        compiler_params=_cparams(("parallel",), blk, _nbytes((tr, wa + wb), F32)),
        name="out_norms",
    )(a, b, ga.reshape(1, wa), gb.reshape(1, wb))


def _mm_kernel(a_ref, w_ref, o_ref):
    o_ref[...] = jnp.dot(a_ref[...], w_ref[...], preferred_element_type=F32).astype(o_ref.dtype)


def _mm_res_kernel(a_ref, w_ref, r_ref, o_ref):
    o_ref[...] = r_ref[...] + jnp.dot(a_ref[...], w_ref[...], preferred_element_type=F32)


def _mm(a, w, res=None, *, tm, tn, out_dtype=F32, name="matmul"):
    m, k = a.shape
    n = w.shape[1]
    tm, tn = _tile(m, tm), _tile(n, tn, LANES)
    blk = _nbytes((tm, k), a.dtype) + _nbytes((k, tn), w.dtype) + _nbytes((tm, tn), out_dtype)
    in_specs = [pl.BlockSpec((tm, k), lambda i, j: (i, 0)), pl.BlockSpec((k, tn), lambda i, j: (0, j))]
    args = [a, w]
    kern = _mm_kernel
    if res is not None:
        in_specs.append(pl.BlockSpec((tm, tn), lambda i, j: (i, j)))
        args.append(res)
        blk += _nbytes((tm, tn), F32)
        kern = _mm_res_kernel
    return pl.pallas_call(
        kern,
        grid=(m // tm, n // tn),
        in_specs=in_specs,
        out_specs=pl.BlockSpec((tm, tn), lambda i, j: (i, j)),
        out_shape=jax.ShapeDtypeStruct((m, n), out_dtype),
        compiler_params=_cparams(("parallel", "arbitrary"), blk, 2 * _nbytes((tm, tn), F32)),
        name=name,
    )(*args)


def _mm_w32_kernel(a_ref, w_ref, *rest, has_res):
    if has_res:
        r_ref, o_ref, wb_ref = rest
    else:
        o_ref, wb_ref = rest

    @pl.when(pl.program_id(1) == 0)
    def _():
        wb_ref[...] = w_ref[...].astype(BF16)

    acc = jnp.dot(a_ref[...], wb_ref[...], preferred_element_type=F32)
    if has_res:
        acc = r_ref[...] + acc
    o_ref[...] = acc.astype(o_ref.dtype)


def _mm_w32(a, w, res=None, *, tm, tn, n=None, out_dtype=F32, name="matmul"):
    m, k = a.shape
    n = w.shape[1] if n is None else n
    tm, tn = _tile(m, tm), _tile(n, tn, LANES)
    blk = _nbytes((tm, k), a.dtype) + _nbytes((k, tn), F32) + _nbytes((tm, tn), out_dtype)
    in_specs = [pl.BlockSpec((tm, k), lambda j, i: (i, 0)), pl.BlockSpec((k, tn), lambda j, i: (0, j))]
    args = [a, w]
    if res is not None:
        in_specs.append(pl.BlockSpec((tm, tn), lambda j, i: (i, j)))
        args.append(res)
        blk += _nbytes((tm, tn), F32)
    return pl.pallas_call(
        functools.partial(_mm_w32_kernel, has_res=res is not None),
        grid=(n // tn, m // tm),
        in_specs=in_specs,
        out_specs=pl.BlockSpec((tm, tn), lambda j, i: (i, j)),
        out_shape=jax.ShapeDtypeStruct((m, n), out_dtype),
        scratch_shapes=[pltpu.VMEM((k, tn), BF16)],
        compiler_params=_cparams(("parallel", "arbitrary"), blk, _nbytes((k, tn), BF16) + 2 * _nbytes((tm, tn), F32)),
        name=name,
    )(*args)


def _mm_wt32_kernel(a_ref, wt_ref, o_ref, wb_ref):
    @pl.when(pl.program_id(1) == 0)
    def _():
        wb_ref[...] = wt_ref[...].astype(BF16)

    o_ref[...] = lax.dot_general(a_ref[...], wb_ref[...], (((1,), (1,)), ((), ())),
                                 preferred_element_type=F32).astype(o_ref.dtype)


def _mm_wt32(a, wt, *, row0, n, tm, tn, name):
    m, k = a.shape
    tm, tn = _tile(m, tm), _tile(n, tn, LANES)
    blk = _nbytes((tm, k), a.dtype) + _nbytes((tn, k), F32) + _nbytes((tm, tn), F32)
    return pl.pallas_call(
        _mm_wt32_kernel,
        grid=(n // tn, m // tm),
        in_specs=[
            pl.BlockSpec((tm, k), lambda j, i: (i, 0)),
            pl.BlockSpec((pl.Element(tn), pl.Element(k)), lambda j, i: (pl.multiple_of(row0 + j * tn, 8), 0)),
        ],
        out_specs=pl.BlockSpec((tm, tn), lambda j, i: (i, j)),
        out_shape=jax.ShapeDtypeStruct((m, n), F32),
        scratch_shapes=[pltpu.VMEM((tn, k), BF16)],
        compiler_params=_cparams(("parallel", "arbitrary"), blk, _nbytes((tn, k), BF16) + 2 * _nbytes((tm, tn), F32)),
        name=name,
    )(a, wt)


def _gated_kernel(x_ref, wg_ref, wu_ref, wd_ref, o_ref, wdb_ref):
    @pl.when(pl.program_id(0) == 0)
    def _():
        wdb_ref[...] = wd_ref[...].astype(BF16)

    x = x_ref[...]
    g = jnp.dot(x, wg_ref[...].astype(BF16), preferred_element_type=F32)
    u = jnp.dot(x, wu_ref[...].astype(BF16), preferred_element_type=F32)
    o_ref[...] = (0.5 * (g / (1.0 + jnp.exp(-g))) * u).astype(o_ref.dtype)


def _gated(x, wg, wu, wd, *, tm, tn):
    m, k = x.shape
    n = wg.shape[1]
    d_out = wd.shape[1]
    tm, tn = _tile(m, tm), _tile(n, tn, LANES)
    nj = n // tn
    blk = (_nbytes((tm, k), BF16) + 2 * _nbytes((k, tn), F32) + _nbytes((tm, tn), BF16)
           + _nbytes((tn, d_out), F32) + _nbytes((tn, d_out), BF16))
    wd_map = lambda i, j: (jnp.where(i == 0, j, nj - 1), 0)
    return pl.pallas_call(
        _gated_kernel,
        grid=(m // tm, nj),
        in_specs=[
            pl.BlockSpec((tm, k), lambda i, j: (i, 0)),
            pl.BlockSpec((k, tn), lambda i, j: (0, j)),
            pl.BlockSpec((k, tn), lambda i, j: (0, j)),
            pl.BlockSpec((tn, d_out), wd_map),
        ],
        out_specs=[pl.BlockSpec((tm, tn), lambda i, j: (i, j)), pl.BlockSpec((tn, d_out), wd_map)],
        out_shape=[jax.ShapeDtypeStruct((m, n), BF16), jax.ShapeDtypeStruct((n, d_out), BF16)],
        compiler_params=_cparams(("arbitrary", "arbitrary"), blk, 2 * _nbytes((k, tn), BF16) + 4 * _nbytes((tm, tn), F32)),
        name="ffn_gate_up",
    )(x, wg, wu, wd)


def _swiglu_half(h, norm_g, wg, wu, wd):
    xn = _norm(h, norm_g)
    act, wd_bf = _gated(xn, wg, wu, wd, tm=1024, tn=256)
    return _mm(act, wd_bf, h, tm=512, tn=512, name="ffn_down")


def _rope_chunk(y, cos, s1, s2):
    return y * cos + pltpu.roll(y, ROPE_PAD - MLA_ROPE // 2, 1) * s1 + pltpu.roll(y, MLA_ROPE // 2, 1) * s2


def _mla_q_kernel(u_ref, ga_ref, wt_ref, gt_ref, cos_ref, sin_ref, o_ref):
    cn = _rms_rows(u_ref[:, :MLA_Q_RANK], ga_ref[...]).astype(BF16)
    qt = lax.dot_general(wt_ref[...], cn, (((1,), (1,)), ((), ())), preferred_element_type=F32)
    cos, sin, gain = cos_ref[...], sin_ref[...], gt_ref[...]
    half = MLA_ROPE // 2
    zeros = jnp.zeros((MLA_HEAD_PAD - MLA_QK, qt.shape[1]), o_ref.dtype)
    for h in range(MLA_HEADS):
        lo = h * MLA_HEAD_PAD
        c = qt[lo:lo + MLA_HEAD_PAD, :]
        y = c * lax.rsqrt(jnp.sum(c * c, axis=0, keepdims=True) / float(MLA_QK) + EPS) * gain
        x1, x2 = y[MLA_NOPE:MLA_NOPE + half, :], y[MLA_NOPE + half:MLA_QK, :]
        o_ref[lo:lo + MLA_NOPE, :] = y[:MLA_NOPE, :].astype(o_ref.dtype)
        o_ref[lo + MLA_NOPE:lo + MLA_NOPE + half, :] = (x1 * cos - x2 * sin).astype(o_ref.dtype)
        o_ref[lo + MLA_NOPE + half:lo + MLA_QK, :] = (x2 * cos + x1 * sin).astype(o_ref.dtype)
        o_ref[lo + MLA_QK:lo + MLA_HEAD_PAD, :] = zeros


def _mla_kv_kernel(u_ref, ga_ref, wk_ref, wvt_ref, gn_ref, gr_ref, cos_ref, s1_ref, s2_ref, k_ref, vt_ref):
    cn = _rms_rows(u_ref[:, U_CKV:U_KR], ga_ref[...]).astype(BF16)
    kv = jnp.dot(cn, wk_ref[...], preferred_element_type=F32)
    vt = lax.dot_general(wvt_ref[...], cn, (((1,), (1,)), ((), ())), preferred_element_type=F32)
    vt_ref[...] = vt.astype(vt_ref.dtype)
    kr = u_ref[:, U_KR:U_MLA]
    kr = jnp.where(lax.broadcasted_iota(jnp.int32, kr.shape, 1) < MLA_ROPE, kr, 0.0)
    ss_r = jnp.sum(kr * kr, axis=-1, keepdims=True)
    cos, s1, s2 = cos_ref[...], s1_ref[...], s2_ref[...]
    gn, gr = gn_ref[...], gr_ref[...]
    for h in range(MLA_HEADS):
        kn = kv[:, h * MLA_NOPE:(h + 1) * MLA_NOPE]
        r = lax.rsqrt((jnp.sum(kn * kn, axis=-1, keepdims=True) + ss_r) / float(MLA_QK) + EPS)
        lo = h * MLA_HEAD_PAD
        k_ref[:, lo:lo + MLA_NOPE] = (kn * r * gn).astype(k_ref.dtype)
        k_ref[:, lo + MLA_NOPE:lo + MLA_HEAD_PAD] = _rope_chunk(kr * r * gr, cos, s1, s2).astype(k_ref.dtype)


def _rope_tables(seq):
    half = MLA_ROPE // 2
    inv = ROPE_THETA ** (-jnp.arange(half, dtype=F32) / half)
    ang = jnp.arange(seq).astype(F32)[:, None] * inv[None, :]
    cos, sin = jnp.cos(ang), jnp.sin(ang)
    z = jnp.zeros_like(cos)
    pad = jnp.zeros((seq, ROPE_PAD - MLA_ROPE), F32)
    return (jnp.concatenate([cos, cos, pad], 1), jnp.concatenate([-sin, z, pad], 1),
            jnp.concatenate([z, sin, pad], 1), cos.T, sin.T)


def _mla_prep(u, seq, q_a_norm, kv_a_norm, w_uq, w_ukv, q_norm, k_norm):
    t = u.shape[0]
    tm = _tile(seq, KEY_BLOCK)
    spb = seq // tm
    cos, s1, s2, cos_t, sin_t = _rope_tables(seq)
    tab_spec = pl.BlockSpec((tm, ROPE_PAD), lambda i: (i % spb, 0))
    tab_t_spec = pl.BlockSpec((MLA_ROPE // 2, tm), lambda i: (0, i % spb))
    qw = MLA_HEADS * MLA_HEAD_PAD
    vw = MLA_HEADS * MLA_V

    wq_t = jnp.pad(w_uq.reshape(MLA_Q_RANK, MLA_HEADS, MLA_QK), ((0, 0), (0, 0), (0, MLA_HEAD_PAD - MLA_QK)))
    wq_t = wq_t.reshape(MLA_Q_RANK, qw).T.astype(BF16)
    gain_t = jnp.concatenate([q_norm * MLA_QK ** -0.5, jnp.zeros((MLA_HEAD_PAD - MLA_QK,), F32)])
    gain_t = jnp.broadcast_to(gain_t[:, None], (MLA_HEAD_PAD, tm))
    blk = (_nbytes((tm, U_MLA), F32) + _nbytes(wq_t.shape, BF16) + _nbytes((qw, tm), BF16)
           + _nbytes((MLA_HEAD_PAD, tm), F32) + 2 * _nbytes((MLA_ROPE // 2, tm), F32))
    q_t = pl.pallas_call(
        _mla_q_kernel,
        grid=(t // tm,),
        in_specs=[
            pl.BlockSpec((tm, U_MLA), lambda i: (i, 0)),
            pl.BlockSpec((1, MLA_Q_RANK), lambda i: (0, 0)),
            pl.BlockSpec(wq_t.shape, lambda i: (0, 0)),
            pl.BlockSpec((MLA_HEAD_PAD, tm), lambda i: (0, 0)),
            tab_t_spec, tab_t_spec,
        ],
        out_specs=pl.BlockSpec((qw, tm), lambda i: (0, i)),
        out_shape=jax.ShapeDtypeStruct((qw, t), BF16),
        compiler_params=_cparams(("parallel",), blk, 3 * _nbytes((qw, tm), F32)),
        name="mla_q_prep",
    )(u, q_a_norm.reshape(1, -1), wq_t, gain_t, cos_t, sin_t)

    wkv = w_ukv.reshape(MLA_KV_RANK, MLA_HEADS, MLA_NOPE + MLA_V)
    wk = wkv[:, :, :MLA_NOPE].reshape(MLA_KV_RANK, -1).astype(BF16)
    wv_t = wkv[:, :, MLA_NOPE:].reshape(MLA_KV_RANK, -1).T.astype(BF16)
    blk = (_nbytes((tm, U_MLA), F32) + _nbytes(wk.shape, BF16) + _nbytes(wv_t.shape, BF16) + _nbytes((tm, qw), BF16)
           + _nbytes((vw, tm), BF16) + 3 * _nbytes((tm, ROPE_PAD), F32))
    k, v_t = pl.pallas_call(
        _mla_kv_kernel,
        grid=(t // tm,),
        in_specs=[
            pl.BlockSpec((tm, U_MLA), lambda i: (i, 0)),
            pl.BlockSpec((1, MLA_KV_RANK), lambda i: (0, 0)),
            pl.BlockSpec(wk.shape, lambda i: (0, 0)),
            pl.BlockSpec(wv_t.shape, lambda i: (0, 0)),
            pl.BlockSpec((1, MLA_NOPE), lambda i: (0, 0)),
            pl.BlockSpec((1, ROPE_PAD), lambda i: (0, 0)),
            tab_spec, tab_spec, tab_spec,
        ],
        out_specs=[pl.BlockSpec((tm, qw), lambda i: (i, 0)), pl.BlockSpec((None, vw, tm), lambda i: (i, 0, 0))],
        out_shape=[jax.ShapeDtypeStruct((t, qw), BF16), jax.ShapeDtypeStruct((t // tm, vw, tm), BF16)],
        compiler_params=_cparams(("parallel",), blk, 3 * _nbytes((tm, qw), F32)),
        name="mla_kv_prep",
    )(u, kv_a_norm.reshape(1, -1), wk, wv_t, k_norm[:MLA_NOPE].reshape(1, -1),
      jnp.concatenate([k_norm[MLA_NOPE:], jnp.zeros((ROPE_PAD - MLA_ROPE,), F32)]).reshape(1, -1), cos, s1, s2)
    return q_t, k, v_t


def _mla_attn_kernel(qt_ref, k_ref, vt_ref, o_ref, *, tq, kb):
    qi = pl.program_id(2)
    per = tq // kb
    nh = qt_ref.shape[0] // MLA_HEAD_PAD
    qts = [qt_ref[hh * MLA_HEAD_PAD:(hh + 1) * MLA_HEAD_PAD, :] for hh in range(nh)]

    def step(sb, carries, masked):
        out = []
        for hh in range(nh):
            m, l, acc = carries[hh]
            ss = []
            for d in range(per):
                start = pl.multiple_of((sb * per + d) * kb, kb)
                kblk = k_ref[pl.ds(start, kb), hh * MLA_HEAD_PAD:(hh + 1) * MLA_HEAD_PAD]
                s = jnp.dot(kblk, qts[hh], preferred_element_type=F32)
                if masked:
                    key = start + lax.broadcasted_iota(jnp.int32, s.shape, 0)
                    qry = qi * tq + lax.broadcasted_iota(jnp.int32, s.shape, 1)
                    s = jnp.where(key <= qry, s, NEG)
                ss.append(s)
            m_new = m
            for s in ss:
                m_new = jnp.maximum(m_new, jnp.max(s, axis=0, keepdims=True))
            alpha = jnp.exp(m - m_new)
            l = alpha * l
            acc = alpha * acc
            for d, s in enumerate(ss):
                p = jnp.exp(s - m_new)
                l = l + jnp.sum(p, axis=0, keepdims=True)
                vt = vt_ref[sb * per + d, hh * MLA_V:(hh + 1) * MLA_V, :]
                acc = acc + jnp.dot(vt, p.astype(BF16), preferred_element_type=F32)
            out.append((m_new, l, acc))
        return tuple(out)

    init = tuple((jnp.full((1, tq), NEG, F32), jnp.zeros((1, tq), F32), jnp.zeros((MLA_V, tq), F32))
                 for _ in range(nh))
    carries = lax.fori_loop(0, qi, lambda sb, c: step(sb, c, False), init)
    carries = step(qi, carries, True)
    for hh in range(nh):
        _, l, acc = carries[hh]
        o_ref[:, hh * MLA_V:(hh + 1) * MLA_V] = (acc * (1.0 / l)).T


def _mla_attention(q_t, k, v_t, batch, seq):
    t = k.shape[0]
    kb = v_t.shape[2]
    tq = _tile(seq, 512)
    nq = seq // tq
    nkb = seq // kb
    nh = 2
    blk = nh * (_nbytes((MLA_HEAD_PAD, tq), BF16) + _nbytes((seq, MLA_HEAD_PAD), BF16) + _nbytes((seq, MLA_V), BF16)
                + _nbytes((tq, MLA_V), F32))
    return pl.pallas_call(
        functools.partial(_mla_attn_kernel, tq=tq, kb=kb),
        grid=(batch, MLA_HEADS // nh, nq),
        in_specs=[
            pl.BlockSpec((nh * MLA_HEAD_PAD, tq), lambda b, h, i: (h, b * nq + i)),
            pl.BlockSpec((seq, nh * MLA_HEAD_PAD), lambda b, h, i: (b, h)),
            pl.BlockSpec((nkb, nh * MLA_V, kb), lambda b, h, i: (b, h, 0)),
        ],
        out_specs=pl.BlockSpec((tq, nh * MLA_V), lambda b, h, i: (b * nq + i, h)),
        out_shape=jax.ShapeDtypeStruct((t, MLA_WIDTH), F32),
        compiler_params=_cparams(("parallel", "parallel", "arbitrary"), blk, 8 * nh * _nbytes((tq, tq), F32)),
        name="mla_attention",
    )(q_t, k, v_t)


def _t5_bucket(dist):
    max_exact = REL_BUCKETS // 2
    d = jnp.maximum(dist, 1).astype(F32)
    large = max_exact + (jnp.log(d / max_exact) / math.log(REL_MAX_DIST / max_exact)
                         * (REL_BUCKETS - max_exact)).astype(jnp.int32)
    large = jnp.minimum(large, REL_BUCKETS - 1)
    return jnp.where(dist < max_exact, dist, large)


def _swa_kernel(sink_ref, q_ref, kp_ref, kc_ref, vp_ref, vc_ref, gfold_ref, bias_ref, o_ref):
    n = pl.program_id(1)
    nk, hd, kvw = 2 * BLOCK_Q, SWA_HEAD_DIM, SWA_KV_WIDTH
    key = lax.broadcasted_iota(jnp.int32, (nk, BLOCK_Q), 0)
    qry = lax.broadcasted_iota(jnp.int32, (nk, BLOCK_Q), 1)
    dist = qry + BLOCK_Q - key
    valid = (dist >= 0) & (dist < WINDOW) & ((key >= BLOCK_Q) | (n > 0))

    k2 = jnp.concatenate([kp_ref[...], kc_ref[...]], axis=0)
    sq = k2 * k2
    sq_hi = sq.astype(BF16)
    sq_lo = (sq - sq_hi.astype(F32)).astype(BF16)
    shift = hd.bit_length() - 1
    ri = lax.broadcasted_iota(jnp.int32, (kvw, kvw), 0) >> shift
    ci = lax.broadcasted_iota(jnp.int32, (kvw, kvw), 1) >> shift
    ind = jnp.where(ri == ci, 1.0, 0.0).astype(BF16)
    ss = jnp.dot(sq_hi, ind, preferred_element_type=F32) + jnp.dot(sq_lo, ind, preferred_element_type=F32)
    k2n = (k2 * lax.rsqrt(ss / float(hd) + EPS) * gfold_ref[...]).astype(BF16)
    v2t = jnp.concatenate([vp_ref[...], vc_ref[...]], axis=0).T.astype(BF16)

    qt = q_ref[...].T
    gw = SWA_GROUP * BLOCK_Q
    rows = []
    for kh in range(SWA_KV_HEADS):
        pieces = []
        if kh > 0:
            pieces.append(jnp.zeros((hd, kh * gw), BF16))
        for g in range(SWA_GROUP):
            j = kh * SWA_GROUP + g
            c = qt[j * hd:(j + 1) * hd, :]
            pieces.append((c * lax.rsqrt(jnp.sum(c * c, axis=0, keepdims=True) / float(hd) + EPS)).astype(BF16))
        if kh < SWA_KV_HEADS - 1:
            pieces.append(jnp.zeros((hd, (SWA_KV_HEADS - 1 - kh) * gw), BF16))
        rows.append(jnp.concatenate(pieces, axis=1))
    q_bd = jnp.concatenate(rows, axis=0)
    s_all = jnp.dot(k2n, q_bd, preferred_element_type=F32)

    es, invs = [], []
    for j in range(SWA_HEADS):
        lo = j * BLOCK_Q
        s = jnp.where(valid, s_all[:, lo:lo + BLOCK_Q] + bias_ref[:, lo:lo + BLOCK_Q], NEG)
        sink = sink_ref[j]
        m = jnp.maximum(jnp.max(s, axis=0, keepdims=True), sink)
        e = jnp.exp(s - m)
        invs.append(1.0 / (jnp.sum(e, axis=0, keepdims=True) + jnp.exp(sink - m)))
        es.append(e.astype(BF16))
    e_all = jnp.concatenate(es, axis=1)
    o_all = jnp.dot(v2t, e_all, preferred_element_type=F32)
    outs = []
    for j in range(SWA_HEADS):
        kh = j // SWA_GROUP
        outs.append(o_all[kh * hd:(kh + 1) * hd, j * BLOCK_Q:(j + 1) * BLOCK_Q] * invs[j])
    o_ref[...] = jnp.concatenate(outs, axis=0).T


def _swa_attention(u, batch, seq, q_norm, k_norm, sinks, rel_bias):
    t = u.shape[0]
    nb = seq // BLOCK_Q
    nk = 2 * BLOCK_Q
    by_dist = rel_bias.astype(F32)[_t5_bucket(jnp.arange(WINDOW))].T
    row = jnp.concatenate([jnp.zeros((SWA_HEADS, 1), F32), by_dist], 1)
    bias = jnp.broadcast_to(row[:, None, :], (SWA_HEADS, nk, BLOCK_Q + 1)).reshape(SWA_HEADS, -1)
    bias = bias[:, :nk * BLOCK_Q].reshape(SWA_HEADS, nk, BLOCK_Q)
    bias = bias.transpose(1, 0, 2).reshape(nk, SWA_HEADS * BLOCK_Q)
    gfold = jnp.tile(k_norm * q_norm * SWA_HEAD_DIM ** -0.5, SWA_KV_HEADS).reshape(1, SWA_KV_WIDTH)
    cur = lambda c: pl.BlockSpec((BLOCK_Q, SWA_KV_WIDTH), lambda b, n: (b * nb + n, c))
    prev = lambda c: pl.BlockSpec((BLOCK_Q, SWA_KV_WIDTH), lambda b, n: (b * nb + jnp.maximum(n - 1, 0), c))
    ck, cv = S_K // SWA_KV_WIDTH, S_V // SWA_KV_WIDTH
    blk = (2 * _nbytes((BLOCK_Q, SWA_WIDTH), F32) + 4 * _nbytes((BLOCK_Q, SWA_KV_WIDTH), F32) + _nbytes(bias.shape, F32))
    return pl.pallas_call(
        _swa_kernel,
        grid=(batch, nb),
        in_specs=[
            pl.BlockSpec(memory_space=pltpu.SMEM),
            pl.BlockSpec((BLOCK_Q, SWA_WIDTH), lambda b, n: (b * nb + n, 0)),
            prev(ck), cur(ck), prev(cv), cur(cv),
            pl.BlockSpec((1, SWA_KV_WIDTH), lambda b, n: (0, 0)),
            pl.BlockSpec(bias.shape, lambda b, n: (0, 0)),
        ],
        out_specs=pl.BlockSpec((BLOCK_Q, SWA_WIDTH), lambda b, n: (b * nb + n, 0)),
        out_shape=jax.ShapeDtypeStruct((t, SWA_WIDTH), F32),
        compiler_params=_cparams(("parallel", "arbitrary"), blk, 4 * _nbytes((nk, SWA_HEADS * BLOCK_Q), F32)),
        name="swa_attention",
    )(sinks, u, u, u, u, u, gfold, bias)


def _mem_attn_kernel(q_ref, kv_ref, gq_ref, gk_ref, o_ref):
    gq = gq_ref[...] * (MEM_HEAD_DIM ** -0.5)
    gk = gk_ref[...]
    for h in range(MEM_HEADS):
        q = _rms_rows(q_ref[:, h * MEM_HEAD_DIM:(h + 1) * MEM_HEAD_DIM], gq).astype(BF16)
        lo = 2 * h * MEM_HEAD_DIM
        k = _rms_rows(kv_ref[:, lo:lo + MEM_HEAD_DIM], gk).astype(BF16)
        v = kv_ref[:, lo + MEM_HEAD_DIM:lo + 2 * MEM_HEAD_DIM].astype(BF16)
        s = lax.dot_general(q, k, (((1,), (1,)), ((), ())), preferred_element_type=F32)
        e = jnp.exp(s - jnp.max(s, axis=-1, keepdims=True))
        p = (e / jnp.sum(e, axis=-1, keepdims=True)).astype(BF16)
        o_ref[:, h * MEM_HEAD_DIM:(h + 1) * MEM_HEAD_DIM] = jnp.dot(p, v, preferred_element_type=F32).astype(o_ref.dtype)


def _mem_attention(qm, kvm, batch, seq, mem_len, q_norm, k_norm):
    t, w = qm.shape
    tq = _tile(seq, 512)
    nq = seq // tq
    blk = _nbytes((tq, w), F32) + _nbytes((mem_len, 2 * w), F32) + _nbytes((tq, w), BF16)
    return pl.pallas_call(
        _mem_attn_kernel,
        grid=(batch, nq),
        in_specs=[
            pl.BlockSpec((tq, w), lambda b, i: (b * nq + i, 0)),
            pl.BlockSpec((mem_len, 2 * w), lambda b, i: (b, 0)),
            pl.BlockSpec((1, MEM_HEAD_DIM), lambda b, i: (0, 0)),
            pl.BlockSpec((1, MEM_HEAD_DIM), lambda b, i: (0, 0)),
        ],
        out_specs=pl.BlockSpec((tq, w), lambda b, i: (b * nq + i, 0)),
        out_shape=jax.ShapeDtypeStruct((t, w), BF16),
        compiler_params=_cparams(("parallel", "arbitrary"), blk, 8 * _nbytes((tq, mem_len), F32)),
        name="mem_attention",
    )(qm, kvm, q_norm.reshape(1, -1), k_norm.reshape(1, -1))


def _layer(h, mem, rel_bias, p, batch, seq):
    h = _swiglu_half(h, p['ffn_a_norm'], p['ffn_a_gate'], p['ffn_a_up'], p['ffn_a_down'])

    xn = _norm(h, p['mix_norm'])
    w_in_t = p['w_in'].T
    u_mla = _mm_wt32(xn, w_in_t, row0=0, n=U_MLA, tm=1024, tn=512, name="in_proj_mla")
    u_swa = _mm_wt32(xn, w_in_t, row0=IN_SWA, n=w_in_t.shape[0] - IN_SWA, tm=1024, tn=512, name="in_proj_swa")

    q_t, k, v_t = _mla_prep(u_mla, seq, p['mla_q_a_norm'], p['mla_kv_a_norm'], p['mla_w_uq'], p['mla_w_ukv'],
                            p['mla_q_norm'], p['mla_k_norm'])
    o_a = _mla_attention(q_t, k, v_t, batch, seq)
    o_b = _swa_attention(u_swa, batch, seq, p['swa_q_norm'], p['swa_k_norm'], p['swa_sinks'], rel_bias)
    o = _dual_norm(o_a, o_b, p['out_norm_mla'], p['out_norm_swa'])
    h = _mm_w32(o, p['w_out'], h, tm=1024, tn=512, name="out_proj")

    mem_len = mem.shape[0] // batch
    qm = _mm_w32(_norm(h, p['mem_attn_norm']), p['mem_w_q'], tm=1024, tn=512, name="mem_q_proj")
    kvm = _mm_w32(_norm(mem, p['mem_norm']), p['mem_w_kv'], tm=512, tn=512, name="mem_kv_proj")
    om = _mem_attention(qm, kvm, batch, seq, mem_len, p['mem_q_norm'], p['mem_k_norm'])
    h = _mm_w32(om, p['mem_w_o'], h, tm=1024, tn=512, name="mem_o_proj")

    return _swiglu_half(h, p['ffn_b_norm'], p['ffn_b_gate'], p['ffn_b_up'], p['ffn_b_down'])


_PARAM_NAMES = (
    'ffn_a_norm', 'ffn_a_gate', 'ffn_a_up', 'ffn_a_down', 'mix_norm', 'w_in',
    'mla_q_a_norm', 'mla_kv_a_norm', 'mla_w_uq', 'mla_w_ukv', 'mla_q_norm', 'mla_k_norm',
    'swa_q_norm', 'swa_k_norm', 'swa_sinks', 'out_norm_mla', 'out_norm_swa', 'w_out',
    'mem_attn_norm', 'mem_norm', 'mem_w_q', 'mem_w_kv', 'mem_q_norm', 'mem_k_norm', 'mem_w_o',
    'ffn_b_norm', 'ffn_b_gate', 'ffn_b_up', 'ffn_b_down',
)


def kernel(x, mem, rel_bias, ffn_a_norm, ffn_a_gate, ffn_a_up, ffn_a_down, mix_norm, w_in, mla_q_a_norm, mla_kv_a_norm, mla_w_uq, mla_w_ukv, mla_q_norm, mla_k_norm, swa_q_norm, swa_k_norm, swa_sinks, out_norm_mla, out_norm_swa, w_out, mem_attn_norm, mem_norm, mem_w_q, mem_w_kv, mem_q_norm, mem_k_norm, mem_w_o, ffn_b_norm, ffn_b_gate, ffn_b_up, ffn_b_down):
    stacked = (ffn_a_norm, ffn_a_gate, ffn_a_up, ffn_a_down, mix_norm, w_in, mla_q_a_norm, mla_kv_a_norm,
               mla_w_uq, mla_w_ukv, mla_q_norm, mla_k_norm, swa_q_norm, swa_k_norm, swa_sinks, out_norm_mla,
               out_norm_swa, w_out, mem_attn_norm, mem_norm, mem_w_q, mem_w_kv, mem_q_norm, mem_k_norm, mem_w_o,
               ffn_b_norm, ffn_b_gate, ffn_b_up, ffn_b_down)
    batch, seq, d = x.shape
    h = x.reshape(batch * seq, d)
    mem2 = mem.reshape(-1, d)
    for layer in range(ffn_a_norm.shape[0]):
        p = {name: arr[layer] for name, arr in zip(_PARAM_NAMES, stacked)}
        h = _layer(h, mem2, rel_bias, p, batch, seq)
    return h.reshape(batch, seq, d)
```

```python
import functools
import math

import jax
import jax.numpy as jnp
from jax import lax
from jax.experimental import pallas as pl
from jax.experimental.pallas import tpu as pltpu

F32 = jnp.float32
BF16 = jnp.bfloat16

MEM_HEADS = 4
MEM_HEAD_DIM = 128
MLA_HEADS = 16
MLA_Q_RANK = 896
MLA_KV_RANK = 512
MLA_NOPE = 128
MLA_ROPE = 64
MLA_V = 128
ROPE_THETA = 10000.0
SWA_HEADS = 32
SWA_KV_HEADS = 4
SWA_HEAD_DIM = 64
WINDOW = 128
REL_BUCKETS = 32
REL_MAX_DIST = 128
BLOCK_Q = 128
EPS = 1e-6
NEG = -1e30

MLA_QK = MLA_NOPE + MLA_ROPE
SWA_GROUP = SWA_HEADS // SWA_KV_HEADS
SWA_WIDTH = SWA_HEADS * SWA_HEAD_DIM
SWA_KV_WIDTH = SWA_KV_HEADS * SWA_HEAD_DIM
MLA_WIDTH = MLA_HEADS * MLA_V

LANES = 128
MLA_HEAD_PAD = 2 * LANES
ROPE_PAD = LANES

U_CKV = MLA_Q_RANK
U_KR = U_CKV + MLA_KV_RANK
U_MLA = U_KR + ROPE_PAD
IN_SWA = U_KR + MLA_ROPE
S_K = SWA_WIDTH
S_V = S_K + SWA_KV_WIDTH

KEY_BLOCK = 256

V7X_SCOPED_VMEM_BYTES = 60000 * 1024


def _tile(n, pref, align=8):
    if n <= pref:
        return n
    t = (pref // align) * align
    while t >= align:
        if n % t == 0:
            return t
        t -= align
    return n


def _nbytes(shape, dtype):
    n = 1
    for s in shape:
        n *= s
    return n * jnp.dtype(dtype).itemsize


def _cparams(semantics, pipelined_bytes, temp_bytes=0):
    need = 2 * pipelined_bytes + temp_bytes + (4 << 20)
    return pltpu.CompilerParams(
        dimension_semantics=semantics,
        vmem_limit_bytes=int(min(max(need, 16 << 20), V7X_SCOPED_VMEM_BYTES)),
    )


def _rms_rows(x, g, n=None):
    n = x.shape[-1] if n is None else n
    r = lax.rsqrt(jnp.sum(x * x, axis=-1, keepdims=True) / float(n) + EPS)
    return x * r * g


def _norm_kernel(x_ref, g_ref, o_ref):
    o_ref[...] = _rms_rows(x_ref[...], g_ref[...]).astype(o_ref.dtype)


def _norm(x, g, out_dtype=BF16):
    rows, d = x.shape
    tr = _tile(rows, 512)
    blk = _nbytes((tr, d), F32) + _nbytes((tr, d), out_dtype)
    return pl.pallas_call(
        _norm_kernel,
        grid=(rows // tr,),
        in_specs=[pl.BlockSpec((tr, d), lambda i: (i, 0)), pl.BlockSpec((1, d), lambda i: (0, 0))],
        out_specs=pl.BlockSpec((tr, d), lambda i: (i, 0)),
        out_shape=jax.ShapeDtypeStruct((rows, d), out_dtype),
        compiler_params=_cparams(("parallel",), blk, _nbytes((tr, d), F32)),
        name="rmsnorm",
    )(x, g.reshape(1, d))


def _dual_norm_kernel(a_ref, b_ref, ga_ref, gb_ref, o_ref):
    wa = a_ref.shape[-1]
    o_ref[:, :wa] = _rms_rows(a_ref[...], ga_ref[...]).astype(o_ref.dtype)
    o_ref[:, wa:] = _rms_rows(b_ref[...], gb_ref[...]).astype(o_ref.dtype)


def _dual_norm(a, b, ga, gb):
    rows, wa = a.shape
    wb = b.shape[1]
    tr = _tile(rows, 512)
    blk = _nbytes((tr, wa + wb), F32) + _nbytes((tr, wa + wb), BF16)
    return pl.pallas_call(
        _dual_norm_kernel,
        grid=(rows // tr,),
        in_specs=[
            pl.BlockSpec((tr, wa), lambda i: (i, 0)),
            pl.BlockSpec((tr, wb), lambda i: (i, 0)),
            pl.BlockSpec((1, wa), lambda i: (0, 0)),
            pl.BlockSpec((1, wb), lambda i: (0, 0)),
        ],
        out_specs=pl.BlockSpec((tr, wa + wb), lambda i: (i, 0)),
        out_shape=jax.ShapeDtypeStruct((rows, wa + wb), BF16),
        compiler_params=_cparams(("parallel",), blk, _nbytes((tr, wa + wb), F32)),
        name="out_norms",
    )(a, b, ga.reshape(1, wa), gb.reshape(1, wb))


def _mm_kernel(a_ref, w_ref, o_ref):
    o_ref[...] = jnp.dot(a_ref[...], w_ref[...], preferred_element_type=F32).astype(o_ref.dtype)


def _mm_res_kernel(a_ref, w_ref, r_ref, o_ref):
    o_ref[...] = r_ref[...] + jnp.dot(a_ref[...], w_ref[...], preferred_element_type=F32)


def _mm(a, w, res=None, *, tm, tn, out_dtype=F32, name="matmul"):
    m, k = a.shape
    n = w.shape[1]
    tm, tn = _tile(m, tm), _tile(n, tn, LANES)
    blk = _nbytes((tm, k), a.dtype) + _nbytes((k, tn), w.dtype) + _nbytes((tm, tn), out_dtype)
    in_specs = [pl.BlockSpec((tm, k), lambda i, j: (i, 0)), pl.BlockSpec((k, tn), lambda i, j: (0, j))]
    args = [a, w]
    kern = _mm_kernel
    if res is not None:
        in_specs.append(pl.BlockSpec((tm, tn), lambda i, j: (i, j)))
        args.append(res)
        blk += _nbytes((tm, tn), F32)
        kern = _mm_res_kernel
    return pl.pallas_call(
        kern,
        grid=(m // tm, n // tn),
        in_specs=in_specs,
        out_specs=pl.BlockSpec((tm, tn), lambda i, j: (i, j)),
        out_shape=jax.ShapeDtypeStruct((m, n), out_dtype),
        compiler_params=_cparams(("parallel", "arbitrary"), blk, 2 * _nbytes((tm, tn), F32)),
        name=name,
    )(*args)


def _mm_w32_kernel(a_ref, w_ref, *rest, has_res):
    if has_res:
        r_ref, o_ref, wb_ref = rest
    else:
        o_ref, wb_ref = rest

    @pl.when(pl.program_id(1) == 0)
    def _():
        wb_ref[...] = w_ref[...].astype(BF16)

    acc = jnp.dot(a_ref[...], wb_ref[...], preferred_element_type=F32)
    if has_res:
        acc = r_ref[...] + acc
    o_ref[...] = acc.astype(o_ref.dtype)


def _mm_w32(a, w, res=None, *, tm, tn, n=None, out_dtype=F32, name="matmul"):
    m, k = a.shape
    n = w.shape[1] if n is None else n
    tm, tn = _tile(m, tm), _tile(n, tn, LANES)
    blk = _nbytes((tm, k), a.dtype) + _nbytes((k, tn), F32) + _nbytes((tm, tn), out_dtype)
    in_specs = [pl.BlockSpec((tm, k), lambda j, i: (i, 0)), pl.BlockSpec((k, tn), lambda j, i: (0, j))]
    args = [a, w]
    if res is not None:
        in_specs.append(pl.BlockSpec((tm, tn), lambda j, i: (i, j)))
        args.append(res)
        blk += _nbytes((tm, tn), F32)
    return pl.pallas_call(
        functools.partial(_mm_w32_kernel, has_res=res is not None),
        grid=(n // tn, m // tm),
        in_specs=in_specs,
        out_specs=pl.BlockSpec((tm, tn), lambda j, i: (i, j)),
        out_shape=jax.ShapeDtypeStruct((m, n), out_dtype),
        scratch_shapes=[pltpu.VMEM((k, tn), BF16)],
        compiler_params=_cparams(("parallel", "arbitrary"), blk, _nbytes((k, tn), BF16) + 2 * _nbytes((tm, tn), F32)),
        name=name,
    )(*args)


def _mm_wt32_kernel(a_ref, wt_ref, o_ref, wb_ref):
    @pl.when(pl.program_id(1) == 0)
    def _():
        wb_ref[...] = wt_ref[...].astype(BF16)

    o_ref[...] = lax.dot_general(a_ref[...], wb_ref[...], (((1,), (1,)), ((), ())),
                                 preferred_element_type=F32).astype(o_ref.dtype)


def _mm_wt32(a, wt, *, row0, n, tm, tn, name):
    m, k = a.shape
    tm, tn = _tile(m, tm), _tile(n, tn, LANES)
    blk = _nbytes((tm, k), a.dtype) + _nbytes((tn, k), F32) + _nbytes((tm, tn), F32)
    return pl.pallas_call(
        _mm_wt32_kernel,
        grid=(n // tn, m // tm),
        in_specs=[
            pl.BlockSpec((tm, k), lambda j, i: (i, 0)),
            pl.BlockSpec((pl.Element(tn), pl.Element(k)), lambda j, i: (pl.multiple_of(row0 + j * tn, 8), 0)),
        ],
        out_specs=pl.BlockSpec((tm, tn), lambda j, i: (i, j)),
        out_shape=jax.ShapeDtypeStruct((m, n), F32),
        scratch_shapes=[pltpu.VMEM((tn, k), BF16)],
        compiler_params=_cparams(("parallel", "arbitrary"), blk, _nbytes((tn, k), BF16) + 2 * _nbytes((tm, tn), F32)),
        name=name,
    )(a, wt)


def _gated_kernel(x_ref, wg_ref, wu_ref, wd_ref, o_ref, wdb_ref):
    @pl.when(pl.program_id(0) == 0)
    def _():
        wdb_ref[...] = wd_ref[...].astype(BF16)

    x = x_ref[...]
    g = jnp.dot(x, wg_ref[...].astype(BF16), preferred_element_type=F32)
    u = jnp.dot(x, wu_ref[...].astype(BF16), preferred_element_type=F32)
    o_ref[...] = (0.5 * (g / (1.0 + jnp.exp(-g))) * u).astype(o_ref.dtype)


def _gated(x, wg, wu, wd, *, tm, tn):
    m, k = x.shape
    n = wg.shape[1]
    d_out = wd.shape[1]
    tm, tn = _tile(m, tm), _tile(n, tn, LANES)
    nj = n // tn
    blk = (_nbytes((tm, k), BF16) + 2 * _nbytes((k, tn), F32) + _nbytes((tm, tn), BF16)
           + _nbytes((tn, d_out), F32) + _nbytes((tn, d_out), BF16))
    wd_map = lambda i, j: (jnp.where(i == 0, j, nj - 1), 0)
    return pl.pallas_call(
        _gated_kernel,
        grid=(m // tm, nj),
        in_specs=[
            pl.BlockSpec((tm, k), lambda i, j: (i, 0)),
            pl.BlockSpec((k, tn), lambda i, j: (0, j)),
            pl.BlockSpec((k, tn), lambda i, j: (0, j)),
            pl.BlockSpec((tn, d_out), wd_map),
        ],
        out_specs=[pl.BlockSpec((tm, tn), lambda i, j: (i, j)), pl.BlockSpec((tn, d_out), wd_map)],
        out_shape=[jax.ShapeDtypeStruct((m, n), BF16), jax.ShapeDtypeStruct((n, d_out), BF16)],
        compiler_params=_cparams(("arbitrary", "arbitrary"), blk, 2 * _nbytes((k, tn), BF16) + 4 * _nbytes((tm, tn), F32)),
        name="ffn_gate_up",
    )(x, wg, wu, wd)


def _swiglu_half(h, norm_g, wg, wu, wd):
    xn = _norm(h, norm_g)
    act, wd_bf = _gated(xn, wg, wu, wd, tm=1024, tn=256)
    return _mm(act, wd_bf, h, tm=512, tn=512, name="ffn_down")


def _rope_chunk(y, cos, s1, s2):
    return y * cos + pltpu.roll(y, ROPE_PAD - MLA_ROPE // 2, 1) * s1 + pltpu.roll(y, MLA_ROPE // 2, 1) * s2


def _mla_q_kernel(u_ref, ga_ref, wt_ref, gt_ref, cos_ref, sin_ref, o_ref):
    cn = _rms_rows(u_ref[:, :MLA_Q_RANK], ga_ref[...]).astype(BF16)
    qt = lax.dot_general(wt_ref[...], cn, (((1,), (1,)), ((), ())), preferred_element_type=F32)
    cos, sin, gain = cos_ref[...], sin_ref[...], gt_ref[...]
    half = MLA_ROPE // 2
    zeros = jnp.zeros((MLA_HEAD_PAD - MLA_QK, qt.shape[1]), o_ref.dtype)
    for h in range(MLA_HEADS):
        lo = h * MLA_HEAD_PAD
        c = qt[lo:lo + MLA_HEAD_PAD, :]
        y = c * lax.rsqrt(jnp.sum(c * c, axis=0, keepdims=True) / float(MLA_QK) + EPS) * gain
        x1, x2 = y[MLA_NOPE:MLA_NOPE + half, :], y[MLA_NOPE + half:MLA_QK, :]
        o_ref[lo:lo + MLA_NOPE, :] = y[:MLA_NOPE, :].astype(o_ref.dtype)
        o_ref[lo + MLA_NOPE:lo + MLA_NOPE + half, :] = (x1 * cos - x2 * sin).astype(o_ref.dtype)
        o_ref[lo + MLA_NOPE + half:lo + MLA_QK, :] = (x2 * cos + x1 * sin).astype(o_ref.dtype)
        o_ref[lo + MLA_QK:lo + MLA_HEAD_PAD, :] = zeros


def _mla_kv_kernel(u_ref, ga_ref, wk_ref, wvt_ref, gn_ref, gr_ref, cos_ref, s1_ref, s2_ref, k_ref, vt_ref):
    cn = _rms_rows(u_ref[:, U_CKV:U_KR], ga_ref[...]).astype(BF16)
    kv = jnp.dot(cn, wk_ref[...], preferred_element_type=F32)
    vt = lax.dot_general(wvt_ref[...], cn, (((1,), (1,)), ((), ())), preferred_element_type=F32)
    vt_ref[...] = vt.astype(vt_ref.dtype)
    kr = u_ref[:, U_KR:U_MLA]
    kr = jnp.where(lax.broadcasted_iota(jnp.int32, kr.shape, 1) < MLA_ROPE, kr, 0.0)
    ss_r = jnp.sum(kr * kr, axis=-1, keepdims=True)
    cos, s1, s2 = cos_ref[...], s1_ref[...], s2_ref[...]
    gn, gr = gn_ref[...], gr_ref[...]
    for h in range(MLA_HEADS):
        kn = kv[:, h * MLA_NOPE:(h + 1) * MLA_NOPE]
        r = lax.rsqrt((jnp.sum(kn * kn, axis=-1, keepdims=True) + ss_r) / float(MLA_QK) + EPS)
        lo = h * MLA_HEAD_PAD
        k_ref[:, lo:lo + MLA_NOPE] = (kn * r * gn).astype(k_ref.dtype)
        k_ref[:, lo + MLA_NOPE:lo + MLA_HEAD_PAD] = _rope_chunk(kr * r * gr, cos, s1, s2).astype(k_ref.dtype)


def _rope_tables(seq):
    half = MLA_ROPE // 2
    inv = ROPE_THETA ** (-jnp.arange(half, dtype=F32) / half)
    ang = jnp.arange(seq).astype(F32)[:, None] * inv[None, :]
    cos, sin = jnp.cos(ang), jnp.sin(ang)
    z = jnp.zeros_like(cos)
    pad = jnp.zeros((seq, ROPE_PAD - MLA_ROPE), F32)
    return (jnp.concatenate([cos, cos, pad], 1), jnp.concatenate([-sin, z, pad], 1),
            jnp.concatenate([z, sin, pad], 1), cos.T, sin.T)


def _mla_prep(u, seq, q_a_norm, kv_a_norm, w_uq, w_ukv, q_norm, k_norm):
    t = u.shape[0]
    tm = _tile(seq, KEY_BLOCK)
    spb = seq // tm
    cos, s1, s2, cos_t, sin_t = _rope_tables(seq)
    tab_spec = pl.BlockSpec((tm, ROPE_PAD), lambda i: (i % spb, 0))
    tab_t_spec = pl.BlockSpec((MLA_ROPE // 2, tm), lambda i: (0, i % spb))
    qw = MLA_HEADS * MLA_HEAD_PAD
    vw = MLA_HEADS * MLA_V

    wq_t = jnp.pad(w_uq.reshape(MLA_Q_RANK, MLA_HEADS, MLA_QK), ((0, 0), (0, 0), (0, MLA_HEAD_PAD - MLA_QK)))
    wq_t = wq_t.reshape(MLA_Q_RANK, qw).T.astype(BF16)
    gain_t = jnp.concatenate([q_norm * MLA_QK ** -0.5, jnp.zeros((MLA_HEAD_PAD - MLA_QK,), F32)])
    gain_t = jnp.broadcast_to(gain_t[:, None], (MLA_HEAD_PAD, tm))
    blk = (_nbytes((tm, U_MLA), F32) + _nbytes(wq_t.shape, BF16) + _nbytes((qw, tm), BF16)
           + _nbytes((MLA_HEAD_PAD, tm), F32) + 2 * _nbytes((MLA_ROPE // 2, tm), F32))
    q_t = pl.pallas_call(
        _mla_q_kernel,
        grid=(t // tm,),
        in_specs=[
            pl.BlockSpec((tm, U_MLA), lambda i: (i, 0)),
            pl.BlockSpec((1, MLA_Q_RANK), lambda i: (0, 0)),
            pl.BlockSpec(wq_t.shape, lambda i: (0, 0)),
            pl.BlockSpec((MLA_HEAD_PAD, tm), lambda i: (0, 0)),
            tab_t_spec, tab_t_spec,
        ],
        out_specs=pl.BlockSpec((qw, tm), lambda i: (0, i)),
        out_shape=jax.ShapeDtypeStruct((qw, t), BF16),
        compiler_params=_cparams(("parallel",), blk, 3 * _nbytes((qw, tm), F32)),
        name="mla_q_prep",
    )(u, q_a_norm.reshape(1, -1), wq_t, gain_t, cos_t, sin_t)

    wkv = w_ukv.reshape(MLA_KV_RANK, MLA_HEADS, MLA_NOPE + MLA_V)
    wk = wkv[:, :, :MLA_NOPE].reshape(MLA_KV_RANK, -1).astype(BF16)
    wv_t = wkv[:, :, MLA_NOPE:].reshape(MLA_KV_RANK, -1).T.astype(BF16)
    blk = (_nbytes((tm, U_MLA), F32) + _nbytes(wk.shape, BF16) + _nbytes(wv_t.shape, BF16) + _nbytes((tm, qw), BF16)
           + _nbytes((vw, tm), BF16) + 3 * _nbytes((tm, ROPE_PAD), F32))
    k, v_t = pl.pallas_call(
        _mla_kv_kernel,
        grid=(t // tm,),
        in_specs=[
            pl.BlockSpec((tm, U_MLA), lambda i: (i, 0)),
            pl.BlockSpec((1, MLA_KV_RANK), lambda i: (0, 0)),
            pl.BlockSpec(wk.shape, lambda i: (0, 0)),
            pl.BlockSpec(wv_t.shape, lambda i: (0, 0)),
            pl.BlockSpec((1, MLA_NOPE), lambda i: (0, 0)),
            pl.BlockSpec((1, ROPE_PAD), lambda i: (0, 0)),
            tab_spec, tab_spec, tab_spec,
        ],
        out_specs=[pl.BlockSpec((tm, qw), lambda i: (i, 0)), pl.BlockSpec((None, vw, tm), lambda i: (i, 0, 0))],
        out_shape=[jax.ShapeDtypeStruct((t, qw), BF16), jax.ShapeDtypeStruct((t // tm, vw, tm), BF16)],
        compiler_params=_cparams(("parallel",), blk, 3 * _nbytes((tm, qw), F32)),
        name="mla_kv_prep",
    )(u, kv_a_norm.reshape(1, -1), wk, wv_t, k_norm[:MLA_NOPE].reshape(1, -1),
      jnp.concatenate([k_norm[MLA_NOPE:], jnp.zeros((ROPE_PAD - MLA_ROPE,), F32)]).reshape(1, -1), cos, s1, s2)
    return q_t, k, v_t


def _mla_attn_kernel(qt_ref, k_ref, vt_ref, o_ref, sa_ref, sb_ref, *, tq, kb):
    qi = pl.program_id(2)
    per = tq // kb
    nh = qt_ref.shape[0] // MLA_HEAD_PAD
    qts = [qt_ref[hh * MLA_HEAD_PAD:(hh + 1) * MLA_HEAD_PAD, :] for hh in range(nh)]

    def scores(blk, s_ref):
        for hh in range(nh):
            for d in range(per):
                start = pl.multiple_of((blk * per + d) * kb, kb)
                kblk = k_ref[pl.ds(start, kb), hh * MLA_HEAD_PAD:(hh + 1) * MLA_HEAD_PAD]
                s_ref[hh, d * kb:(d + 1) * kb, :] = jnp.dot(kblk, qts[hh], preferred_element_type=F32)

    def softmax_pv(blk, s_ref, carries, diagonal):
        out = []
        for hh in range(nh):
            m, l, acc = carries[hh]
            ss = []
            for d in range(per):
                s = s_ref[hh, d * kb:(d + 1) * kb, :]
                if diagonal:
                    key = d * kb + lax.broadcasted_iota(jnp.int32, s.shape, 0)
                    qry = lax.broadcasted_iota(jnp.int32, s.shape, 1)
                    s = jnp.where(key <= qry, s, NEG)
                ss.append(s)
            m_new = m
            for s in ss:
                m_new = jnp.maximum(m_new, jnp.max(s, axis=0, keepdims=True))
            alpha = jnp.exp(m - m_new)
            l = alpha * l
            acc = alpha * acc
            for d, s in enumerate(ss):
                p = jnp.exp(s - m_new)
                l = l + jnp.sum(p, axis=0, keepdims=True)
                vt = vt_ref[blk * per + d, hh * MLA_V:(hh + 1) * MLA_V, :]
                acc = acc + jnp.dot(vt, p.astype(BF16), preferred_element_type=F32)
            out.append((m_new, l, acc))
        return tuple(out)

    def pair(t, carries):
        scores(2 * t + 1, sb_ref)
        carries = softmax_pv(2 * t, sa_ref, carries, False)
        scores(2 * t + 2, sa_ref)
        return softmax_pv(2 * t + 1, sb_ref, carries, False)

    def tail_even(carries):
        return softmax_pv(qi, sa_ref, carries, True)

    def tail_odd(carries):
        scores(qi, sb_ref)
        carries = softmax_pv(qi - 1, sa_ref, carries, False)
        return softmax_pv(qi, sb_ref, carries, True)

    init = tuple((jnp.full((1, tq), NEG, F32), jnp.zeros((1, tq), F32), jnp.zeros((MLA_V, tq), F32))
                 for _ in range(nh))
    scores(0, sa_ref)
    carries = lax.fori_loop(0, qi >> 1, pair, init)
    carries = lax.cond((qi & 1) == 0, tail_even, tail_odd, carries)
    for hh in range(nh):
        _, l, acc = carries[hh]
        o_ref[:, hh * MLA_V:(hh + 1) * MLA_V] = (acc * (1.0 / l)).T


def _mla_attention(q_t, k, v_t, batch, seq):
    t = k.shape[0]
    kb = v_t.shape[2]
    tq = _tile(seq, 512)
    nq = seq // tq
    nkb = seq // kb
    nh = 2
    blk = nh * (_nbytes((MLA_HEAD_PAD, tq), BF16) + _nbytes((seq, MLA_HEAD_PAD), BF16) + _nbytes((seq, MLA_V), BF16)
                + _nbytes((tq, MLA_V), F32))
    return pl.pallas_call(
        functools.partial(_mla_attn_kernel, tq=tq, kb=kb),
        grid=(batch, MLA_HEADS // nh, nq),
        in_specs=[
            pl.BlockSpec((nh * MLA_HEAD_PAD, tq), lambda b, h, i: (h, b * nq + i)),
            pl.BlockSpec((seq, nh * MLA_HEAD_PAD), lambda b, h, i: (b, h)),
            pl.BlockSpec((nkb, nh * MLA_V, kb), lambda b, h, i: (b, h, 0)),
        ],
        out_specs=pl.BlockSpec((tq, nh * MLA_V), lambda b, h, i: (b * nq + i, h)),
        out_shape=jax.ShapeDtypeStruct((t, MLA_WIDTH), F32),
        scratch_shapes=[pltpu.VMEM((nh, tq, tq), F32), pltpu.VMEM((nh, tq, tq), F32)],
        compiler_params=_cparams(("parallel", "parallel", "arbitrary"), blk, 10 * nh * _nbytes((tq, tq), F32)),
        name="mla_attention",
    )(q_t, k, v_t)


def _t5_bucket(dist):
    max_exact = REL_BUCKETS // 2
    d = jnp.maximum(dist, 1).astype(F32)
    large = max_exact + (jnp.log(d / max_exact) / math.log(REL_MAX_DIST / max_exact)
                         * (REL_BUCKETS - max_exact)).astype(jnp.int32)
    large = jnp.minimum(large, REL_BUCKETS - 1)
    return jnp.where(dist < max_exact, dist, large)


def _swa_kernel(sink_ref, q_ref, kp_ref, kc_ref, vp_ref, vc_ref, gfold_ref, bias_ref, o_ref):
    n = pl.program_id(1)
    nk, hd, kvw = 2 * BLOCK_Q, SWA_HEAD_DIM, SWA_KV_WIDTH
    key = lax.broadcasted_iota(jnp.int32, (nk, BLOCK_Q), 0)
    qry = lax.broadcasted_iota(jnp.int32, (nk, BLOCK_Q), 1)
    dist = qry + BLOCK_Q - key
    valid = (dist >= 0) & (dist < WINDOW) & ((key >= BLOCK_Q) | (n > 0))

    k2 = jnp.concatenate([kp_ref[...], kc_ref[...]], axis=0)
    sq = k2 * k2
    sq_hi = sq.astype(BF16)
    sq_lo = (sq - sq_hi.astype(F32)).astype(BF16)
    shift = hd.bit_length() - 1
    ri = lax.broadcasted_iota(jnp.int32, (kvw, kvw), 0) >> shift
    ci = lax.broadcasted_iota(jnp.int32, (kvw, kvw), 1) >> shift
    ind = jnp.where(ri == ci, 1.0, 0.0).astype(BF16)
    ss = jnp.dot(sq_hi, ind, preferred_element_type=F32) + jnp.dot(sq_lo, ind, preferred_element_type=F32)
    k2n = (k2 * lax.rsqrt(ss / float(hd) + EPS) * gfold_ref[...]).astype(BF16)
    v2t = jnp.concatenate([vp_ref[...], vc_ref[...]], axis=0).T.astype(BF16)

    qt = q_ref[...].T
    gw = SWA_GROUP * BLOCK_Q
    rows = []
    for kh in range(SWA_KV_HEADS):
        pieces = []
        if kh > 0:
            pieces.append(jnp.zeros((hd, kh * gw), BF16))
        for g in range(SWA_GROUP):
            j = kh * SWA_GROUP + g
            c = qt[j * hd:(j + 1) * hd, :]
            pieces.append((c * lax.rsqrt(jnp.sum(c * c, axis=0, keepdims=True) / float(hd) + EPS)).astype(BF16))
        if kh < SWA_KV_HEADS - 1:
            pieces.append(jnp.zeros((hd, (SWA_KV_HEADS - 1 - kh) * gw), BF16))
        rows.append(jnp.concatenate(pieces, axis=1))
    q_bd = jnp.concatenate(rows, axis=0)
    s_all = jnp.dot(k2n, q_bd, preferred_element_type=F32)

    es, invs = [], []
    for j in range(SWA_HEADS):
        lo = j * BLOCK_Q
        s = jnp.where(valid, s_all[:, lo:lo + BLOCK_Q] + bias_ref[:, lo:lo + BLOCK_Q], NEG)
        sink = sink_ref[j]
        m = jnp.maximum(jnp.max(s, axis=0, keepdims=True), sink)
        e = jnp.exp(s - m)
        invs.append(1.0 / (jnp.sum(e, axis=0, keepdims=True) + jnp.exp(sink - m)))
        es.append(e.astype(BF16))
    e_all = jnp.concatenate(es, axis=1)
    o_all = jnp.dot(v2t, e_all, preferred_element_type=F32)
    outs = []
    for j in range(SWA_HEADS):
        kh = j // SWA_GROUP
        outs.append(o_all[kh * hd:(kh + 1) * hd, j * BLOCK_Q:(j + 1) * BLOCK_Q] * invs[j])
    o_ref[...] = jnp.concatenate(outs, axis=0).T


def _swa_attention(u, batch, seq, q_norm, k_norm, sinks, rel_bias):
    t = u.shape[0]
    nb = seq // BLOCK_Q
    nk = 2 * BLOCK_Q
    by_dist = rel_bias.astype(F32)[_t5_bucket(jnp.arange(WINDOW))].T
    row = jnp.concatenate([jnp.zeros((SWA_HEADS, 1), F32), by_dist], 1)
    bias = jnp.broadcast_to(row[:, None, :], (SWA_HEADS, nk, BLOCK_Q + 1)).reshape(SWA_HEADS, -1)
    bias = bias[:, :nk * BLOCK_Q].reshape(SWA_HEADS, nk, BLOCK_Q)
    bias = bias.transpose(1, 0, 2).reshape(nk, SWA_HEADS * BLOCK_Q)
    gfold = jnp.tile(k_norm * q_norm * SWA_HEAD_DIM ** -0.5, SWA_KV_HEADS).reshape(1, SWA_KV_WIDTH)
    cur = lambda c: pl.BlockSpec((BLOCK_Q, SWA_KV_WIDTH), lambda b, n: (b * nb + n, c))
    prev = lambda c: pl.BlockSpec((BLOCK_Q, SWA_KV_WIDTH), lambda b, n: (b * nb + jnp.maximum(n - 1, 0), c))
    ck, cv = S_K // SWA_KV_WIDTH, S_V // SWA_KV_WIDTH
    blk = (2 * _nbytes((BLOCK_Q, SWA_WIDTH), F32) + 4 * _nbytes((BLOCK_Q, SWA_KV_WIDTH), F32) + _nbytes(bias.shape, F32))
    return pl.pallas_call(
        _swa_kernel,
        grid=(batch, nb),
        in_specs=[
            pl.BlockSpec(memory_space=pltpu.SMEM),
            pl.BlockSpec((BLOCK_Q, SWA_WIDTH), lambda b, n: (b * nb + n, 0)),
            prev(ck), cur(ck), prev(cv), cur(cv),
            pl.BlockSpec((1, SWA_KV_WIDTH), lambda b, n: (0, 0)),
            pl.BlockSpec(bias.shape, lambda b, n: (0, 0)),
        ],
        out_specs=pl.BlockSpec((BLOCK_Q, SWA_WIDTH), lambda b, n: (b * nb + n, 0)),
        out_shape=jax.ShapeDtypeStruct((t, SWA_WIDTH), F32),
        compiler_params=_cparams(("parallel", "arbitrary"), blk, 4 * _nbytes((nk, SWA_HEADS * BLOCK_Q), F32)),
        name="swa_attention",
    )(sinks, u, u, u, u, u, gfold, bias)


def _mem_attn_kernel(q_ref, kv_ref, gq_ref, gk_ref, o_ref):
    gq = gq_ref[...] * (MEM_HEAD_DIM ** -0.5)
    gk = gk_ref[...]
    for h in range(MEM_HEADS):
        q = _rms_rows(q_ref[:, h * MEM_HEAD_DIM:(h + 1) * MEM_HEAD_DIM], gq).astype(BF16)
        lo = 2 * h * MEM_HEAD_DIM
        k = _rms_rows(kv_ref[:, lo:lo + MEM_HEAD_DIM], gk).astype(BF16)
        v = kv_ref[:, lo + MEM_HEAD_DIM:lo + 2 * MEM_HEAD_DIM].astype(BF16)
        s = lax.dot_general(q, k, (((1,), (1,)), ((), ())), preferred_element_type=F32)
        e = jnp.exp(s - jnp.max(s, axis=-1, keepdims=True))
        p = (e / jnp.sum(e, axis=-1, keepdims=True)).astype(BF16)
        o_ref[:, h * MEM_HEAD_DIM:(h + 1) * MEM_HEAD_DIM] = jnp.dot(p, v, preferred_element_type=F32).astype(o_ref.dtype)


def _mem_attention(qm, kvm, batch, seq, mem_len, q_norm, k_norm):
    t, w = qm.shape
    tq = _tile(seq, 512)
    nq = seq // tq
    blk = _nbytes((tq, w), F32) + _nbytes((mem_len, 2 * w), F32) + _nbytes((tq, w), BF16)
    return pl.pallas_call(
        _mem_attn_kernel,
        grid=(batch, nq),
        in_specs=[
            pl.BlockSpec((tq, w), lambda b, i: (b * nq + i, 0)),
            pl.BlockSpec((mem_len, 2 * w), lambda b, i: (b, 0)),
            pl.BlockSpec((1, MEM_HEAD_DIM), lambda b, i: (0, 0)),
            pl.BlockSpec((1, MEM_HEAD_DIM), lambda b, i: (0, 0)),
        ],
        out_specs=pl.BlockSpec((tq, w), lambda b, i: (b * nq + i, 0)),
        out_shape=jax.ShapeDtypeStruct((t, w), BF16),
        compiler_params=_cparams(("parallel", "arbitrary"), blk, 8 * _nbytes((tq, mem_len), F32)),
        name="mem_attention",
    )(qm, kvm, q_norm.reshape(1, -1), k_norm.reshape(1, -1))


def _layer(h, mem, rel_bias, p, batch, seq):
    h = _swiglu_half(h, p['ffn_a_norm'], p['ffn_a_gate'], p['ffn_a_up'], p['ffn_a_down'])

    xn = _norm(h, p['mix_norm'])
    w_in_t = p['w_in'].T
    u_mla = _mm_wt32(xn, w_in_t, row0=0, n=U_MLA, tm=1024, tn=512, name="in_proj_mla")
    u_swa = _mm_wt32(xn, w_in_t, row0=IN_SWA, n=w_in_t.shape[0] - IN_SWA, tm=1024, tn=512, name="in_proj_swa")

    q_t, k, v_t = _mla_prep(u_mla, seq, p['mla_q_a_norm'], p['mla_kv_a_norm'], p['mla_w_uq'], p['mla_w_ukv'],
                            p['mla_q_norm'], p['mla_k_norm'])
    o_a = _mla_attention(q_t, k, v_t, batch, seq)
    o_b = _swa_attention(u_swa, batch, seq, p['swa_q_norm'], p['swa_k_norm'], p['swa_sinks'], rel_bias)
    o = _dual_norm(o_a, o_b, p['out_norm_mla'], p['out_norm_swa'])
    h = _mm_w32(o, p['w_out'], h, tm=1024, tn=512, name="out_proj")

    mem_len = mem.shape[0] // batch
    qm = _mm_w32(_norm(h, p['mem_attn_norm']), p['mem_w_q'], tm=1024, tn=512, name="mem_q_proj")
    kvm = _mm_w32(_norm(mem, p['mem_norm']), p['mem_w_kv'], tm=512, tn=512, name="mem_kv_proj")
    om = _mem_attention(qm, kvm, batch, seq, mem_len, p['mem_q_norm'], p['mem_k_norm'])
    h = _mm_w32(om, p['mem_w_o'], h, tm=1024, tn=512, name="mem_o_proj")

    return _swiglu_half(h, p['ffn_b_norm'], p['ffn_b_gate'], p['ffn_b_up'], p['ffn_b_down'])


_PARAM_NAMES = (
    'ffn_a_norm', 'ffn_a_gate', 'ffn_a_up', 'ffn_a_down', 'mix_norm', 'w_in',
    'mla_q_a_norm', 'mla_kv_a_norm', 'mla_w_uq', 'mla_w_ukv', 'mla_q_norm', 'mla_k_norm',
    'swa_q_norm', 'swa_k_norm', 'swa_sinks', 'out_norm_mla', 'out_norm_swa', 'w_out',
    'mem_attn_norm', 'mem_norm', 'mem_w_q', 'mem_w_kv', 'mem_q_norm', 'mem_k_norm', 'mem_w_o',
    'ffn_b_norm', 'ffn_b_gate', 'ffn_b_up', 'ffn_b_down',
)


def kernel(x, mem, rel_bias, ffn_a_norm, ffn_a_gate, ffn_a_up, ffn_a_down, mix_norm, w_in, mla_q_a_norm, mla_kv_a_norm, mla_w_uq, mla_w_ukv, mla_q_norm, mla_k_norm, swa_q_norm, swa_k_norm, swa_sinks, out_norm_mla, out_norm_swa, w_out, mem_attn_norm, mem_norm, mem_w_q, mem_w_kv, mem_q_norm, mem_k_norm, mem_w_o, ffn_b_norm, ffn_b_gate, ffn_b_up, ffn_b_down):
    stacked = (ffn_a_norm, ffn_a_gate, ffn_a_up, ffn_a_down, mix_norm, w_in, mla_q_a_norm, mla_kv_a_norm,
               mla_w_uq, mla_w_ukv, mla_q_norm, mla_k_norm, swa_q_norm, swa_k_norm, swa_sinks, out_norm_mla,
               out_norm_swa, w_out, mem_attn_norm, mem_norm, mem_w_q, mem_w_kv, mem_q_norm, mem_k_norm, mem_w_o,
               ffn_b_norm, ffn_b_gate, ffn_b_up, ffn_b_down)
    batch, seq, d = x.shape
    h = x.reshape(batch * seq, d)
    mem2 = mem.reshape(-1, d)
    for layer in range(ffn_a_norm.shape[0]):
        p = {name: arr[layer] for name, arr in zip(_PARAM_NAMES, stacked)}
        h = _layer(h, mem2, rel_bias, p, batch, seq)
    return h.reshape(batch, seq, d)
```

```python
import functools
import math

import jax
import jax.numpy as jnp
from jax import lax
from jax.experimental import pallas as pl
from jax.experimental.pallas import tpu as pltpu

F32 = jnp.float32
BF16 = jnp.bfloat16

MEM_HEADS = 4
MEM_HEAD_DIM = 128
MLA_HEADS = 16
MLA_Q_RANK = 896
MLA_KV_RANK = 512
MLA_NOPE = 128
MLA_ROPE = 64
MLA_V = 128
ROPE_THETA = 10000.0
SWA_HEADS = 32
SWA_KV_HEADS = 4
SWA_HEAD_DIM = 64
WINDOW = 128
REL_BUCKETS = 32
REL_MAX_DIST = 128
BLOCK_Q = 128
EPS = 1e-6
NEG = -1e30

MLA_QK = MLA_NOPE + MLA_ROPE
SWA_GROUP = SWA_HEADS // SWA_KV_HEADS
SWA_WIDTH = SWA_HEADS * SWA_HEAD_DIM
SWA_KV_WIDTH = SWA_KV_HEADS * SWA_HEAD_DIM
MLA_WIDTH = MLA_HEADS * MLA_V

LANES = 128
MLA_HEAD_PAD = 2 * LANES
ROPE_PAD = LANES

U_CKV = MLA_Q_RANK
U_KR = U_CKV + MLA_KV_RANK
U_MLA = U_KR + ROPE_PAD
IN_SWA = U_KR + MLA_ROPE
S_K = SWA_WIDTH
S_V = S_K + SWA_KV_WIDTH

LOG2E = math.log2(math.e)
KEY_BLOCK = 256

V7X_SCOPED_VMEM_BYTES = 60000 * 1024


def _tile(n, pref, align=8):
    if n <= pref:
        return n
    t = (pref // align) * align
    while t >= align:
        if n % t == 0:
            return t
        t -= align
    return n


def _nbytes(shape, dtype):
    n = 1
    for s in shape:
        n *= s
    return n * jnp.dtype(dtype).itemsize


def _cparams(semantics, pipelined_bytes, temp_bytes=0):
    need = 2 * pipelined_bytes + temp_bytes + (4 << 20)
    return pltpu.CompilerParams(
        dimension_semantics=semantics,
        vmem_limit_bytes=int(min(max(need, 16 << 20), V7X_SCOPED_VMEM_BYTES)),
    )


def _rms_rows(x, g, n=None):
    n = x.shape[-1] if n is None else n
    r = lax.rsqrt(jnp.sum(x * x, axis=-1, keepdims=True) / float(n) + EPS)
    return x * r * g


def _norm_kernel(x_ref, g_ref, o_ref):
    o_ref[...] = _rms_rows(x_ref[...], g_ref[...]).astype(o_ref.dtype)


def _norm(x, g, out_dtype=BF16):
    rows, d = x.shape
    tr = _tile(rows, 512)
    blk = _nbytes((tr, d), F32) + _nbytes((tr, d), out_dtype)
    return pl.pallas_call(
        _norm_kernel,
        grid=(rows // tr,),
        in_specs=[pl.BlockSpec((tr, d), lambda i: (i, 0)), pl.BlockSpec((1, d), lambda i: (0, 0))],
        out_specs=pl.BlockSpec((tr, d), lambda i: (i, 0)),
        out_shape=jax.ShapeDtypeStruct((rows, d), out_dtype),
        compiler_params=_cparams(("parallel",), blk, _nbytes((tr, d), F32)),
        name="rmsnorm",
    )(x, g.reshape(1, d))


def _dual_norm_kernel(a_ref, b_ref, ga_ref, gb_ref, o_ref):
    wa = a_ref.shape[-1]
    o_ref[:, :wa] = _rms_rows(a_ref[...], ga_ref[...]).astype(o_ref.dtype)
    o_ref[:, wa:] = _rms_rows(b_ref[...], gb_ref[...]).astype(o_ref.dtype)


def _dual_norm(a, b, ga, gb):
    rows, wa = a.shape
    wb = b.shape[1]
    tr = _tile(rows, 512)
    blk = _nbytes((tr, wa + wb), F32) + _nbytes((tr, wa + wb), BF16)
    return pl.pallas_call(
        _dual_norm_kernel,
        grid=(rows // tr,),
        in_specs=[
            pl.BlockSpec((tr, wa), lambda i: (i, 0)),
            pl.BlockSpec((tr, wb), lambda i: (i, 0)),
            pl.BlockSpec((1, wa), lambda i: (0, 0)),
            pl.BlockSpec((1, wb), lambda i: (0, 0)),
        ],
        out_specs=pl.BlockSpec((tr, wa + wb), lambda i: (i, 0)),
        out_shape=jax.ShapeDtypeStruct((rows, wa + wb), BF16),
        compiler_params=_cparams(("parallel",), blk, _nbytes((tr, wa + wb), F32)),
        name="out_norms",
    )(a, b, ga.reshape(1, wa), gb.reshape(1, wb))


def _mm_kernel(a_ref, w_ref, o_ref):
    o_ref[...] = jnp.dot(a_ref[...], w_ref[...], preferred_element_type=F32).astype(o_ref.dtype)


def _mm_res_kernel(a_ref, w_ref, r_ref, o_ref):
    o_ref[...] = r_ref[...] + jnp.dot(a_ref[...], w_ref[...], preferred_element_type=F32)


def _mm(a, w, res=None, *, tm, tn, out_dtype=F32, name="matmul"):
    m, k = a.shape
    n = w.shape[1]
    tm, tn = _tile(m, tm), _tile(n, tn, LANES)
    blk = _nbytes((tm, k), a.dtype) + _nbytes((k, tn), w.dtype) + _nbytes((tm, tn), out_dtype)
    in_specs = [pl.BlockSpec((tm, k), lambda i, j: (i, 0)), pl.BlockSpec((k, tn), lambda i, j: (0, j))]
    args = [a, w]
    kern = _mm_kernel
    if res is not None:
        in_specs.append(pl.BlockSpec((tm, tn), lambda i, j: (i, j)))
        args.append(res)
        blk += _nbytes((tm, tn), F32)
        kern = _mm_res_kernel
    return pl.pallas_call(
        kern,
        grid=(m // tm, n // tn),
        in_specs=in_specs,
        out_specs=pl.BlockSpec((tm, tn), lambda i, j: (i, j)),
        out_shape=jax.ShapeDtypeStruct((m, n), out_dtype),
        compiler_params=_cparams(("parallel", "arbitrary"), blk, 2 * _nbytes((tm, tn), F32)),
        name=name,
    )(*args)


def _mm_w32_kernel(a_ref, w_ref, *rest, has_res):
    if has_res:
        r_ref, o_ref, wb_ref = rest
    else:
        o_ref, wb_ref = rest

    @pl.when(pl.program_id(1) == 0)
    def _():
        wb_ref[...] = w_ref[...].astype(BF16)

    acc = jnp.dot(a_ref[...], wb_ref[...], preferred_element_type=F32)
    if has_res:
        acc = r_ref[...] + acc
    o_ref[...] = acc.astype(o_ref.dtype)


def _mm_w32(a, w, res=None, *, tm, tn, n=None, out_dtype=F32, name="matmul"):
    m, k = a.shape
    n = w.shape[1] if n is None else n
    tm, tn = _tile(m, tm), _tile(n, tn, LANES)
    blk = _nbytes((tm, k), a.dtype) + _nbytes((k, tn), F32) + _nbytes((tm, tn), out_dtype)
    in_specs = [pl.BlockSpec((tm, k), lambda j, i: (i, 0)), pl.BlockSpec((k, tn), lambda j, i: (0, j))]
    args = [a, w]
    if res is not None:
        in_specs.append(pl.BlockSpec((tm, tn), lambda j, i: (i, j)))
        args.append(res)
        blk += _nbytes((tm, tn), F32)
    return pl.pallas_call(
        functools.partial(_mm_w32_kernel, has_res=res is not None),
        grid=(n // tn, m // tm),
        in_specs=in_specs,
        out_specs=pl.BlockSpec((tm, tn), lambda j, i: (i, j)),
        out_shape=jax.ShapeDtypeStruct((m, n), out_dtype),
        scratch_shapes=[pltpu.VMEM((k, tn), BF16)],
        compiler_params=_cparams(("parallel", "arbitrary"), blk, _nbytes((k, tn), BF16) + 2 * _nbytes((tm, tn), F32)),
        name=name,
    )(*args)


def _mm_wt32_kernel(a_ref, wt_ref, o_ref, wb_ref):
    @pl.when(pl.program_id(1) == 0)
    def _():
        wb_ref[...] = wt_ref[...].astype(BF16)

    o_ref[...] = lax.dot_general(a_ref[...], wb_ref[...], (((1,), (1,)), ((), ())),
                                 preferred_element_type=F32).astype(o_ref.dtype)


def _mm_wt32(a, wt, *, row0, n, tm, tn, name):
    m, k = a.shape
    tm, tn = _tile(m, tm), _tile(n, tn, LANES)
    blk = _nbytes((tm, k), a.dtype) + _nbytes((tn, k), F32) + _nbytes((tm, tn), F32)
    return pl.pallas_call(
        _mm_wt32_kernel,
        grid=(n // tn, m // tm),
        in_specs=[
            pl.BlockSpec((tm, k), lambda j, i: (i, 0)),
            pl.BlockSpec((pl.Element(tn), pl.Element(k)), lambda j, i: (pl.multiple_of(row0 + j * tn, 8), 0)),
        ],
        out_specs=pl.BlockSpec((tm, tn), lambda j, i: (i, j)),
        out_shape=jax.ShapeDtypeStruct((m, n), F32),
        scratch_shapes=[pltpu.VMEM((tn, k), BF16)],
        compiler_params=_cparams(("parallel", "arbitrary"), blk, _nbytes((tn, k), BF16) + 2 * _nbytes((tm, tn), F32)),
        name=name,
    )(a, wt)


def _gated_kernel(x_ref, wg_ref, wu_ref, wd_ref, o_ref, wdb_ref):
    @pl.when(pl.program_id(0) == 0)
    def _():
        wdb_ref[...] = wd_ref[...].astype(BF16)

    x = x_ref[...]
    g = jnp.dot(x, wg_ref[...].astype(BF16), preferred_element_type=F32)
    u = jnp.dot(x, wu_ref[...].astype(BF16), preferred_element_type=F32)
    o_ref[...] = (0.5 * (g / (1.0 + jnp.exp(-g))) * u).astype(o_ref.dtype)


def _gated(x, wg, wu, wd, *, tm, tn):
    m, k = x.shape
    n = wg.shape[1]
    d_out = wd.shape[1]
    tm, tn = _tile(m, tm), _tile(n, tn, LANES)
    nj = n // tn
    blk = (2 * _nbytes((k, tn), F32) + _nbytes((tm, tn), BF16)
           + _nbytes((tn, d_out), F32) + _nbytes((tn, d_out), BF16))
    wd_map = lambda i, j: (jnp.where(i == 0, j, nj - 1), 0)
    return pl.pallas_call(
        _gated_kernel,
        grid=(m // tm, nj),
        in_specs=[
            pl.BlockSpec((tm, k), lambda i, j: (i, 0), pipeline_mode=pl.Buffered(1)),
            pl.BlockSpec((k, tn), lambda i, j: (0, j)),
            pl.BlockSpec((k, tn), lambda i, j: (0, j)),
            pl.BlockSpec((tn, d_out), wd_map),
        ],
        out_specs=[pl.BlockSpec((tm, tn), lambda i, j: (i, j)), pl.BlockSpec((tn, d_out), wd_map)],
        out_shape=[jax.ShapeDtypeStruct((m, n), BF16), jax.ShapeDtypeStruct((n, d_out), BF16)],
        compiler_params=_cparams(("arbitrary", "arbitrary"), blk,
                                 _nbytes((tm, k), BF16) + 2 * _nbytes((k, tn), BF16) + 3 * _nbytes((tm, tn), F32)),
        name="ffn_gate_up",
    )(x, wg, wu, wd)


def _swiglu_half(h, xn, wg, wu, wd):
    act, wd_bf = _gated(xn, wg, wu, wd, tm=2048, tn=256)
    return _mm(act, wd_bf, h, tm=512, tn=512, name="ffn_down")


def _rope_chunk(y, cos, s1, s2):
    return y * cos + pltpu.roll(y, ROPE_PAD - MLA_ROPE // 2, 1) * s1 + pltpu.roll(y, MLA_ROPE // 2, 1) * s2


def _mla_q_kernel(u_ref, ga_ref, wt_ref, gt_ref, cos_ref, sin_ref, o_ref):
    cn = _rms_rows(u_ref[:, :MLA_Q_RANK], ga_ref[...]).astype(BF16)
    qt = lax.dot_general(wt_ref[...], cn, (((1,), (1,)), ((), ())), preferred_element_type=F32)
    cos, sin, gain = cos_ref[...], sin_ref[...], gt_ref[...]
    half = MLA_ROPE // 2
    zeros = jnp.zeros((MLA_HEAD_PAD - MLA_QK, qt.shape[1]), o_ref.dtype)
    for h in range(MLA_HEADS):
        lo = h * MLA_HEAD_PAD
        c = qt[lo:lo + MLA_HEAD_PAD, :]
        y = c * lax.rsqrt(jnp.sum(c * c, axis=0, keepdims=True) / float(MLA_QK) + EPS) * gain
        x1, x2 = y[MLA_NOPE:MLA_NOPE + half, :], y[MLA_NOPE + half:MLA_QK, :]
        o_ref[lo:lo + MLA_NOPE, :] = y[:MLA_NOPE, :].astype(o_ref.dtype)
        o_ref[lo + MLA_NOPE:lo + MLA_NOPE + half, :] = (x1 * cos - x2 * sin).astype(o_ref.dtype)
        o_ref[lo + MLA_NOPE + half:lo + MLA_QK, :] = (x2 * cos + x1 * sin).astype(o_ref.dtype)
        o_ref[lo + MLA_QK:lo + MLA_HEAD_PAD, :] = zeros


def _mla_kv_kernel(u_ref, ga_ref, wk_ref, wvt_ref, gn_ref, gr_ref, cos_ref, s1_ref, s2_ref, k_ref, vt_ref):
    cn = _rms_rows(u_ref[:, U_CKV:U_KR], ga_ref[...]).astype(BF16)
    kv = jnp.dot(cn, wk_ref[...], preferred_element_type=F32)
    vt = lax.dot_general(wvt_ref[...], cn, (((1,), (1,)), ((), ())), preferred_element_type=F32)
    vt_ref[...] = vt.astype(vt_ref.dtype)
    kr = u_ref[:, U_KR:U_MLA]
    kr = jnp.where(lax.broadcasted_iota(jnp.int32, kr.shape, 1) < MLA_ROPE, kr, 0.0)
    ss_r = jnp.sum(kr * kr, axis=-1, keepdims=True)
    cos, s1, s2 = cos_ref[...], s1_ref[...], s2_ref[...]
    gn, gr = gn_ref[...], gr_ref[...]
    for h in range(MLA_HEADS):
        kn = kv[:, h * MLA_NOPE:(h + 1) * MLA_NOPE]
        r = lax.rsqrt((jnp.sum(kn * kn, axis=-1, keepdims=True) + ss_r) / float(MLA_QK) + EPS)
        lo = h * MLA_HEAD_PAD
        k_ref[:, lo:lo + MLA_NOPE] = (kn * r * gn).astype(k_ref.dtype)
        k_ref[:, lo + MLA_NOPE:lo + MLA_HEAD_PAD] = _rope_chunk(kr * r * gr, cos, s1, s2).astype(k_ref.dtype)


def _rope_tables(seq):
    half = MLA_ROPE // 2
    inv = ROPE_THETA ** (-jnp.arange(half, dtype=F32) / half)
    ang = jnp.arange(seq).astype(F32)[:, None] * inv[None, :]
    cos, sin = jnp.cos(ang), jnp.sin(ang)
    z = jnp.zeros_like(cos)
    pad = jnp.zeros((seq, ROPE_PAD - MLA_ROPE), F32)
    return (jnp.concatenate([cos, cos, pad], 1), jnp.concatenate([-sin, z, pad], 1),
            jnp.concatenate([z, sin, pad], 1), cos.T, sin.T)


def _mla_prep(u, seq, q_a_norm, kv_a_norm, w_uq, w_ukv, q_norm, k_norm):
    t = u.shape[0]
    tm = _tile(seq, KEY_BLOCK)
    spb = seq // tm
    cos, s1, s2, cos_t, sin_t = _rope_tables(seq)
    tab_spec = pl.BlockSpec((tm, ROPE_PAD), lambda i: (i % spb, 0))
    tab_t_spec = pl.BlockSpec((MLA_ROPE // 2, tm), lambda i: (0, i % spb))
    qw = MLA_HEADS * MLA_HEAD_PAD
    vw = MLA_HEADS * MLA_V

    wq_t = jnp.pad(w_uq.reshape(MLA_Q_RANK, MLA_HEADS, MLA_QK), ((0, 0), (0, 0), (0, MLA_HEAD_PAD - MLA_QK)))
    wq_t = wq_t.reshape(MLA_Q_RANK, qw).T.astype(BF16)
    gain_t = jnp.concatenate([q_norm * (MLA_QK ** -0.5 * LOG2E), jnp.zeros((MLA_HEAD_PAD - MLA_QK,), F32)])
    gain_t = jnp.broadcast_to(gain_t[:, None], (MLA_HEAD_PAD, tm))
    blk = (_nbytes((tm, U_MLA), F32) + _nbytes(wq_t.shape, BF16) + _nbytes((qw, tm), BF16)
           + _nbytes((MLA_HEAD_PAD, tm), F32) + 2 * _nbytes((MLA_ROPE // 2, tm), F32))
    q_t = pl.pallas_call(
        _mla_q_kernel,
        grid=(t // tm,),
        in_specs=[
            pl.BlockSpec((tm, U_MLA), lambda i: (i, 0)),
            pl.BlockSpec((1, MLA_Q_RANK), lambda i: (0, 0)),
            pl.BlockSpec(wq_t.shape, lambda i: (0, 0)),
            pl.BlockSpec((MLA_HEAD_PAD, tm), lambda i: (0, 0)),
            tab_t_spec, tab_t_spec,
        ],
        out_specs=pl.BlockSpec((qw, tm), lambda i: (0, i)),
        out_shape=jax.ShapeDtypeStruct((qw, t), BF16),
        compiler_params=_cparams(("parallel",), blk, 3 * _nbytes((qw, tm), F32)),
        name="mla_q_prep",
    )(u, q_a_norm.reshape(1, -1), wq_t, gain_t, cos_t, sin_t)

    wkv = w_ukv.reshape(MLA_KV_RANK, MLA_HEADS, MLA_NOPE + MLA_V)
    wk = wkv[:, :, :MLA_NOPE].reshape(MLA_KV_RANK, -1).astype(BF16)
    wv_t = wkv[:, :, MLA_NOPE:].reshape(MLA_KV_RANK, -1).T.astype(BF16)
    blk = (_nbytes((tm, U_MLA), F32) + _nbytes(wk.shape, BF16) + _nbytes(wv_t.shape, BF16) + _nbytes((tm, qw), BF16)
           + _nbytes((vw, tm), BF16) + 3 * _nbytes((tm, ROPE_PAD), F32))
    k, v_t = pl.pallas_call(
        _mla_kv_kernel,
        grid=(t // tm,),
        in_specs=[
            pl.BlockSpec((tm, U_MLA), lambda i: (i, 0)),
            pl.BlockSpec((1, MLA_KV_RANK), lambda i: (0, 0)),
            pl.BlockSpec(wk.shape, lambda i: (0, 0)),
            pl.BlockSpec(wv_t.shape, lambda i: (0, 0)),
            pl.BlockSpec((1, MLA_NOPE), lambda i: (0, 0)),
            pl.BlockSpec((1, ROPE_PAD), lambda i: (0, 0)),
            tab_spec, tab_spec, tab_spec,
        ],
        out_specs=[pl.BlockSpec((tm, qw), lambda i: (i, 0)), pl.BlockSpec((None, vw, tm), lambda i: (i, 0, 0))],
        out_shape=[jax.ShapeDtypeStruct((t, qw), BF16), jax.ShapeDtypeStruct((t // tm, vw, tm), BF16)],
        compiler_params=_cparams(("parallel",), blk, 3 * _nbytes((tm, qw), F32)),
        name="mla_kv_prep",
    )(u, kv_a_norm.reshape(1, -1), wk, wv_t, k_norm[:MLA_NOPE].reshape(1, -1),
      jnp.concatenate([k_norm[MLA_NOPE:], jnp.zeros((ROPE_PAD - MLA_ROPE,), F32)]).reshape(1, -1), cos, s1, s2)
    return q_t, k, v_t


def _mla_attn_kernel(qt_ref, k_ref, vt_ref, o_ref, sa_ref, sb_ref, *, tq, kb):
    qi = pl.program_id(2)
    per = tq // kb
    nh = qt_ref.shape[0] // MLA_HEAD_PAD
    qts = [qt_ref[hh * MLA_HEAD_PAD:(hh + 1) * MLA_HEAD_PAD, :] for hh in range(nh)]

    def scores(blk, s_ref):
        for hh in range(nh):
            for d in range(per):
                start = pl.multiple_of((blk * per + d) * kb, kb)
                kblk = k_ref[pl.ds(start, kb), hh * MLA_HEAD_PAD:(hh + 1) * MLA_HEAD_PAD]
                s_ref[hh, d * kb:(d + 1) * kb, :] = jnp.dot(kblk, qts[hh], preferred_element_type=F32)

    def softmax_pv(blk, s_ref, carries, diagonal):
        out = []
        for hh in range(nh):
            m, l, acc = carries[hh]
            ss = []
            for d in range(per):
                s = s_ref[hh, d * kb:(d + 1) * kb, :]
                if diagonal:
                    key = d * kb + lax.broadcasted_iota(jnp.int32, s.shape, 0)
                    qry = lax.broadcasted_iota(jnp.int32, s.shape, 1)
                    s = jnp.where(key <= qry, s, NEG)
                ss.append(s)
            m_new = m
            for s in ss:
                m_new = jnp.maximum(m_new, jnp.max(s, axis=0, keepdims=True))
            alpha = jnp.exp2(m - m_new)
            l = alpha * l
            acc = alpha * acc
            for d, s in enumerate(ss):
                p = jnp.exp2(s - m_new)
                l = l + jnp.sum(p, axis=0, keepdims=True)
                vt = vt_ref[blk * per + d, hh * MLA_V:(hh + 1) * MLA_V, :]
                acc = acc + jnp.dot(vt, p.astype(BF16), preferred_element_type=F32)
            out.append((m_new, l, acc))
        return tuple(out)

    def pair(t, carries):
        scores(2 * t + 1, sb_ref)
        carries = softmax_pv(2 * t, sa_ref, carries, False)
        scores(2 * t + 2, sa_ref)
        return softmax_pv(2 * t + 1, sb_ref, carries, False)

    def tail_even(carries):
        return softmax_pv(qi, sa_ref, carries, True)

    def tail_odd(carries):
        scores(qi, sb_ref)
        carries = softmax_pv(qi - 1, sa_ref, carries, False)
        return softmax_pv(qi, sb_ref, carries, True)

    init = tuple((jnp.full((1, tq), NEG, F32), jnp.zeros((1, tq), F32), jnp.zeros((MLA_V, tq), F32))
                 for _ in range(nh))
    scores(0, sa_ref)
    carries = lax.fori_loop(0, qi >> 1, pair, init)
    carries = lax.cond((qi & 1) == 0, tail_even, tail_odd, carries)
    for hh in range(nh):
        _, l, acc = carries[hh]
        o_ref[:, hh * MLA_V:(hh + 1) * MLA_V] = (acc * (1.0 / l)).T


def _mla_attention(q_t, k, v_t, batch, seq):
    t = k.shape[0]
    kb = v_t.shape[2]
    tq = _tile(seq, 512)
    nq = seq // tq
    nkb = seq // kb
    nh = 2
    blk = nh * (_nbytes((MLA_HEAD_PAD, tq), BF16) + _nbytes((seq, MLA_HEAD_PAD), BF16) + _nbytes((seq, MLA_V), BF16)
                + _nbytes((tq, MLA_V), F32))
    return pl.pallas_call(
        functools.partial(_mla_attn_kernel, tq=tq, kb=kb),
        grid=(batch, MLA_HEADS // nh, nq),
        in_specs=[
            pl.BlockSpec((nh * MLA_HEAD_PAD, tq), lambda b, h, i: (h, b * nq + i)),
            pl.BlockSpec((seq, nh * MLA_HEAD_PAD), lambda b, h, i: (b, h)),
            pl.BlockSpec((nkb, nh * MLA_V, kb), lambda b, h, i: (b, h, 0)),
        ],
        out_specs=pl.BlockSpec((tq, nh * MLA_V), lambda b, h, i: (b * nq + i, h)),
        out_shape=jax.ShapeDtypeStruct((t, MLA_WIDTH), F32),
        scratch_shapes=[pltpu.VMEM((nh, tq, tq), F32), pltpu.VMEM((nh, tq, tq), F32)],
        compiler_params=_cparams(("parallel", "parallel", "arbitrary"), blk, 10 * nh * _nbytes((tq, tq), F32)),
        name="mla_attention",
    )(q_t, k, v_t)


def _t5_bucket(dist):
    max_exact = REL_BUCKETS // 2
    d = jnp.maximum(dist, 1).astype(F32)
    large = max_exact + (jnp.log(d / max_exact) / math.log(REL_MAX_DIST / max_exact)
                         * (REL_BUCKETS - max_exact)).astype(jnp.int32)
    large = jnp.minimum(large, REL_BUCKETS - 1)
    return jnp.where(dist < max_exact, dist, large)


def _swa_kernel(sink_ref, q_ref, kp_ref, kc_ref, vp_ref, vc_ref, gfold_ref, bias_ref, o_ref):
    n = pl.program_id(1)
    nk, hd, kvw = 2 * BLOCK_Q, SWA_HEAD_DIM, SWA_KV_WIDTH
    key = lax.broadcasted_iota(jnp.int32, (nk, BLOCK_Q), 0)
    qry = lax.broadcasted_iota(jnp.int32, (nk, BLOCK_Q), 1)
    dist = qry + BLOCK_Q - key
    valid = (dist >= 0) & (dist < WINDOW) & ((key >= BLOCK_Q) | (n > 0))

    k2 = jnp.concatenate([kp_ref[...], kc_ref[...]], axis=0)
    sq = k2 * k2
    sq_hi = sq.astype(BF16)
    sq_lo = (sq - sq_hi.astype(F32)).astype(BF16)
    shift = hd.bit_length() - 1
    ri = lax.broadcasted_iota(jnp.int32, (kvw, kvw), 0) >> shift
    ci = lax.broadcasted_iota(jnp.int32, (kvw, kvw), 1) >> shift
    ind = jnp.where(ri == ci, 1.0, 0.0).astype(BF16)
    ss = jnp.dot(sq_hi, ind, preferred_element_type=F32) + jnp.dot(sq_lo, ind, preferred_element_type=F32)
    k2n = (k2 * lax.rsqrt(ss / float(hd) + EPS) * gfold_ref[...]).astype(BF16)
    v2t = jnp.concatenate([vp_ref[...], vc_ref[...]], axis=0).T.astype(BF16)

    qt = q_ref[...].T
    gw = SWA_GROUP * BLOCK_Q
    rows = []
    for kh in range(SWA_KV_HEADS):
        pieces = []
        if kh > 0:
            pieces.append(jnp.zeros((hd, kh * gw), BF16))
        for g in range(SWA_GROUP):
            j = kh * SWA_GROUP + g
            c = qt[j * hd:(j + 1) * hd, :]
            pieces.append((c * lax.rsqrt(jnp.sum(c * c, axis=0, keepdims=True) / float(hd) + EPS)).astype(BF16))
        if kh < SWA_KV_HEADS - 1:
            pieces.append(jnp.zeros((hd, (SWA_KV_HEADS - 1 - kh) * gw), BF16))
        rows.append(jnp.concatenate(pieces, axis=1))
    q_bd = jnp.concatenate(rows, axis=0)
    s_all = jnp.dot(k2n, q_bd, preferred_element_type=F32)

    es, invs = [], []
    for j in range(SWA_HEADS):
        lo = j * BLOCK_Q
        s = jnp.where(valid, s_all[:, lo:lo + BLOCK_Q] + bias_ref[:, lo:lo + BLOCK_Q], NEG)
        sink = sink_ref[j]
        m = jnp.maximum(jnp.max(s, axis=0, keepdims=True), sink)
        e = jnp.exp2(s - m)
        invs.append(1.0 / (jnp.sum(e, axis=0, keepdims=True) + jnp.exp2(sink - m)))
        es.append(e.astype(BF16))
    e_all = jnp.concatenate(es, axis=1)
    o_all = jnp.dot(v2t, e_all, preferred_element_type=F32)
    outs = []
    for j in range(SWA_HEADS):
        kh = j // SWA_GROUP
        outs.append(o_all[kh * hd:(kh + 1) * hd, j * BLOCK_Q:(j + 1) * BLOCK_Q] * invs[j])
    o_ref[...] = jnp.concatenate(outs, axis=0).T


def _swa_attention(u, batch, seq, q_norm, k_norm, sinks, rel_bias):
    t = u.shape[0]
    nb = seq // BLOCK_Q
    nk = 2 * BLOCK_Q
    by_dist = rel_bias.astype(F32)[_t5_bucket(jnp.arange(WINDOW))].T
    row = jnp.concatenate([jnp.zeros((SWA_HEADS, 1), F32), by_dist], 1)
    bias = jnp.broadcast_to(row[:, None, :], (SWA_HEADS, nk, BLOCK_Q + 1)).reshape(SWA_HEADS, -1)
    bias = bias[:, :nk * BLOCK_Q].reshape(SWA_HEADS, nk, BLOCK_Q)
    bias = bias.transpose(1, 0, 2).reshape(nk, SWA_HEADS * BLOCK_Q) * LOG2E
    sinks = sinks * LOG2E
    gfold = jnp.tile(k_norm * q_norm * (SWA_HEAD_DIM ** -0.5 * LOG2E), SWA_KV_HEADS).reshape(1, SWA_KV_WIDTH)
    cur = lambda c: pl.BlockSpec((BLOCK_Q, SWA_KV_WIDTH), lambda b, n: (b * nb + n, c))
    prev = lambda c: pl.BlockSpec((BLOCK_Q, SWA_KV_WIDTH), lambda b, n: (b * nb + jnp.maximum(n - 1, 0), c))
    ck, cv = S_K // SWA_KV_WIDTH, S_V // SWA_KV_WIDTH
    blk = (2 * _nbytes((BLOCK_Q, SWA_WIDTH), F32) + 4 * _nbytes((BLOCK_Q, SWA_KV_WIDTH), F32) + _nbytes(bias.shape, F32))
    return pl.pallas_call(
        _swa_kernel,
        grid=(batch, nb),
        in_specs=[
            pl.BlockSpec(memory_space=pltpu.SMEM),
            pl.BlockSpec((BLOCK_Q, SWA_WIDTH), lambda b, n: (b * nb + n, 0)),
            prev(ck), cur(ck), prev(cv), cur(cv),
            pl.BlockSpec((1, SWA_KV_WIDTH), lambda b, n: (0, 0)),
            pl.BlockSpec(bias.shape, lambda b, n: (0, 0)),
        ],
        out_specs=pl.BlockSpec((BLOCK_Q, SWA_WIDTH), lambda b, n: (b * nb + n, 0)),
        out_shape=jax.ShapeDtypeStruct((t, SWA_WIDTH), F32),
        compiler_params=_cparams(("parallel", "arbitrary"), blk, 4 * _nbytes((nk, SWA_HEADS * BLOCK_Q), F32)),
        name="swa_attention",
    )(sinks, u, u, u, u, u, gfold, bias)


def _mem_block_kernel(h_ref, kv_ref, gn_ref, wq_ref, wo_ref, gq_ref, gk_ref, gnext_ref, h_out_ref, xn_out_ref):
    h = h_ref[...]
    xn = _rms_rows(h, gn_ref[...]).astype(BF16)
    qm = jnp.dot(xn, wq_ref[...], preferred_element_type=F32)
    gq = gq_ref[...] * (MEM_HEAD_DIM ** -0.5)
    gk = gk_ref[...]
    outs = []
    for hd in range(MEM_HEADS):
        q = _rms_rows(qm[:, hd * MEM_HEAD_DIM:(hd + 1) * MEM_HEAD_DIM], gq).astype(BF16)
        lo = 2 * hd * MEM_HEAD_DIM
        k = _rms_rows(kv_ref[:, lo:lo + MEM_HEAD_DIM], gk).astype(BF16)
        v = kv_ref[:, lo + MEM_HEAD_DIM:lo + 2 * MEM_HEAD_DIM].astype(BF16)
        s = lax.dot_general(q, k, (((1,), (1,)), ((), ())), preferred_element_type=F32)
        e = jnp.exp(s - jnp.max(s, axis=-1, keepdims=True))
        p = (e / jnp.sum(e, axis=-1, keepdims=True)).astype(BF16)
        outs.append(jnp.dot(p, v, preferred_element_type=F32).astype(BF16))
    o = jnp.concatenate(outs, axis=-1)
    h_new = h + jnp.dot(o, wo_ref[...], preferred_element_type=F32)
    h_out_ref[...] = h_new
    xn_out_ref[...] = _rms_rows(h_new, gnext_ref[...]).astype(xn_out_ref.dtype)


def _mem_block(h, kvm, batch, seq, mem_len, norm_g, w_q, w_o, q_norm, k_norm, next_norm_g):
    t, d = h.shape
    w = w_q.shape[1]
    tm = _tile(seq, 256)
    nq = seq // tm
    once = pl.Buffered(1)
    blk = 2 * _nbytes((tm, d), F32) + _nbytes((tm, d), BF16) + _nbytes((mem_len, 2 * w), F32)
    temp = 2 * _nbytes((d, w), BF16) + 3 * _nbytes((tm, d), F32)
    return pl.pallas_call(
        _mem_block_kernel,
        grid=(batch, nq),
        in_specs=[
            pl.BlockSpec((tm, d), lambda b, i: (b * nq + i, 0)),
            pl.BlockSpec((mem_len, 2 * w), lambda b, i: (b, 0)),
            pl.BlockSpec((1, d), lambda b, i: (0, 0)),
            pl.BlockSpec((d, w), lambda b, i: (0, 0), pipeline_mode=once),
            pl.BlockSpec((w, d), lambda b, i: (0, 0), pipeline_mode=once),
            pl.BlockSpec((1, MEM_HEAD_DIM), lambda b, i: (0, 0)),
            pl.BlockSpec((1, MEM_HEAD_DIM), lambda b, i: (0, 0)),
            pl.BlockSpec((1, d), lambda b, i: (0, 0)),
        ],
        out_specs=[pl.BlockSpec((tm, d), lambda b, i: (b * nq + i, 0)), pl.BlockSpec((tm, d), lambda b, i: (b * nq + i, 0))],
        out_shape=[jax.ShapeDtypeStruct((t, d), F32), jax.ShapeDtypeStruct((t, d), BF16)],
        compiler_params=_cparams(("parallel", "arbitrary"), blk, temp),
        name="mem_block",
    )(h, kvm, norm_g.reshape(1, d), w_q.astype(BF16), w_o.astype(BF16), q_norm.reshape(1, -1), k_norm.reshape(1, -1),
      next_norm_g.reshape(1, d))


def _layer(h, mem, rel_bias, p, batch, seq):
    h = _swiglu_half(h, _norm(h, p['ffn_a_norm']), p['ffn_a_gate'], p['ffn_a_up'], p['ffn_a_down'])

    xn = _norm(h, p['mix_norm'])
    w_in_t = p['w_in'].T
    u_mla = _mm_wt32(xn, w_in_t, row0=0, n=U_MLA, tm=1024, tn=512, name="in_proj_mla")
    u_swa = _mm_wt32(xn, w_in_t, row0=IN_SWA, n=w_in_t.shape[0] - IN_SWA, tm=1024, tn=512, name="in_proj_swa")

    q_t, k, v_t = _mla_prep(u_mla, seq, p['mla_q_a_norm'], p['mla_kv_a_norm'], p['mla_w_uq'], p['mla_w_ukv'],
                            p['mla_q_norm'], p['mla_k_norm'])
    o_a = _mla_attention(q_t, k, v_t, batch, seq)
    o_b = _swa_attention(u_swa, batch, seq, p['swa_q_norm'], p['swa_k_norm'], p['swa_sinks'], rel_bias)
    o = _dual_norm(o_a, o_b, p['out_norm_mla'], p['out_norm_swa'])
    h = _mm_w32(o, p['w_out'], h, tm=1024, tn=512, name="out_proj")

    mem_len = mem.shape[0] // batch
    kvm = _mm_w32(_norm(mem, p['mem_norm']), p['mem_w_kv'], tm=512, tn=512, name="mem_kv_proj")
    h, xn = _mem_block(h, kvm, batch, seq, mem_len, p['mem_attn_norm'], p['mem_w_q'], p['mem_w_o'],
                       p['mem_q_norm'], p['mem_k_norm'], p['ffn_b_norm'])

    return _swiglu_half(h, xn, p['ffn_b_gate'], p['ffn_b_up'], p['ffn_b_down'])


_PARAM_NAMES = (
    'ffn_a_norm', 'ffn_a_gate', 'ffn_a_up', 'ffn_a_down', 'mix_norm', 'w_in',
    'mla_q_a_norm', 'mla_kv_a_norm', 'mla_w_uq', 'mla_w_ukv', 'mla_q_norm', 'mla_k_norm',
    'swa_q_norm', 'swa_k_norm', 'swa_sinks', 'out_norm_mla', 'out_norm_swa', 'w_out',
    'mem_attn_norm', 'mem_norm', 'mem_w_q', 'mem_w_kv', 'mem_q_norm', 'mem_k_norm', 'mem_w_o',
    'ffn_b_norm', 'ffn_b_gate', 'ffn_b_up', 'ffn_b_down',
)


def kernel(x, mem, rel_bias, ffn_a_norm, ffn_a_gate, ffn_a_up, ffn_a_down, mix_norm, w_in, mla_q_a_norm, mla_kv_a_norm, mla_w_uq, mla_w_ukv, mla_q_norm, mla_k_norm, swa_q_norm, swa_k_norm, swa_sinks, out_norm_mla, out_norm_swa, w_out, mem_attn_norm, mem_norm, mem_w_q, mem_w_kv, mem_q_norm, mem_k_norm, mem_w_o, ffn_b_norm, ffn_b_gate, ffn_b_up, ffn_b_down):
    stacked = (ffn_a_norm, ffn_a_gate, ffn_a_up, ffn_a_down, mix_norm, w_in, mla_q_a_norm, mla_kv_a_norm,
               mla_w_uq, mla_w_ukv, mla_q_norm, mla_k_norm, swa_q_norm, swa_k_norm, swa_sinks, out_norm_mla,
               out_norm_swa, w_out, mem_attn_norm, mem_norm, mem_w_q, mem_w_kv, mem_q_norm, mem_k_norm, mem_w_o,
               ffn_b_norm, ffn_b_gate, ffn_b_up, ffn_b_down)
    batch, seq, d = x.shape
    h = x.reshape(batch * seq, d)
    mem2 = mem.reshape(-1, d)
    for layer in range(ffn_a_norm.shape[0]):
        p = {name: arr[layer] for name, arr in zip(_PARAM_NAMES, stacked)}
        h = _layer(h, mem2, rel_bias, p, batch, seq)
    return h.reshape(batch, seq, d)
```

```python
import functools
import math

import jax
import jax.numpy as jnp
from jax import lax
from jax.experimental import pallas as pl
from jax.experimental.pallas import tpu as pltpu

F32 = jnp.float32
BF16 = jnp.bfloat16

MEM_HEADS = 4
MEM_HEAD_DIM = 128
MLA_HEADS = 16
MLA_Q_RANK = 896
MLA_KV_RANK = 512
MLA_NOPE = 128
MLA_ROPE = 64
MLA_V = 128
ROPE_THETA = 10000.0
SWA_HEADS = 32
SWA_KV_HEADS = 4
SWA_HEAD_DIM = 64
WINDOW = 128
REL_BUCKETS = 32
REL_MAX_DIST = 128
BLOCK_Q = 128
EPS = 1e-6
NEG = -1e30

MLA_QK = MLA_NOPE + MLA_ROPE
SWA_GROUP = SWA_HEADS // SWA_KV_HEADS
SWA_WIDTH = SWA_HEADS * SWA_HEAD_DIM
SWA_KV_WIDTH = SWA_KV_HEADS * SWA_HEAD_DIM
MLA_WIDTH = MLA_HEADS * MLA_V

LANES = 128
MLA_HEAD_PAD = 2 * LANES
ROPE_PAD = LANES

U_CKV = MLA_Q_RANK
U_KR = U_CKV + MLA_KV_RANK
U_MLA = U_KR + ROPE_PAD
IN_SWA = U_KR + MLA_ROPE
S_K = SWA_WIDTH
S_V = S_K + SWA_KV_WIDTH

LOG2E = math.log2(math.e)
KEY_BLOCK = 256

V7X_SCOPED_VMEM_BYTES = 60000 * 1024


def _tile(n, pref, align=8):
    if n <= pref:
        return n
    t = (pref // align) * align
    while t >= align:
        if n % t == 0:
            return t
        t -= align
    return n


def _nbytes(shape, dtype):
    n = 1
    for s in shape:
        n *= s
    return n * jnp.dtype(dtype).itemsize


def _cparams(semantics, pipelined_bytes, temp_bytes=0):
    need = 2 * pipelined_bytes + temp_bytes + (4 << 20)
    return pltpu.CompilerParams(
        dimension_semantics=semantics,
        vmem_limit_bytes=int(min(max(need, 16 << 20), V7X_SCOPED_VMEM_BYTES)),
    )


def _rms_rows(x, g, n=None):
    n = x.shape[-1] if n is None else n
    r = lax.rsqrt(jnp.sum(x * x, axis=-1, keepdims=True) / float(n) + EPS)
    return x * r * g


def _norm_kernel(x_ref, g_ref, o_ref):
    o_ref[...] = _rms_rows(x_ref[...], g_ref[...]).astype(o_ref.dtype)


def _norm(x, g, out_dtype=BF16):
    rows, d = x.shape
    tr = _tile(rows, 512)
    blk = _nbytes((tr, d), F32) + _nbytes((tr, d), out_dtype)
    return pl.pallas_call(
        _norm_kernel,
        grid=(rows // tr,),
        in_specs=[pl.BlockSpec((tr, d), lambda i: (i, 0)), pl.BlockSpec((1, d), lambda i: (0, 0))],
        out_specs=pl.BlockSpec((tr, d), lambda i: (i, 0)),
        out_shape=jax.ShapeDtypeStruct((rows, d), out_dtype),
        compiler_params=_cparams(("parallel",), blk, _nbytes((tr, d), F32)),
        name="rmsnorm",
    )(x, g.reshape(1, d))


def _dual_norm_kernel(a_ref, b_ref, ga_ref, gb_ref, o_ref):
    wa = a_ref.shape[-1]
    o_ref[:, :wa] = _rms_rows(a_ref[...], ga_ref[...]).astype(o_ref.dtype)
    o_ref[:, wa:] = _rms_rows(b_ref[...], gb_ref[...]).astype(o_ref.dtype)


def _dual_norm(a, b, ga, gb):
    rows, wa = a.shape
    wb = b.shape[1]
    tr = _tile(rows, 512)
    blk = _nbytes((tr, wa + wb), F32) + _nbytes((tr, wa + wb), BF16)
    return pl.pallas_call(
        _dual_norm_kernel,
        grid=(rows // tr,),
        in_specs=[
            pl.BlockSpec((tr, wa), lambda i: (i, 0)),
            pl.BlockSpec((tr, wb), lambda i: (i, 0)),
            pl.BlockSpec((1, wa), lambda i: (0, 0)),
            pl.BlockSpec((1, wb), lambda i: (0, 0)),
        ],
        out_specs=pl.BlockSpec((tr, wa + wb), lambda i: (i, 0)),
        out_shape=jax.ShapeDtypeStruct((rows, wa + wb), BF16),
        compiler_params=_cparams(("parallel",), blk, _nbytes((tr, wa + wb), F32)),
        name="out_norms",
    )(a, b, ga.reshape(1, wa), gb.reshape(1, wb))


def _mm_kernel(a_ref, w_ref, o_ref):
    o_ref[...] = jnp.dot(a_ref[...], w_ref[...], preferred_element_type=F32).astype(o_ref.dtype)


def _mm_res_kernel(a_ref, w_ref, r_ref, o_ref):
    o_ref[...] = r_ref[...] + jnp.dot(a_ref[...], w_ref[...], preferred_element_type=F32)


def _mm(a, w, res=None, *, tm, tn, out_dtype=F32, name="matmul"):
    m, k = a.shape
    n = w.shape[1]
    tm, tn = _tile(m, tm), _tile(n, tn, LANES)
    blk = _nbytes((tm, k), a.dtype) + _nbytes((k, tn), w.dtype) + _nbytes((tm, tn), out_dtype)
    in_specs = [pl.BlockSpec((tm, k), lambda i, j: (i, 0)), pl.BlockSpec((k, tn), lambda i, j: (0, j))]
    args = [a, w]
    kern = _mm_kernel
    if res is not None:
        in_specs.append(pl.BlockSpec((tm, tn), lambda i, j: (i, j)))
        args.append(res)
        blk += _nbytes((tm, tn), F32)
        kern = _mm_res_kernel
    return pl.pallas_call(
        kern,
        grid=(m // tm, n // tn),
        in_specs=in_specs,
        out_specs=pl.BlockSpec((tm, tn), lambda i, j: (i, j)),
        out_shape=jax.ShapeDtypeStruct((m, n), out_dtype),
        compiler_params=_cparams(("parallel", "arbitrary"), blk, 2 * _nbytes((tm, tn), F32)),
        name=name,
    )(*args)


def _mm_w32_kernel(a_ref, w_ref, *rest, has_res):
    if has_res:
        r_ref, o_ref, wb_ref = rest
    else:
        o_ref, wb_ref = rest

    @pl.when(pl.program_id(1) == 0)
    def _():
        wb_ref[...] = w_ref[...].astype(BF16)

    acc = jnp.dot(a_ref[...], wb_ref[...], preferred_element_type=F32)
    if has_res:
        acc = r_ref[...] + acc
    o_ref[...] = acc.astype(o_ref.dtype)


def _mm_w32(a, w, res=None, *, tm, tn, n=None, out_dtype=F32, name="matmul"):
    m, k = a.shape
    n = w.shape[1] if n is None else n
    tm, tn = _tile(m, tm), _tile(n, tn, LANES)
    blk = _nbytes((tm, k), a.dtype) + _nbytes((k, tn), F32) + _nbytes((tm, tn), out_dtype)
    in_specs = [pl.BlockSpec((tm, k), lambda j, i: (i, 0)), pl.BlockSpec((k, tn), lambda j, i: (0, j))]
    args = [a, w]
    if res is not None:
        in_specs.append(pl.BlockSpec((tm, tn), lambda j, i: (i, j)))
        args.append(res)
        blk += _nbytes((tm, tn), F32)
    return pl.pallas_call(
        functools.partial(_mm_w32_kernel, has_res=res is not None),
        grid=(n // tn, m // tm),
        in_specs=in_specs,
        out_specs=pl.BlockSpec((tm, tn), lambda j, i: (i, j)),
        out_shape=jax.ShapeDtypeStruct((m, n), out_dtype),
        scratch_shapes=[pltpu.VMEM((k, tn), BF16)],
        compiler_params=_cparams(("parallel", "arbitrary"), blk, _nbytes((k, tn), BF16) + 2 * _nbytes((tm, tn), F32)),
        name=name,
    )(*args)


def _mm_wt32_kernel(a_ref, wt_ref, o_ref, wb_ref):
    @pl.when(pl.program_id(1) == 0)
    def _():
        wb_ref[...] = wt_ref[...].astype(BF16)

    o_ref[...] = lax.dot_general(a_ref[...], wb_ref[...], (((1,), (1,)), ((), ())),
                                 preferred_element_type=F32).astype(o_ref.dtype)


def _mm_wt32(a, wt, *, row0, n, tm, tn, name):
    m, k = a.shape
    tm, tn = _tile(m, tm), _tile(n, tn, LANES)
    blk = _nbytes((tm, k), a.dtype) + _nbytes((tn, k), F32) + _nbytes((tm, tn), F32)
    return pl.pallas_call(
        _mm_wt32_kernel,
        grid=(n // tn, m // tm),
        in_specs=[
            pl.BlockSpec((tm, k), lambda j, i: (i, 0)),
            pl.BlockSpec((pl.Element(tn), pl.Element(k)), lambda j, i: (pl.multiple_of(row0 + j * tn, 8), 0)),
        ],
        out_specs=pl.BlockSpec((tm, tn), lambda j, i: (i, j)),
        out_shape=jax.ShapeDtypeStruct((m, n), F32),
        scratch_shapes=[pltpu.VMEM((tn, k), BF16)],
        compiler_params=_cparams(("parallel", "arbitrary"), blk, _nbytes((tn, k), BF16) + 2 * _nbytes((tm, tn), F32)),
        name=name,
    )(a, wt)


def _gated_kernel(x_ref, wg_ref, wu_ref, wd_ref, o_ref, wdb_ref):
    @pl.when(pl.program_id(0) == 0)
    def _():
        wdb_ref[...] = wd_ref[...].astype(BF16)

    x = x_ref[...]
    g = jnp.dot(x, wg_ref[...].astype(BF16), preferred_element_type=F32)
    u = jnp.dot(x, wu_ref[...].astype(BF16), preferred_element_type=F32)
    o_ref[...] = (0.5 * (g / (1.0 + jnp.exp(-g))) * u).astype(o_ref.dtype)


def _gated(x, wg, wu, wd, *, tm, tn):
    m, k = x.shape
    n = wg.shape[1]
    d_out = wd.shape[1]
    tm, tn = _tile(m, tm), _tile(n, tn, LANES)
    nj = n // tn
    blk = (2 * _nbytes((k, tn), F32) + _nbytes((tm, tn), BF16)
           + _nbytes((tn, d_out), F32) + _nbytes((tn, d_out), BF16))
    wd_map = lambda i, j: (jnp.where(i == 0, j, nj - 1), 0)
    return pl.pallas_call(
        _gated_kernel,
        grid=(m // tm, nj),
        in_specs=[
            pl.BlockSpec((tm, k), lambda i, j: (i, 0), pipeline_mode=pl.Buffered(1)),
            pl.BlockSpec((k, tn), lambda i, j: (0, j)),
            pl.BlockSpec((k, tn), lambda i, j: (0, j)),
            pl.BlockSpec((tn, d_out), wd_map),
        ],
        out_specs=[pl.BlockSpec((tm, tn), lambda i, j: (i, j)), pl.BlockSpec((tn, d_out), wd_map)],
        out_shape=[jax.ShapeDtypeStruct((m, n), BF16), jax.ShapeDtypeStruct((n, d_out), BF16)],
        compiler_params=_cparams(("arbitrary", "arbitrary"), blk,
                                 _nbytes((tm, k), BF16) + 2 * _nbytes((k, tn), BF16) + 3 * _nbytes((tm, tn), F32)),
        name="ffn_gate_up",
    )(x, wg, wu, wd)


def _swiglu_half(h, xn, wg, wu, wd):
    act, wd_bf = _gated(xn, wg, wu, wd, tm=2048, tn=256)
    return _mm(act, wd_bf, h, tm=512, tn=512, name="ffn_down")


def _rope_chunk(y, cos, s1, s2):
    return y * cos + pltpu.roll(y, ROPE_PAD - MLA_ROPE // 2, 1) * s1 + pltpu.roll(y, MLA_ROPE // 2, 1) * s2


def _mla_q_kernel(u_ref, ga_ref, wt_ref, gt_ref, cos_ref, sin_ref, o_ref):
    cn = _rms_rows(u_ref[:, :MLA_Q_RANK], ga_ref[...]).astype(BF16)
    qt = lax.dot_general(wt_ref[...], cn, (((1,), (1,)), ((), ())), preferred_element_type=F32)
    cos, sin, gain = cos_ref[...], sin_ref[...], gt_ref[...]
    half = MLA_ROPE // 2
    zeros = jnp.zeros((MLA_HEAD_PAD - MLA_QK, qt.shape[1]), o_ref.dtype)
    for h in range(MLA_HEADS):
        lo = h * MLA_HEAD_PAD
        c = qt[lo:lo + MLA_HEAD_PAD, :]
        y = c * lax.rsqrt(jnp.sum(c * c, axis=0, keepdims=True) / float(MLA_QK) + EPS) * gain
        x1, x2 = y[MLA_NOPE:MLA_NOPE + half, :], y[MLA_NOPE + half:MLA_QK, :]
        o_ref[lo:lo + MLA_NOPE, :] = y[:MLA_NOPE, :].astype(o_ref.dtype)
        o_ref[lo + MLA_NOPE:lo + MLA_NOPE + half, :] = (x1 * cos - x2 * sin).astype(o_ref.dtype)
        o_ref[lo + MLA_NOPE + half:lo + MLA_QK, :] = (x2 * cos + x1 * sin).astype(o_ref.dtype)
        o_ref[lo + MLA_QK:lo + MLA_HEAD_PAD, :] = zeros


def _mla_kv_kernel(u_ref, ga_ref, wk_ref, wvt_ref, gn_ref, gr_ref, cos_ref, s1_ref, s2_ref, k_ref, vt_ref):
    cn = _rms_rows(u_ref[:, U_CKV:U_KR], ga_ref[...]).astype(BF16)
    kv = jnp.dot(cn, wk_ref[...], preferred_element_type=F32)
    vt = lax.dot_general(wvt_ref[...], cn, (((1,), (1,)), ((), ())), preferred_element_type=F32)
    vt_ref[...] = vt.astype(vt_ref.dtype)
    kr = u_ref[:, U_KR:U_MLA]
    kr = jnp.where(lax.broadcasted_iota(jnp.int32, kr.shape, 1) < MLA_ROPE, kr, 0.0)
    ss_r = jnp.sum(kr * kr, axis=-1, keepdims=True)
    cos, s1, s2 = cos_ref[...], s1_ref[...], s2_ref[...]
    gn, gr = gn_ref[...], gr_ref[...]
    for h in range(MLA_HEADS):
        kn = kv[:, h * MLA_NOPE:(h + 1) * MLA_NOPE]
        r = lax.rsqrt((jnp.sum(kn * kn, axis=-1, keepdims=True) + ss_r) / float(MLA_QK) + EPS)
        lo = h * MLA_HEAD_PAD
        k_ref[:, lo:lo + MLA_NOPE] = (kn * r * gn).astype(k_ref.dtype)
        k_ref[:, lo + MLA_NOPE:lo + MLA_HEAD_PAD] = _rope_chunk(kr * r * gr, cos, s1, s2).astype(k_ref.dtype)


def _rope_tables(seq):
    half = MLA_ROPE // 2
    inv = ROPE_THETA ** (-jnp.arange(half, dtype=F32) / half)
    ang = jnp.arange(seq).astype(F32)[:, None] * inv[None, :]
    cos, sin = jnp.cos(ang), jnp.sin(ang)
    z = jnp.zeros_like(cos)
    pad = jnp.zeros((seq, ROPE_PAD - MLA_ROPE), F32)
    return (jnp.concatenate([cos, cos, pad], 1), jnp.concatenate([-sin, z, pad], 1),
            jnp.concatenate([z, sin, pad], 1), cos.T, sin.T)


def _mla_prep(u, seq, q_a_norm, kv_a_norm, w_uq, w_ukv, q_norm, k_norm):
    t = u.shape[0]
    tm = _tile(seq, KEY_BLOCK)
    spb = seq // tm
    cos, s1, s2, cos_t, sin_t = _rope_tables(seq)
    tab_spec = pl.BlockSpec((tm, ROPE_PAD), lambda i: (i % spb, 0))
    tab_t_spec = pl.BlockSpec((MLA_ROPE // 2, tm), lambda i: (0, i % spb))
    qw = MLA_HEADS * MLA_HEAD_PAD
    vw = MLA_HEADS * MLA_V

    wq_t = jnp.pad(w_uq.reshape(MLA_Q_RANK, MLA_HEADS, MLA_QK), ((0, 0), (0, 0), (0, MLA_HEAD_PAD - MLA_QK)))
    wq_t = wq_t.reshape(MLA_Q_RANK, qw).T.astype(BF16)
    gain_t = jnp.concatenate([q_norm * (MLA_QK ** -0.5 * LOG2E), jnp.zeros((MLA_HEAD_PAD - MLA_QK,), F32)])
    gain_t = jnp.broadcast_to(gain_t[:, None], (MLA_HEAD_PAD, tm))
    blk = (_nbytes((tm, U_MLA), F32) + _nbytes(wq_t.shape, BF16) + _nbytes((qw, tm), BF16)
           + _nbytes((MLA_HEAD_PAD, tm), F32) + 2 * _nbytes((MLA_ROPE // 2, tm), F32))
    q_t = pl.pallas_call(
        _mla_q_kernel,
        grid=(t // tm,),
        in_specs=[
            pl.BlockSpec((tm, U_MLA), lambda i: (i, 0)),
            pl.BlockSpec((1, MLA_Q_RANK), lambda i: (0, 0)),
            pl.BlockSpec(wq_t.shape, lambda i: (0, 0)),
            pl.BlockSpec((MLA_HEAD_PAD, tm), lambda i: (0, 0)),
            tab_t_spec, tab_t_spec,
        ],
        out_specs=pl.BlockSpec((qw, tm), lambda i: (0, i)),
        out_shape=jax.ShapeDtypeStruct((qw, t), BF16),
        compiler_params=_cparams(("parallel",), blk, 3 * _nbytes((qw, tm), F32)),
        name="mla_q_prep",
    )(u, q_a_norm.reshape(1, -1), wq_t, gain_t, cos_t, sin_t)

    wkv = w_ukv.reshape(MLA_KV_RANK, MLA_HEADS, MLA_NOPE + MLA_V)
    wk = wkv[:, :, :MLA_NOPE].reshape(MLA_KV_RANK, -1).astype(BF16)
    wv_t = wkv[:, :, MLA_NOPE:].reshape(MLA_KV_RANK, -1).T.astype(BF16)
    blk = (_nbytes((tm, U_MLA), F32) + _nbytes(wk.shape, BF16) + _nbytes(wv_t.shape, BF16) + _nbytes((tm, qw), BF16)
           + _nbytes((vw, tm), BF16) + 3 * _nbytes((tm, ROPE_PAD), F32))
    k, v_t = pl.pallas_call(
        _mla_kv_kernel,
        grid=(t // tm,),
        in_specs=[
            pl.BlockSpec((tm, U_MLA), lambda i: (i, 0)),
            pl.BlockSpec((1, MLA_KV_RANK), lambda i: (0, 0)),
            pl.BlockSpec(wk.shape, lambda i: (0, 0)),
            pl.BlockSpec(wv_t.shape, lambda i: (0, 0)),
            pl.BlockSpec((1, MLA_NOPE), lambda i: (0, 0)),
            pl.BlockSpec((1, ROPE_PAD), lambda i: (0, 0)),
            tab_spec, tab_spec, tab_spec,
        ],
        out_specs=[pl.BlockSpec((tm, qw), lambda i: (i, 0)), pl.BlockSpec((None, vw, tm), lambda i: (i, 0, 0))],
        out_shape=[jax.ShapeDtypeStruct((t, qw), BF16), jax.ShapeDtypeStruct((t // tm, vw, tm), BF16)],
        compiler_params=_cparams(("parallel",), blk, 3 * _nbytes((tm, qw), F32)),
        name="mla_kv_prep",
    )(u, kv_a_norm.reshape(1, -1), wk, wv_t, k_norm[:MLA_NOPE].reshape(1, -1),
      jnp.concatenate([k_norm[MLA_NOPE:], jnp.zeros((ROPE_PAD - MLA_ROPE,), F32)]).reshape(1, -1), cos, s1, s2)
    return q_t, k, v_t


def _mla_attn_kernel(qt_ref, k_ref, vt_ref, o_ref, sa_ref, sb_ref, *, tq, kb):
    per = tq // kb
    nq = qt_ref.shape[1] // tq
    nh = qt_ref.shape[0] // MLA_HEAD_PAD

    def scores(qi, blk, s_ref):
        for hh in range(nh):
            qt = qt_ref[hh * MLA_HEAD_PAD:(hh + 1) * MLA_HEAD_PAD, qi * tq:(qi + 1) * tq]
            for d in range(per):
                start = (blk * per + d) * kb
                kblk = k_ref[start:start + kb, hh * MLA_HEAD_PAD:(hh + 1) * MLA_HEAD_PAD]
                s = jnp.dot(kblk, qt, preferred_element_type=F32)
                if blk == qi:
                    key = d * kb + lax.broadcasted_iota(jnp.int32, s.shape, 0)
                    qry = lax.broadcasted_iota(jnp.int32, s.shape, 1)
                    s = jnp.where(key <= qry, s, NEG)
                s_ref[hh, d * kb:(d + 1) * kb, :] = s

    def softmax_pv(blk, s_ref, carries):
        out = []
        for hh in range(nh):
            m, l, acc = carries[hh]
            ss = [s_ref[hh, d * kb:(d + 1) * kb, :] for d in range(per)]
            m_new = m
            for s in ss:
                m_new = jnp.maximum(m_new, jnp.max(s, axis=0, keepdims=True))
            alpha = jnp.exp2(m - m_new)
            l = alpha * l
            acc = alpha * acc
            for d, s in enumerate(ss):
                p = jnp.exp2(s - m_new)
                l = l + jnp.sum(p, axis=0, keepdims=True)
                vt = vt_ref[blk * per + d, hh * MLA_V:(hh + 1) * MLA_V, :]
                acc = acc + jnp.dot(vt, p.astype(BF16), preferred_element_type=F32)
            out.append((m_new, l, acc))
        return tuple(out)

    items = [(qi, blk) for qi in range(nq) for blk in range(qi + 1)]
    bufs = (sa_ref, sb_ref)
    scores(*items[0], bufs[0])
    carries = None
    for idx, (qi, blk) in enumerate(items):
        if idx + 1 < len(items):
            scores(*items[idx + 1], bufs[(idx + 1) % 2])
        if blk == 0:
            carries = tuple((jnp.full((1, tq), NEG, F32), jnp.zeros((1, tq), F32), jnp.zeros((MLA_V, tq), F32))
                            for _ in range(nh))
        carries = softmax_pv(blk, bufs[idx % 2], carries)
        if blk == qi:
            for hh in range(nh):
                _, l, acc = carries[hh]
                o_ref[qi * tq:(qi + 1) * tq, hh * MLA_V:(hh + 1) * MLA_V] = (acc * (1.0 / l)).T


def _mla_attention(q_t, k, v_t, batch, seq):
    t = k.shape[0]
    kb = v_t.shape[2]
    tq = _tile(seq, 512)
    nkb = seq // kb
    nh = 2
    blk = nh * (2 * _nbytes((seq, MLA_HEAD_PAD), BF16) + _nbytes((seq, MLA_V), BF16) + _nbytes((seq, MLA_V), F32))
    return pl.pallas_call(
        functools.partial(_mla_attn_kernel, tq=tq, kb=kb),
        grid=(batch, MLA_HEADS // nh),
        in_specs=[
            pl.BlockSpec((nh * MLA_HEAD_PAD, seq), lambda b, h: (h, b)),
            pl.BlockSpec((seq, nh * MLA_HEAD_PAD), lambda b, h: (b, h)),
            pl.BlockSpec((nkb, nh * MLA_V, kb), lambda b, h: (b, h, 0)),
        ],
        out_specs=pl.BlockSpec((seq, nh * MLA_V), lambda b, h: (b, h)),
        out_shape=jax.ShapeDtypeStruct((t, MLA_WIDTH), F32),
        scratch_shapes=[pltpu.VMEM((nh, tq, tq), F32), pltpu.VMEM((nh, tq, tq), F32)],
        compiler_params=_cparams(("parallel", "parallel"), blk, 10 * nh * _nbytes((tq, tq), F32)),
        name="mla_attention",
    )(q_t, k, v_t)


def _t5_bucket(dist):
    max_exact = REL_BUCKETS // 2
    d = jnp.maximum(dist, 1).astype(F32)
    large = max_exact + (jnp.log(d / max_exact) / math.log(REL_MAX_DIST / max_exact)
                         * (REL_BUCKETS - max_exact)).astype(jnp.int32)
    large = jnp.minimum(large, REL_BUCKETS - 1)
    return jnp.where(dist < max_exact, dist, large)


def _swa_kernel(sink_ref, q_ref, kp_ref, kc_ref, vp_ref, vc_ref, gfold_ref, bias_ref, o_ref):
    n = pl.program_id(1)
    nk, hd, kvw = 2 * BLOCK_Q, SWA_HEAD_DIM, SWA_KV_WIDTH
    key = lax.broadcasted_iota(jnp.int32, (nk, BLOCK_Q), 0)
    qry = lax.broadcasted_iota(jnp.int32, (nk, BLOCK_Q), 1)
    dist = qry + BLOCK_Q - key
    valid = (dist >= 0) & (dist < WINDOW) & ((key >= BLOCK_Q) | (n > 0))

    k2 = jnp.concatenate([kp_ref[...], kc_ref[...]], axis=0)
    sq = k2 * k2
    sq_hi = sq.astype(BF16)
    sq_lo = (sq - sq_hi.astype(F32)).astype(BF16)
    shift = hd.bit_length() - 1
    ri = lax.broadcasted_iota(jnp.int32, (kvw, kvw), 0) >> shift
    ci = lax.broadcasted_iota(jnp.int32, (kvw, kvw), 1) >> shift
    ind = jnp.where(ri == ci, 1.0, 0.0).astype(BF16)
    ss = jnp.dot(sq_hi, ind, preferred_element_type=F32) + jnp.dot(sq_lo, ind, preferred_element_type=F32)
    k2n = (k2 * lax.rsqrt(ss / float(hd) + EPS) * gfold_ref[...]).astype(BF16)
    v2t = jnp.concatenate([vp_ref[...], vc_ref[...]], axis=0).T.astype(BF16)

    qt = q_ref[...].T
    gw = SWA_GROUP * BLOCK_Q
    rows = []
    for kh in range(SWA_KV_HEADS):
        pieces = []
        if kh > 0:
            pieces.append(jnp.zeros((hd, kh * gw), BF16))
        for g in range(SWA_GROUP):
            j = kh * SWA_GROUP + g
            c = qt[j * hd:(j + 1) * hd, :]
            pieces.append((c * lax.rsqrt(jnp.sum(c * c, axis=0, keepdims=True) / float(hd) + EPS)).astype(BF16))
        if kh < SWA_KV_HEADS - 1:
            pieces.append(jnp.zeros((hd, (SWA_KV_HEADS - 1 - kh) * gw), BF16))
        rows.append(jnp.concatenate(pieces, axis=1))
    q_bd = jnp.concatenate(rows, axis=0)
    s_all = jnp.dot(k2n, q_bd, preferred_element_type=F32)

    es, invs = [], []
    for j in range(SWA_HEADS):
        lo = j * BLOCK_Q
        s = jnp.where(valid, s_all[:, lo:lo + BLOCK_Q] + bias_ref[:, lo:lo + BLOCK_Q], NEG)
        sink = sink_ref[j]
        m = jnp.maximum(jnp.max(s, axis=0, keepdims=True), sink)
        e = jnp.exp2(s - m)
        invs.append(1.0 / (jnp.sum(e, axis=0, keepdims=True) + jnp.exp2(sink - m)))
        es.append(e.astype(BF16))
    e_all = jnp.concatenate(es, axis=1)
    o_all = jnp.dot(v2t, e_all, preferred_element_type=F32)
    outs = []
    for j in range(SWA_HEADS):
        kh = j // SWA_GROUP
        outs.append(o_all[kh * hd:(kh + 1) * hd, j * BLOCK_Q:(j + 1) * BLOCK_Q] * invs[j])
    o_ref[...] = jnp.concatenate(outs, axis=0).T


def _swa_attention(u, batch, seq, q_norm, k_norm, sinks, rel_bias):
    t = u.shape[0]
    nb = seq // BLOCK_Q
    nk = 2 * BLOCK_Q
    by_dist = rel_bias.astype(F32)[_t5_bucket(jnp.arange(WINDOW))].T
    row = jnp.concatenate([jnp.zeros((SWA_HEADS, 1), F32), by_dist], 1)
    bias = jnp.broadcast_to(row[:, None, :], (SWA_HEADS, nk, BLOCK_Q + 1)).reshape(SWA_HEADS, -1)
    bias = bias[:, :nk * BLOCK_Q].reshape(SWA_HEADS, nk, BLOCK_Q)
    bias = bias.transpose(1, 0, 2).reshape(nk, SWA_HEADS * BLOCK_Q) * LOG2E
    sinks = sinks * LOG2E
    gfold = jnp.tile(k_norm * q_norm * (SWA_HEAD_DIM ** -0.5 * LOG2E), SWA_KV_HEADS).reshape(1, SWA_KV_WIDTH)
    cur = lambda c: pl.BlockSpec((BLOCK_Q, SWA_KV_WIDTH), lambda b, n: (b * nb + n, c))
    prev = lambda c: pl.BlockSpec((BLOCK_Q, SWA_KV_WIDTH), lambda b, n: (b * nb + jnp.maximum(n - 1, 0), c))
    ck, cv = S_K // SWA_KV_WIDTH, S_V // SWA_KV_WIDTH
    blk = (2 * _nbytes((BLOCK_Q, SWA_WIDTH), F32) + 4 * _nbytes((BLOCK_Q, SWA_KV_WIDTH), F32) + _nbytes(bias.shape, F32))
    return pl.pallas_call(
        _swa_kernel,
        grid=(batch, nb),
        in_specs=[
            pl.BlockSpec(memory_space=pltpu.SMEM),
            pl.BlockSpec((BLOCK_Q, SWA_WIDTH), lambda b, n: (b * nb + n, 0)),
            prev(ck), cur(ck), prev(cv), cur(cv),
            pl.BlockSpec((1, SWA_KV_WIDTH), lambda b, n: (0, 0)),
            pl.BlockSpec(bias.shape, lambda b, n: (0, 0)),
        ],
        out_specs=pl.BlockSpec((BLOCK_Q, SWA_WIDTH), lambda b, n: (b * nb + n, 0)),
        out_shape=jax.ShapeDtypeStruct((t, SWA_WIDTH), F32),
        compiler_params=_cparams(("parallel", "arbitrary"), blk, 4 * _nbytes((nk, SWA_HEADS * BLOCK_Q), F32)),
        name="swa_attention",
    )(sinks, u, u, u, u, u, gfold, bias)


def _mem_block_kernel(h_ref, kv_ref, gn_ref, wq_ref, wo_ref, gq_ref, gk_ref, gnext_ref, h_out_ref, xn_out_ref):
    h = h_ref[...]
    xn = _rms_rows(h, gn_ref[...]).astype(BF16)
    qm = jnp.dot(xn, wq_ref[...], preferred_element_type=F32)
    gq = gq_ref[...] * (MEM_HEAD_DIM ** -0.5)
    gk = gk_ref[...]
    outs = []
    for hd in range(MEM_HEADS):
        q = _rms_rows(qm[:, hd * MEM_HEAD_DIM:(hd + 1) * MEM_HEAD_DIM], gq).astype(BF16)
        lo = 2 * hd * MEM_HEAD_DIM
        k = _rms_rows(kv_ref[:, lo:lo + MEM_HEAD_DIM], gk).astype(BF16)
        v = kv_ref[:, lo + MEM_HEAD_DIM:lo + 2 * MEM_HEAD_DIM].astype(BF16)
        s = lax.dot_general(q, k, (((1,), (1,)), ((), ())), preferred_element_type=F32)
        e = jnp.exp(s - jnp.max(s, axis=-1, keepdims=True))
        p = (e / jnp.sum(e, axis=-1, keepdims=True)).astype(BF16)
        outs.append(jnp.dot(p, v, preferred_element_type=F32).astype(BF16))
    o = jnp.concatenate(outs, axis=-1)
    h_new = h + jnp.dot(o, wo_ref[...], preferred_element_type=F32)
    h_out_ref[...] = h_new
    xn_out_ref[...] = _rms_rows(h_new, gnext_ref[...]).astype(xn_out_ref.dtype)


def _mem_block(h, kvm, batch, seq, mem_len, norm_g, w_q, w_o, q_norm, k_norm, next_norm_g):
    t, d = h.shape
    w = w_q.shape[1]
    tm = _tile(seq, 256)
    nq = seq // tm
    once = pl.Buffered(1)
    blk = 2 * _nbytes((tm, d), F32) + _nbytes((tm, d), BF16) + _nbytes((mem_len, 2 * w), F32)
    temp = 2 * _nbytes((d, w), BF16) + 3 * _nbytes((tm, d), F32)
    return pl.pallas_call(
        _mem_block_kernel,
        grid=(batch, nq),
        in_specs=[
            pl.BlockSpec((tm, d), lambda b, i: (b * nq + i, 0)),
            pl.BlockSpec((mem_len, 2 * w), lambda b, i: (b, 0)),
            pl.BlockSpec((1, d), lambda b, i: (0, 0)),
            pl.BlockSpec((d, w), lambda b, i: (0, 0), pipeline_mode=once),
            pl.BlockSpec((w, d), lambda b, i: (0, 0), pipeline_mode=once),
            pl.BlockSpec((1, MEM_HEAD_DIM), lambda b, i: (0, 0)),
            pl.BlockSpec((1, MEM_HEAD_DIM), lambda b, i: (0, 0)),
            pl.BlockSpec((1, d), lambda b, i: (0, 0)),
        ],
        out_specs=[pl.BlockSpec((tm, d), lambda b, i: (b * nq + i, 0)), pl.BlockSpec((tm, d), lambda b, i: (b * nq + i, 0))],
        out_shape=[jax.ShapeDtypeStruct((t, d), F32), jax.ShapeDtypeStruct((t, d), BF16)],
        compiler_params=_cparams(("parallel", "arbitrary"), blk, temp),
        name="mem_block",
    )(h, kvm, norm_g.reshape(1, d), w_q.astype(BF16), w_o.astype(BF16), q_norm.reshape(1, -1), k_norm.reshape(1, -1),
      next_norm_g.reshape(1, d))


def _layer(h, mem, rel_bias, p, batch, seq):
    h = _swiglu_half(h, _norm(h, p['ffn_a_norm']), p['ffn_a_gate'], p['ffn_a_up'], p['ffn_a_down'])

    xn = _norm(h, p['mix_norm'])
    w_in_t = p['w_in'].T
    u_mla = _mm_wt32(xn, w_in_t, row0=0, n=U_MLA, tm=1024, tn=512, name="in_proj_mla")
    u_swa = _mm_wt32(xn, w_in_t, row0=IN_SWA, n=w_in_t.shape[0] - IN_SWA, tm=1024, tn=512, name="in_proj_swa")

    q_t, k, v_t = _mla_prep(u_mla, seq, p['mla_q_a_norm'], p['mla_kv_a_norm'], p['mla_w_uq'], p['mla_w_ukv'],
                            p['mla_q_norm'], p['mla_k_norm'])
    o_a = _mla_attention(q_t, k, v_t, batch, seq)
    o_b = _swa_attention(u_swa, batch, seq, p['swa_q_norm'], p['swa_k_norm'], p['swa_sinks'], rel_bias)
    o = _dual_norm(o_a, o_b, p['out_norm_mla'], p['out_norm_swa'])
    h = _mm_w32(o, p['w_out'], h, tm=1024, tn=512, name="out_proj")

    mem_len = mem.shape[0] // batch
    kvm = _mm_w32(_norm(mem, p['mem_norm']), p['mem_w_kv'], tm=512, tn=512, name="mem_kv_proj")
    h, xn = _mem_block(h, kvm, batch, seq, mem_len, p['mem_attn_norm'], p['mem_w_q'], p['mem_w_o'],
                       p['mem_q_norm'], p['mem_k_norm'], p['ffn_b_norm'])

    return _swiglu_half(h, xn, p['ffn_b_gate'], p['ffn_b_up'], p['ffn_b_down'])


_PARAM_NAMES = (
    'ffn_a_norm', 'ffn_a_gate', 'ffn_a_up', 'ffn_a_down', 'mix_norm', 'w_in',
    'mla_q_a_norm', 'mla_kv_a_norm', 'mla_w_uq', 'mla_w_ukv', 'mla_q_norm', 'mla_k_norm',
    'swa_q_norm', 'swa_k_norm', 'swa_sinks', 'out_norm_mla', 'out_norm_swa', 'w_out',
    'mem_attn_norm', 'mem_norm', 'mem_w_q', 'mem_w_kv', 'mem_q_norm', 'mem_k_norm', 'mem_w_o',
    'ffn_b_norm', 'ffn_b_gate', 'ffn_b_up', 'ffn_b_down',
)


def kernel(x, mem, rel_bias, ffn_a_norm, ffn_a_gate, ffn_a_up, ffn_a_down, mix_norm, w_in, mla_q_a_norm, mla_kv_a_norm, mla_w_uq, mla_w_ukv, mla_q_norm, mla_k_norm, swa_q_norm, swa_k_norm, swa_sinks, out_norm_mla, out_norm_swa, w_out, mem_attn_norm, mem_norm, mem_w_q, mem_w_kv, mem_q_norm, mem_k_norm, mem_w_o, ffn_b_norm, ffn_b_gate, ffn_b_up, ffn_b_down):
    stacked = (ffn_a_norm, ffn_a_gate, ffn_a_up, ffn_a_down, mix_norm, w_in, mla_q_a_norm, mla_kv_a_norm,
               mla_w_uq, mla_w_ukv, mla_q_norm, mla_k_norm, swa_q_norm, swa_k_norm, swa_sinks, out_norm_mla,
               out_norm_swa, w_out, mem_attn_norm, mem_norm, mem_w_q, mem_w_kv, mem_q_norm, mem_k_norm, mem_w_o,
               ffn_b_norm, ffn_b_gate, ffn_b_up, ffn_b_down)
    batch, seq, d = x.shape
    h = x.reshape(batch * seq, d)
    mem2 = mem.reshape(-1, d)
    for layer in range(ffn_a_norm.shape[0]):
        p = {name: arr[layer] for name, arr in zip(_PARAM_NAMES, stacked)}
        h = _layer(h, mem2, rel_bias, p, batch, seq)
    return h.reshape(batch, seq, d)
```

```python
import functools
import math

import jax
import jax.numpy as jnp
from jax import lax
from jax.experimental import pallas as pl
from jax.experimental.pallas import tpu as pltpu

F32 = jnp.float32
BF16 = jnp.bfloat16

MEM_HEADS = 4
MEM_HEAD_DIM = 128
MLA_HEADS = 16
MLA_Q_RANK = 896
MLA_KV_RANK = 512
MLA_NOPE = 128
MLA_ROPE = 64
MLA_V = 128
ROPE_THETA = 10000.0
SWA_HEADS = 32
SWA_KV_HEADS = 4
SWA_HEAD_DIM = 64
WINDOW = 128
REL_BUCKETS = 32
REL_MAX_DIST = 128
BLOCK_Q = 128
EPS = 1e-6
NEG = -1e30

MLA_QK = MLA_NOPE + MLA_ROPE
SWA_GROUP = SWA_HEADS // SWA_KV_HEADS
SWA_WIDTH = SWA_HEADS * SWA_HEAD_DIM
SWA_KV_WIDTH = SWA_KV_HEADS * SWA_HEAD_DIM
MLA_WIDTH = MLA_HEADS * MLA_V

LANES = 128
MLA_HEAD_PAD = 2 * LANES
ROPE_PAD = LANES

U_CKV = MLA_Q_RANK
U_KR = U_CKV + MLA_KV_RANK
U_MLA = U_KR + ROPE_PAD
IN_SWA = U_KR + MLA_ROPE
S_K = SWA_WIDTH
S_V = S_K + SWA_KV_WIDTH

LOG2E = math.log2(math.e)
KEY_BLOCK = 256

V7X_SCOPED_VMEM_BYTES = 60000 * 1024


def _tile(n, pref, align=8):
    if n <= pref:
        return n
    t = (pref // align) * align
    while t >= align:
        if n % t == 0:
            return t
        t -= align
    return n


def _nbytes(shape, dtype):
    n = 1
    for s in shape:
        n *= s
    return n * jnp.dtype(dtype).itemsize


def _cparams(semantics, pipelined_bytes, temp_bytes=0):
    need = 2 * pipelined_bytes + temp_bytes + (4 << 20)
    return pltpu.CompilerParams(
        dimension_semantics=semantics,
        vmem_limit_bytes=int(min(max(need, 16 << 20), V7X_SCOPED_VMEM_BYTES)),
    )


def _rms_rows(x, g, n=None):
    n = x.shape[-1] if n is None else n
    r = lax.rsqrt(jnp.sum(x * x, axis=-1, keepdims=True) / float(n) + EPS)
    return x * r * g


def _norm_kernel(x_ref, g_ref, o_ref):
    o_ref[...] = _rms_rows(x_ref[...], g_ref[...]).astype(o_ref.dtype)


def _norm(x, g, out_dtype=BF16):
    rows, d = x.shape
    tr = _tile(rows, 512)
    blk = _nbytes((tr, d), F32) + _nbytes((tr, d), out_dtype)
    return pl.pallas_call(
        _norm_kernel,
        grid=(rows // tr,),
        in_specs=[pl.BlockSpec((tr, d), lambda i: (i, 0)), pl.BlockSpec((1, d), lambda i: (0, 0))],
        out_specs=pl.BlockSpec((tr, d), lambda i: (i, 0)),
        out_shape=jax.ShapeDtypeStruct((rows, d), out_dtype),
        compiler_params=_cparams(("parallel",), blk, _nbytes((tr, d), F32)),
        name="rmsnorm",
    )(x, g.reshape(1, d))


def _mm_kernel(a_ref, w_ref, o_ref):
    o_ref[...] = jnp.dot(a_ref[...], w_ref[...], preferred_element_type=F32).astype(o_ref.dtype)


def _mm_res_kernel(a_ref, w_ref, r_ref, o_ref):
    o_ref[...] = r_ref[...] + jnp.dot(a_ref[...], w_ref[...], preferred_element_type=F32)


def _mm(a, w, res=None, *, tm, tn, out_dtype=F32, name="matmul"):
    m, k = a.shape
    n = w.shape[1]
    tm, tn = _tile(m, tm), _tile(n, tn, LANES)
    blk = _nbytes((tm, k), a.dtype) + _nbytes((k, tn), w.dtype) + _nbytes((tm, tn), out_dtype)
    in_specs = [pl.BlockSpec((tm, k), lambda i, j: (i, 0)), pl.BlockSpec((k, tn), lambda i, j: (0, j))]
    args = [a, w]
    kern = _mm_kernel
    if res is not None:
        in_specs.append(pl.BlockSpec((tm, tn), lambda i, j: (i, j)))
        args.append(res)
        blk += _nbytes((tm, tn), F32)
        kern = _mm_res_kernel
    return pl.pallas_call(
        kern,
        grid=(m // tm, n // tn),
        in_specs=in_specs,
        out_specs=pl.BlockSpec((tm, tn), lambda i, j: (i, j)),
        out_shape=jax.ShapeDtypeStruct((m, n), out_dtype),
        compiler_params=_cparams(("parallel", "arbitrary"), blk, 2 * _nbytes((tm, tn), F32)),
        name=name,
    )(*args)


def _row_ssq(x):
    return jnp.broadcast_to(jnp.sum(x * x, axis=-1, keepdims=True), (x.shape[0], LANES))


def _inv_rms(ssq_block, n):
    parts = ssq_block.shape[1] // LANES
    total = ssq_block[:, :1] if parts == 1 else jnp.sum(ssq_block, axis=-1, keepdims=True) / float(LANES)
    return lax.rsqrt(total / float(n) + EPS)


def _mm_res_stats_kernel(a_ref, w_ref, r_ref, o_ref, ob_ref, ssq_ref):
    h = r_ref[...] + jnp.dot(a_ref[...], w_ref[...], preferred_element_type=F32)
    o_ref[...] = h
    ob_ref[...] = h.astype(ob_ref.dtype)
    part = _row_ssq(h)

    @pl.when(pl.program_id(1) == 0)
    def _():
        ssq_ref[...] = part

    @pl.when(pl.program_id(1) != 0)
    def _():
        ssq_ref[...] += part


def _mm_res_stats(a, w, res, *, tm, tn, name):
    m, k = a.shape
    n = w.shape[1]
    tm, tn = _tile(m, tm), _tile(n, tn, LANES)
    blk = (_nbytes((tm, k), a.dtype) + _nbytes((k, tn), w.dtype) + 2 * _nbytes((tm, tn), F32) + _nbytes((tm, tn), BF16)
           + _nbytes((tm, LANES), F32))
    return pl.pallas_call(
        _mm_res_stats_kernel,
        grid=(m // tm, n // tn),
        in_specs=[
            pl.BlockSpec((tm, k), lambda i, j: (i, 0)),
            pl.BlockSpec((k, tn), lambda i, j: (0, j)),
            pl.BlockSpec((tm, tn), lambda i, j: (i, j)),
        ],
        out_specs=[
            pl.BlockSpec((tm, tn), lambda i, j: (i, j)),
            pl.BlockSpec((tm, tn), lambda i, j: (i, j)),
            pl.BlockSpec((tm, LANES), lambda i, j: (i, 0)),
        ],
        out_shape=[jax.ShapeDtypeStruct((m, n), F32), jax.ShapeDtypeStruct((m, n), BF16),
                   jax.ShapeDtypeStruct((m, LANES), F32)],
        compiler_params=_cparams(("parallel", "arbitrary"), blk, 3 * _nbytes((tm, tn), F32)),
        name=name,
    )(a, w, res)


def _proj_t_kernel(a_ref, ssq_ref, g_ref, wt_ref, o_ref):
    wb = (wt_ref[...] * g_ref[...]).astype(BF16)
    acc = lax.dot_general(a_ref[...], wb, (((1,), (1,)), ((), ())), preferred_element_type=F32)
    o_ref[...] = acc * _inv_rms(ssq_ref[...], a_ref.shape[1])


def _proj_t(a, ssq, g, wt, *, n_first, second_start, n_second, tm, tn, name):
    m, k = a.shape
    tm = _tile(m, tm)
    t1 = n_first // tn
    nt = t1 + n_second // tn
    shift = second_start - n_first
    wt_map = lambda i, j: (pl.multiple_of(j * tn + jnp.where(j >= t1, shift, 0), 8), 0)
    blk = _nbytes((tn, k), F32) + _nbytes((tm, tn), F32) + _nbytes((tm, LANES), F32)
    return pl.pallas_call(
        _proj_t_kernel,
        grid=(m // tm, nt),
        in_specs=[
            pl.BlockSpec((tm, k), lambda i, j: (i, 0), pipeline_mode=pl.Buffered(1)),
            pl.BlockSpec((tm, LANES), lambda i, j: (i, 0)),
            pl.BlockSpec((1, k), lambda i, j: (0, 0)),
            pl.BlockSpec((pl.Element(tn), pl.Element(k)), wt_map),
        ],
        out_specs=pl.BlockSpec((tm, tn), lambda i, j: (i, j)),
        out_shape=jax.ShapeDtypeStruct((m, nt * tn), F32),
        compiler_params=_cparams(("parallel", "arbitrary"), blk,
                                 _nbytes((tm, k), BF16) + _nbytes((tn, k), BF16) + 2 * _nbytes((tm, tn), F32)),
        name=name,
    )(a, ssq, g.reshape(1, k), wt)


def _proj2_kernel(a_ref, b_ref, ssqa_ref, ssqb_ref, gt_ref, w_ref, r_ref, o_ref):
    ka = a_ref.shape[1]
    wb = (w_ref[...] * gt_ref[...]).astype(BF16)
    ya = jnp.dot(a_ref[...], wb[:ka], preferred_element_type=F32) * _inv_rms(ssqa_ref[...], ka)
    yb = jnp.dot(b_ref[...], wb[ka:], preferred_element_type=F32) * _inv_rms(ssqb_ref[...], b_ref.shape[1])
    o_ref[...] = r_ref[...] + ya + yb


def _proj2(a, b, ssqa, ssqb, ga, gb, w, res, *, tm, tn, name):
    m, ka = a.shape
    kb = b.shape[1]
    n = w.shape[1]
    tm, tn = _tile(m, tm), _tile(n, tn, LANES)
    gtab = jnp.broadcast_to(jnp.concatenate([ga, gb])[:, None], (ka + kb, tn))
    once = pl.Buffered(1)
    blk = _nbytes((ka + kb, tn), F32) * 2 + 2 * _nbytes((tm, tn), F32) + _nbytes((tm, ssqa.shape[1] + LANES), F32)
    return pl.pallas_call(
        _proj2_kernel,
        grid=(m // tm, n // tn),
        in_specs=[
            pl.BlockSpec((tm, ka), lambda i, j: (i, 0), pipeline_mode=once),
            pl.BlockSpec((tm, kb), lambda i, j: (i, 0), pipeline_mode=once),
            pl.BlockSpec((tm, ssqa.shape[1]), lambda i, j: (i, 0)),
            pl.BlockSpec((tm, LANES), lambda i, j: (i, 0)),
            pl.BlockSpec((ka + kb, tn), lambda i, j: (0, 0), pipeline_mode=once),
            pl.BlockSpec((ka + kb, tn), lambda i, j: (0, j)),
            pl.BlockSpec((tm, tn), lambda i, j: (i, j)),
        ],
        out_specs=pl.BlockSpec((tm, tn), lambda i, j: (i, j)),
        out_shape=jax.ShapeDtypeStruct((m, n), F32),
        compiler_params=_cparams(("parallel", "arbitrary"), blk,
                                 _nbytes((tm, ka + kb), BF16) + _nbytes((ka + kb, tn), BF16) + 3 * _nbytes((tm, tn), F32)),
        name=name,
    )(a, b, ssqa, ssqb, gtab, w, res)


def _mm_w32_kernel(a_ref, w_ref, *rest, has_res):
    if has_res:
        r_ref, o_ref, wb_ref = rest
    else:
        o_ref, wb_ref = rest

    @pl.when(pl.program_id(1) == 0)
    def _():
        wb_ref[...] = w_ref[...].astype(BF16)

    acc = jnp.dot(a_ref[...], wb_ref[...], preferred_element_type=F32)
    if has_res:
        acc = r_ref[...] + acc
    o_ref[...] = acc.astype(o_ref.dtype)


def _mm_w32(a, w, res=None, *, tm, tn, n=None, out_dtype=F32, name="matmul"):
    m, k = a.shape
    n = w.shape[1] if n is None else n
    tm, tn = _tile(m, tm), _tile(n, tn, LANES)
    blk = _nbytes((tm, k), a.dtype) + _nbytes((k, tn), F32) + _nbytes((tm, tn), out_dtype)
    in_specs = [pl.BlockSpec((tm, k), lambda j, i: (i, 0)), pl.BlockSpec((k, tn), lambda j, i: (0, j))]
    args = [a, w]
    if res is not None:
        in_specs.append(pl.BlockSpec((tm, tn), lambda j, i: (i, j)))
        args.append(res)
        blk += _nbytes((tm, tn), F32)
    return pl.pallas_call(
        functools.partial(_mm_w32_kernel, has_res=res is not None),
        grid=(n // tn, m // tm),
        in_specs=in_specs,
        out_specs=pl.BlockSpec((tm, tn), lambda j, i: (i, j)),
        out_shape=jax.ShapeDtypeStruct((m, n), out_dtype),
        scratch_shapes=[pltpu.VMEM((k, tn), BF16)],
        compiler_params=_cparams(("parallel", "arbitrary"), blk, _nbytes((k, tn), BF16) + 2 * _nbytes((tm, tn), F32)),
        name=name,
    )(*args)


def _gated_kernel(x_ref, wg_ref, wu_ref, wd_ref, o_ref, wdb_ref):
    @pl.when(pl.program_id(0) == 0)
    def _():
        wdb_ref[...] = wd_ref[...].astype(BF16)

    x = x_ref[...]
    g = jnp.dot(x, wg_ref[...].astype(BF16), preferred_element_type=F32)
    u = jnp.dot(x, wu_ref[...].astype(BF16), preferred_element_type=F32)
    o_ref[...] = (0.5 * (g / (1.0 + jnp.exp(-g))) * u).astype(o_ref.dtype)


def _gated(x, wg, wu, wd, *, tm, tn):
    m, k = x.shape
    n = wg.shape[1]
    d_out = wd.shape[1]
    tm, tn = _tile(m, tm), _tile(n, tn, LANES)
    nj = n // tn
    blk = (2 * _nbytes((k, tn), F32) + _nbytes((tm, tn), BF16)
           + _nbytes((tn, d_out), F32) + _nbytes((tn, d_out), BF16))
    wd_map = lambda i, j: (jnp.where(i == 0, j, nj - 1), 0)
    return pl.pallas_call(
        _gated_kernel,
        grid=(m // tm, nj),
        in_specs=[
            pl.BlockSpec((tm, k), lambda i, j: (i, 0), pipeline_mode=pl.Buffered(1)),
            pl.BlockSpec((k, tn), lambda i, j: (0, j)),
            pl.BlockSpec((k, tn), lambda i, j: (0, j)),
            pl.BlockSpec((tn, d_out), wd_map),
        ],
        out_specs=[pl.BlockSpec((tm, tn), lambda i, j: (i, j)), pl.BlockSpec((tn, d_out), wd_map)],
        out_shape=[jax.ShapeDtypeStruct((m, n), BF16), jax.ShapeDtypeStruct((n, d_out), BF16)],
        compiler_params=_cparams(("arbitrary", "arbitrary"), blk,
                                 _nbytes((tm, k), BF16) + 2 * _nbytes((k, tn), BF16) + 3 * _nbytes((tm, tn), F32)),
        name="ffn_gate_up",
    )(x, wg, wu, wd)


def _swiglu_half(h, xn, wg, wu, wd, stats=False):
    act, wd_bf = _gated(xn, wg, wu, wd, tm=2048, tn=256)
    if stats:
        return _mm_res_stats(act, wd_bf, h, tm=512, tn=512, name="ffn_down_stats")
    return _mm(act, wd_bf, h, tm=512, tn=512, name="ffn_down")


def _rope_chunk(y, cos, s1, s2):
    return y * cos + pltpu.roll(y, ROPE_PAD - MLA_ROPE // 2, 1) * s1 + pltpu.roll(y, MLA_ROPE // 2, 1) * s2


def _mla_q_kernel(u_ref, ga_ref, wt_ref, gt_ref, cos_ref, sin_ref, o_ref):
    cn = _rms_rows(u_ref[:, :MLA_Q_RANK], ga_ref[...]).astype(BF16)
    qt = lax.dot_general(wt_ref[...], cn, (((1,), (1,)), ((), ())), preferred_element_type=F32)
    cos, sin, gain = cos_ref[...], sin_ref[...], gt_ref[...]
    half = MLA_ROPE // 2
    zeros = jnp.zeros((MLA_HEAD_PAD - MLA_QK, qt.shape[1]), o_ref.dtype)
    for h in range(MLA_HEADS):
        lo = h * MLA_HEAD_PAD
        c = qt[lo:lo + MLA_HEAD_PAD, :]
        y = c * lax.rsqrt(jnp.sum(c * c, axis=0, keepdims=True) / float(MLA_QK) + EPS) * gain
        x1, x2 = y[MLA_NOPE:MLA_NOPE + half, :], y[MLA_NOPE + half:MLA_QK, :]
        o_ref[lo:lo + MLA_NOPE, :] = y[:MLA_NOPE, :].astype(o_ref.dtype)
        o_ref[lo + MLA_NOPE:lo + MLA_NOPE + half, :] = (x1 * cos - x2 * sin).astype(o_ref.dtype)
        o_ref[lo + MLA_NOPE + half:lo + MLA_QK, :] = (x2 * cos + x1 * sin).astype(o_ref.dtype)
        o_ref[lo + MLA_QK:lo + MLA_HEAD_PAD, :] = zeros


def _mla_kv_kernel(u_ref, ga_ref, wk_ref, wvt_ref, gn_ref, gr_ref, cos_ref, s1_ref, s2_ref, k_ref, vt_ref):
    cn = _rms_rows(u_ref[:, U_CKV:U_KR], ga_ref[...]).astype(BF16)
    kv = jnp.dot(cn, wk_ref[...], preferred_element_type=F32)
    vt = lax.dot_general(wvt_ref[...], cn, (((1,), (1,)), ((), ())), preferred_element_type=F32)
    vt_ref[...] = vt.astype(vt_ref.dtype)
    kr = u_ref[:, U_KR:U_MLA]
    kr = jnp.where(lax.broadcasted_iota(jnp.int32, kr.shape, 1) < MLA_ROPE, kr, 0.0)
    ss_r = jnp.sum(kr * kr, axis=-1, keepdims=True)
    cos, s1, s2 = cos_ref[...], s1_ref[...], s2_ref[...]
    gn, gr = gn_ref[...], gr_ref[...]
    for h in range(MLA_HEADS):
        kn = kv[:, h * MLA_NOPE:(h + 1) * MLA_NOPE]
        r = lax.rsqrt((jnp.sum(kn * kn, axis=-1, keepdims=True) + ss_r) / float(MLA_QK) + EPS)
        lo = h * MLA_HEAD_PAD
        k_ref[:, lo:lo + MLA_NOPE] = (kn * r * gn).astype(k_ref.dtype)
        k_ref[:, lo + MLA_NOPE:lo + MLA_HEAD_PAD] = _rope_chunk(kr * r * gr, cos, s1, s2).astype(k_ref.dtype)


def _rope_tables(seq):
    half = MLA_ROPE // 2
    inv = ROPE_THETA ** (-jnp.arange(half, dtype=F32) / half)
    ang = jnp.arange(seq).astype(F32)[:, None] * inv[None, :]
    cos, sin = jnp.cos(ang), jnp.sin(ang)
    z = jnp.zeros_like(cos)
    pad = jnp.zeros((seq, ROPE_PAD - MLA_ROPE), F32)
    return (jnp.concatenate([cos, cos, pad], 1), jnp.concatenate([-sin, z, pad], 1),
            jnp.concatenate([z, sin, pad], 1), cos.T, sin.T)


def _mla_prep(u, seq, q_a_norm, kv_a_norm, w_uq, w_ukv, q_norm, k_norm):
    t = u.shape[0]
    tm = _tile(seq, KEY_BLOCK)
    spb = seq // tm
    cos, s1, s2, cos_t, sin_t = _rope_tables(seq)
    tab_spec = pl.BlockSpec((tm, ROPE_PAD), lambda i: (i % spb, 0))
    tab_t_spec = pl.BlockSpec((MLA_ROPE // 2, tm), lambda i: (0, i % spb))
    qw = MLA_HEADS * MLA_HEAD_PAD
    vw = MLA_HEADS * MLA_V

    wq_t = jnp.pad(w_uq.reshape(MLA_Q_RANK, MLA_HEADS, MLA_QK), ((0, 0), (0, 0), (0, MLA_HEAD_PAD - MLA_QK)))
    wq_t = wq_t.reshape(MLA_Q_RANK, qw).T.astype(BF16)
    gain_t = jnp.concatenate([q_norm * (MLA_QK ** -0.5 * LOG2E), jnp.zeros((MLA_HEAD_PAD - MLA_QK,), F32)])
    gain_t = jnp.broadcast_to(gain_t[:, None], (MLA_HEAD_PAD, tm))
    blk = (_nbytes((tm, U_MLA), F32) + _nbytes(wq_t.shape, BF16) + _nbytes((qw, tm), BF16)
           + _nbytes((MLA_HEAD_PAD, tm), F32) + 2 * _nbytes((MLA_ROPE // 2, tm), F32))
    q_t = pl.pallas_call(
        _mla_q_kernel,
        grid=(t // tm,),
        in_specs=[
            pl.BlockSpec((tm, U_MLA), lambda i: (i, 0)),
            pl.BlockSpec((1, MLA_Q_RANK), lambda i: (0, 0)),
            pl.BlockSpec(wq_t.shape, lambda i: (0, 0)),
            pl.BlockSpec((MLA_HEAD_PAD, tm), lambda i: (0, 0)),
            tab_t_spec, tab_t_spec,
        ],
        out_specs=pl.BlockSpec((qw, tm), lambda i: (0, i)),
        out_shape=jax.ShapeDtypeStruct((qw, t), BF16),
        compiler_params=_cparams(("parallel",), blk, 3 * _nbytes((qw, tm), F32)),
        name="mla_q_prep",
    )(u, q_a_norm.reshape(1, -1), wq_t, gain_t, cos_t, sin_t)

    wkv = w_ukv.reshape(MLA_KV_RANK, MLA_HEADS, MLA_NOPE + MLA_V)
    wk = wkv[:, :, :MLA_NOPE].reshape(MLA_KV_RANK, -1).astype(BF16)
    wv_t = wkv[:, :, MLA_NOPE:].reshape(MLA_KV_RANK, -1).T.astype(BF16)
    blk = (_nbytes((tm, U_MLA), F32) + _nbytes(wk.shape, BF16) + _nbytes(wv_t.shape, BF16) + _nbytes((tm, qw), BF16)
           + _nbytes((vw, tm), BF16) + 3 * _nbytes((tm, ROPE_PAD), F32))
    k, v_t = pl.pallas_call(
        _mla_kv_kernel,
        grid=(t // tm,),
        in_specs=[
            pl.BlockSpec((tm, U_MLA), lambda i: (i, 0)),
            pl.BlockSpec((1, MLA_KV_RANK), lambda i: (0, 0)),
            pl.BlockSpec(wk.shape, lambda i: (0, 0)),
            pl.BlockSpec(wv_t.shape, lambda i: (0, 0)),
            pl.BlockSpec((1, MLA_NOPE), lambda i: (0, 0)),
            pl.BlockSpec((1, ROPE_PAD), lambda i: (0, 0)),
            tab_spec, tab_spec, tab_spec,
        ],
        out_specs=[pl.BlockSpec((tm, qw), lambda i: (i, 0)), pl.BlockSpec((None, vw, tm), lambda i: (i, 0, 0))],
        out_shape=[jax.ShapeDtypeStruct((t, qw), BF16), jax.ShapeDtypeStruct((t // tm, vw, tm), BF16)],
        compiler_params=_cparams(("parallel",), blk, 3 * _nbytes((tm, qw), F32)),
        name="mla_kv_prep",
    )(u, kv_a_norm.reshape(1, -1), wk, wv_t, k_norm[:MLA_NOPE].reshape(1, -1),
      jnp.concatenate([k_norm[MLA_NOPE:], jnp.zeros((ROPE_PAD - MLA_ROPE,), F32)]).reshape(1, -1), cos, s1, s2)
    return q_t, k, v_t


def _mla_attn_kernel(qt_ref, k_ref, vt_ref, o_ref, ssq_ref, sa_ref, sb_ref, *, tq, kb):
    per = tq // kb
    nq = qt_ref.shape[1] // tq
    nh = qt_ref.shape[0] // MLA_HEAD_PAD

    def scores(qi, blk, s_ref):
        for hh in range(nh):
            qt = qt_ref[hh * MLA_HEAD_PAD:(hh + 1) * MLA_HEAD_PAD, qi * tq:(qi + 1) * tq]
            for d in range(per):
                start = (blk * per + d) * kb
                kblk = k_ref[start:start + kb, hh * MLA_HEAD_PAD:(hh + 1) * MLA_HEAD_PAD]
                s = jnp.dot(kblk, qt, preferred_element_type=F32)
                if blk == qi:
                    key = d * kb + lax.broadcasted_iota(jnp.int32, s.shape, 0)
                    qry = lax.broadcasted_iota(jnp.int32, s.shape, 1)
                    s = jnp.where(key <= qry, s, NEG)
                s_ref[hh, d * kb:(d + 1) * kb, :] = s

    def softmax_pv(blk, s_ref, carries):
        out = []
        for hh in range(nh):
            m, l, acc = carries[hh]
            ss = [s_ref[hh, d * kb:(d + 1) * kb, :] for d in range(per)]
            m_new = m
            for s in ss:
                m_new = jnp.maximum(m_new, jnp.max(s, axis=0, keepdims=True))
            alpha = jnp.exp2(m - m_new)
            l = alpha * l
            acc = alpha * acc
            for d, s in enumerate(ss):
                p = jnp.exp2(s - m_new)
                l = l + jnp.sum(p, axis=0, keepdims=True)
                vt = vt_ref[blk * per + d, hh * MLA_V:(hh + 1) * MLA_V, :]
                acc = acc + jnp.dot(vt, p.astype(BF16), preferred_element_type=F32)
            out.append((m_new, l, acc))
        return tuple(out)

    items = [(qi, blk) for qi in range(nq) for blk in range(qi + 1)]
    bufs = (sa_ref, sb_ref)
    scores(*items[0], bufs[0])
    carries = None
    for idx, (qi, blk) in enumerate(items):
        if idx + 1 < len(items):
            scores(*items[idx + 1], bufs[(idx + 1) % 2])
        if blk == 0:
            carries = tuple((jnp.full((1, tq), NEG, F32), jnp.zeros((1, tq), F32), jnp.zeros((MLA_V, tq), F32))
                            for _ in range(nh))
        carries = softmax_pv(blk, bufs[idx % 2], carries)
        if blk == qi:
            ssq = None
            for hh in range(nh):
                _, l, acc = carries[hh]
                o = (acc * (1.0 / l)).T
                o_ref[qi * tq:(qi + 1) * tq, hh * MLA_V:(hh + 1) * MLA_V] = o.astype(o_ref.dtype)
                ssq = _row_ssq(o) if ssq is None else ssq + _row_ssq(o)
            ssq_ref[qi * tq:(qi + 1) * tq, :] = ssq


def _mla_attention(q_t, k, v_t, batch, seq):
    t = k.shape[0]
    kb = v_t.shape[2]
    tq = _tile(seq, 512)
    nkb = seq // kb
    nh = 2
    blk = nh * (2 * _nbytes((seq, MLA_HEAD_PAD), BF16) + _nbytes((seq, MLA_V), BF16) + _nbytes((seq, MLA_V), F32))
    return pl.pallas_call(
        functools.partial(_mla_attn_kernel, tq=tq, kb=kb),
        grid=(batch, MLA_HEADS // nh),
        in_specs=[
            pl.BlockSpec((nh * MLA_HEAD_PAD, seq), lambda b, h: (h, b)),
            pl.BlockSpec((seq, nh * MLA_HEAD_PAD), lambda b, h: (b, h)),
            pl.BlockSpec((nkb, nh * MLA_V, kb), lambda b, h: (b, h, 0)),
        ],
        out_specs=[pl.BlockSpec((seq, nh * MLA_V), lambda b, h: (b, h)), pl.BlockSpec((seq, LANES), lambda b, h: (b, h))],
        out_shape=[jax.ShapeDtypeStruct((t, MLA_WIDTH), BF16),
                   jax.ShapeDtypeStruct((t, (MLA_HEADS // nh) * LANES), F32)],
        scratch_shapes=[pltpu.VMEM((nh, tq, tq), F32), pltpu.VMEM((nh, tq, tq), F32)],
        compiler_params=_cparams(("parallel", "parallel"), blk, 10 * nh * _nbytes((tq, tq), F32)),
        name="mla_attention",
    )(q_t, k, v_t)


def _t5_bucket(dist):
    max_exact = REL_BUCKETS // 2
    d = jnp.maximum(dist, 1).astype(F32)
    large = max_exact + (jnp.log(d / max_exact) / math.log(REL_MAX_DIST / max_exact)
                         * (REL_BUCKETS - max_exact)).astype(jnp.int32)
    large = jnp.minimum(large, REL_BUCKETS - 1)
    return jnp.where(dist < max_exact, dist, large)


def _swa_kernel(sink_ref, q0_ref, q1_ref, q2_ref, q3_ref, kp_ref, kc_ref, vp_ref, vc_ref, gfold_ref, bias_ref,
                o_ref, ssq_ref):
    n = pl.program_id(1)
    nk, hd, kvw = 2 * BLOCK_Q, SWA_HEAD_DIM, SWA_KV_WIDTH
    key = lax.broadcasted_iota(jnp.int32, (nk, BLOCK_Q), 0)
    qry = lax.broadcasted_iota(jnp.int32, (nk, BLOCK_Q), 1)
    dist = qry + BLOCK_Q - key
    valid = (dist >= 0) & (dist < WINDOW) & ((key >= BLOCK_Q) | (n > 0))

    k2 = jnp.concatenate([kp_ref[...], kc_ref[...]], axis=0)
    sq = k2 * k2
    sq_hi = sq.astype(BF16)
    sq_lo = (sq - sq_hi.astype(F32)).astype(BF16)
    shift = hd.bit_length() - 1
    ri = lax.broadcasted_iota(jnp.int32, (kvw, kvw), 0) >> shift
    ci = lax.broadcasted_iota(jnp.int32, (kvw, kvw), 1) >> shift
    ind = jnp.where(ri == ci, 1.0, 0.0).astype(BF16)
    ss = jnp.dot(sq_hi, ind, preferred_element_type=F32) + jnp.dot(sq_lo, ind, preferred_element_type=F32)
    k2n = (k2 * lax.rsqrt(ss / float(hd) + EPS) * gfold_ref[...]).astype(BF16)
    v2t = jnp.concatenate([vp_ref[...], vc_ref[...]], axis=0).T.astype(BF16)

    qt = jnp.concatenate([q0_ref[...], q1_ref[...], q2_ref[...], q3_ref[...]], axis=1).T
    gw = SWA_GROUP * BLOCK_Q
    rows = []
    for kh in range(SWA_KV_HEADS):
        pieces = []
        if kh > 0:
            pieces.append(jnp.zeros((hd, kh * gw), BF16))
        for g in range(SWA_GROUP):
            j = kh * SWA_GROUP + g
            c = qt[j * hd:(j + 1) * hd, :]
            pieces.append((c * lax.rsqrt(jnp.sum(c * c, axis=0, keepdims=True) / float(hd) + EPS)).astype(BF16))
        if kh < SWA_KV_HEADS - 1:
            pieces.append(jnp.zeros((hd, (SWA_KV_HEADS - 1 - kh) * gw), BF16))
        rows.append(jnp.concatenate(pieces, axis=1))
    q_bd = jnp.concatenate(rows, axis=0)
    s_all = jnp.dot(k2n, q_bd, preferred_element_type=F32)

    es, invs = [], []
    for j in range(SWA_HEADS):
        lo = j * BLOCK_Q
        s = jnp.where(valid, s_all[:, lo:lo + BLOCK_Q] + bias_ref[:, lo:lo + BLOCK_Q], NEG)
        sink = sink_ref[j]
        m = jnp.maximum(jnp.max(s, axis=0, keepdims=True), sink)
        e = jnp.exp2(s - m)
        invs.append(1.0 / (jnp.sum(e, axis=0, keepdims=True) + jnp.exp2(sink - m)))
        es.append(e.astype(BF16))
    e_all = jnp.concatenate(es, axis=1)
    o_all = jnp.dot(v2t, e_all, preferred_element_type=F32)
    outs = []
    for j in range(SWA_HEADS):
        kh = j // SWA_GROUP
        outs.append(o_all[kh * hd:(kh + 1) * hd, j * BLOCK_Q:(j + 1) * BLOCK_Q] * invs[j])
    o = jnp.concatenate(outs, axis=0).T
    o_ref[...] = o.astype(o_ref.dtype)
    ssq_ref[...] = _row_ssq(o)


def _swa_attention(u, batch, seq, q_norm, k_norm, sinks, rel_bias):
    t = u.shape[0]
    nb = seq // BLOCK_Q
    nk = 2 * BLOCK_Q
    by_dist = rel_bias.astype(F32)[_t5_bucket(jnp.arange(WINDOW))].T
    row = jnp.concatenate([jnp.zeros((SWA_HEADS, 1), F32), by_dist], 1)
    bias = jnp.broadcast_to(row[:, None, :], (SWA_HEADS, nk, BLOCK_Q + 1)).reshape(SWA_HEADS, -1)
    bias = bias[:, :nk * BLOCK_Q].reshape(SWA_HEADS, nk, BLOCK_Q)
    bias = bias.transpose(1, 0, 2).reshape(nk, SWA_HEADS * BLOCK_Q) * LOG2E
    sinks = sinks * LOG2E
    gfold = jnp.tile(k_norm * q_norm * (SWA_HEAD_DIM ** -0.5 * LOG2E), SWA_KV_HEADS).reshape(1, SWA_KV_WIDTH)
    cur = lambda c: pl.BlockSpec((BLOCK_Q, SWA_KV_WIDTH), lambda b, n: (b * nb + n, c))
    prev = lambda c: pl.BlockSpec((BLOCK_Q, SWA_KV_WIDTH), lambda b, n: (b * nb + jnp.maximum(n - 1, 0), c))
    ck, cv = (U_MLA + S_K) // SWA_KV_WIDTH, (U_MLA + S_V) // SWA_KV_WIDTH
    gw = SWA_GROUP * SWA_HEAD_DIM
    q_specs = [pl.BlockSpec((BLOCK_Q, gw), functools.partial(lambda b, n, c: (b * nb + n, c), c=U_MLA // gw + kh))
               for kh in range(SWA_KV_HEADS)]
    blk = (_nbytes((BLOCK_Q, SWA_WIDTH), F32) + _nbytes((BLOCK_Q, SWA_WIDTH), BF16) + _nbytes((BLOCK_Q, LANES), F32)
           + 4 * _nbytes((BLOCK_Q, SWA_KV_WIDTH), F32) + _nbytes(bias.shape, F32))
    return pl.pallas_call(
        _swa_kernel,
        grid=(batch, nb),
        in_specs=[pl.BlockSpec(memory_space=pltpu.SMEM)] + q_specs + [
            prev(ck), cur(ck), prev(cv), cur(cv),
            pl.BlockSpec((1, SWA_KV_WIDTH), lambda b, n: (0, 0)),
            pl.BlockSpec(bias.shape, lambda b, n: (0, 0)),
        ],
        out_specs=[pl.BlockSpec((BLOCK_Q, SWA_WIDTH), lambda b, n: (b * nb + n, 0)),
                   pl.BlockSpec((BLOCK_Q, LANES), lambda b, n: (b * nb + n, 0))],
        out_shape=[jax.ShapeDtypeStruct((t, SWA_WIDTH), BF16), jax.ShapeDtypeStruct((t, LANES), F32)],
        compiler_params=_cparams(("parallel", "arbitrary"), blk, 4 * _nbytes((nk, SWA_HEADS * BLOCK_Q), F32)),
        name="swa_attention",
    )(sinks, u, u, u, u, u, u, u, u, gfold, bias)


def _mem_block_kernel(h_ref, kv_ref, gn_ref, wq_ref, wo_ref, gq_ref, gk_ref, gnext_ref, h_out_ref, xn_out_ref):
    h = h_ref[...]
    xn = _rms_rows(h, gn_ref[...]).astype(BF16)
    qm = jnp.dot(xn, wq_ref[...], preferred_element_type=F32)
    gq = gq_ref[...] * (MEM_HEAD_DIM ** -0.5)
    gk = gk_ref[...]
    outs = []
    for hd in range(MEM_HEADS):
        q = _rms_rows(qm[:, hd * MEM_HEAD_DIM:(hd + 1) * MEM_HEAD_DIM], gq).astype(BF16)
        lo = 2 * hd * MEM_HEAD_DIM
        k = _rms_rows(kv_ref[:, lo:lo + MEM_HEAD_DIM], gk).astype(BF16)
        v = kv_ref[:, lo + MEM_HEAD_DIM:lo + 2 * MEM_HEAD_DIM].astype(BF16)
        s = lax.dot_general(q, k, (((1,), (1,)), ((), ())), preferred_element_type=F32)
        e = jnp.exp(s - jnp.max(s, axis=-1, keepdims=True))
        p = (e / jnp.sum(e, axis=-1, keepdims=True)).astype(BF16)
        outs.append(jnp.dot(p, v, preferred_element_type=F32).astype(BF16))
    o = jnp.concatenate(outs, axis=-1)
    h_new = h + jnp.dot(o, wo_ref[...], preferred_element_type=F32)
    h_out_ref[...] = h_new
    xn_out_ref[...] = _rms_rows(h_new, gnext_ref[...]).astype(xn_out_ref.dtype)


def _mem_block(h, kvm, batch, seq, mem_len, norm_g, w_q, w_o, q_norm, k_norm, next_norm_g):
    t, d = h.shape
    w = w_q.shape[1]
    tm = _tile(seq, 256)
    nq = seq // tm
    once = pl.Buffered(1)
    blk = 2 * _nbytes((tm, d), F32) + _nbytes((tm, d), BF16) + _nbytes((mem_len, 2 * w), F32)
    temp = 2 * _nbytes((d, w), BF16) + 3 * _nbytes((tm, d), F32)
    return pl.pallas_call(
        _mem_block_kernel,
        grid=(batch, nq),
        in_specs=[
            pl.BlockSpec((tm, d), lambda b, i: (b * nq + i, 0)),
            pl.BlockSpec((mem_len, 2 * w), lambda b, i: (b, 0)),
            pl.BlockSpec((1, d), lambda b, i: (0, 0)),
            pl.BlockSpec((d, w), lambda b, i: (0, 0), pipeline_mode=once),
            pl.BlockSpec((w, d), lambda b, i: (0, 0), pipeline_mode=once),
            pl.BlockSpec((1, MEM_HEAD_DIM), lambda b, i: (0, 0)),
            pl.BlockSpec((1, MEM_HEAD_DIM), lambda b, i: (0, 0)),
            pl.BlockSpec((1, d), lambda b, i: (0, 0)),
        ],
        out_specs=[pl.BlockSpec((tm, d), lambda b, i: (b * nq + i, 0)), pl.BlockSpec((tm, d), lambda b, i: (b * nq + i, 0))],
        out_shape=[jax.ShapeDtypeStruct((t, d), F32), jax.ShapeDtypeStruct((t, d), BF16)],
        compiler_params=_cparams(("parallel", "arbitrary"), blk, temp),
        name="mem_block",
    )(h, kvm, norm_g.reshape(1, d), w_q.astype(BF16), w_o.astype(BF16), q_norm.reshape(1, -1), k_norm.reshape(1, -1),
      next_norm_g.reshape(1, d))


def _layer(h, mem, rel_bias, p, batch, seq):
    h, hb, ssq = _swiglu_half(h, _norm(h, p['ffn_a_norm']), p['ffn_a_gate'], p['ffn_a_up'], p['ffn_a_down'],
                              stats=True)

    w_in_t = p['w_in'].T
    u = _proj_t(hb, ssq, p['mix_norm'], w_in_t, n_first=U_MLA, second_start=IN_SWA,
                n_second=w_in_t.shape[0] - IN_SWA, tm=2048, tn=256, name="in_proj")

    q_t, k, v_t = _mla_prep(u, seq, p['mla_q_a_norm'], p['mla_kv_a_norm'], p['mla_w_uq'], p['mla_w_ukv'],
                            p['mla_q_norm'], p['mla_k_norm'])
    o_a, ssq_a = _mla_attention(q_t, k, v_t, batch, seq)
    ssq_a = ssq_a.reshape(ssq_a.shape[0], -1, LANES).sum(axis=1)
    o_b, ssq_b = _swa_attention(u, batch, seq, p['swa_q_norm'], p['swa_k_norm'], p['swa_sinks'], rel_bias)
    h = _proj2(o_a, o_b, ssq_a, ssq_b, p['out_norm_mla'], p['out_norm_swa'], p['w_out'], h,
               tm=2048, tn=256, name="out_proj")

    mem_len = mem.shape[0] // batch
    kvm = _mm_w32(_norm(mem, p['mem_norm']), p['mem_w_kv'], tm=512, tn=512, name="mem_kv_proj")
    h, xn = _mem_block(h, kvm, batch, seq, mem_len, p['mem_attn_norm'], p['mem_w_q'], p['mem_w_o'],
                       p['mem_q_norm'], p['mem_k_norm'], p['ffn_b_norm'])

    return _swiglu_half(h, xn, p['ffn_b_gate'], p['ffn_b_up'], p['ffn_b_down'])


_PARAM_NAMES = (
    'ffn_a_norm', 'ffn_a_gate', 'ffn_a_up', 'ffn_a_down', 'mix_norm', 'w_in',
    'mla_q_a_norm', 'mla_kv_a_norm', 'mla_w_uq', 'mla_w_ukv', 'mla_q_norm', 'mla_k_norm',
    'swa_q_norm', 'swa_k_norm', 'swa_sinks', 'out_norm_mla', 'out_norm_swa', 'w_out',
    'mem_attn_norm', 'mem_norm', 'mem_w_q', 'mem_w_kv', 'mem_q_norm', 'mem_k_norm', 'mem_w_o',
    'ffn_b_norm', 'ffn_b_gate', 'ffn_b_up', 'ffn_b_down',
)


def kernel(x, mem, rel_bias, ffn_a_norm, ffn_a_gate, ffn_a_up, ffn_a_down, mix_norm, w_in, mla_q_a_norm, mla_kv_a_norm, mla_w_uq, mla_w_ukv, mla_q_norm, mla_k_norm, swa_q_norm, swa_k_norm, swa_sinks, out_norm_mla, out_norm_swa, w_out, mem_attn_norm, mem_norm, mem_w_q, mem_w_kv, mem_q_norm, mem_k_norm, mem_w_o, ffn_b_norm, ffn_b_gate, ffn_b_up, ffn_b_down):
    stacked = (ffn_a_norm, ffn_a_gate, ffn_a_up, ffn_a_down, mix_norm, w_in, mla_q_a_norm, mla_kv_a_norm,
               mla_w_uq, mla_w_ukv, mla_q_norm, mla_k_norm, swa_q_norm, swa_k_norm, swa_sinks, out_norm_mla,
               out_norm_swa, w_out, mem_attn_norm, mem_norm, mem_w_q, mem_w_kv, mem_q_norm, mem_k_norm, mem_w_o,
               ffn_b_norm, ffn_b_gate, ffn_b_up, ffn_b_down)
    batch, seq, d = x.shape
    h = x.reshape(batch * seq, d)
    mem2 = mem.reshape(-1, d)
    for layer in range(ffn_a_norm.shape[0]):
        p = {name: arr[layer] for name, arr in zip(_PARAM_NAMES, stacked)}
        h = _layer(h, mem2, rel_bias, p, batch, seq)
    return h.reshape(batch, seq, d)
```

```python
import functools
import math

import jax
import jax.numpy as jnp
from jax import lax
from jax.experimental import pallas as pl
from jax.experimental.pallas import tpu as pltpu

F32 = jnp.float32
BF16 = jnp.bfloat16

MEM_HEADS = 4
MEM_HEAD_DIM = 128
MLA_HEADS = 16
MLA_Q_RANK = 896
MLA_KV_RANK = 512
MLA_NOPE = 128
MLA_ROPE = 64
MLA_V = 128
ROPE_THETA = 10000.0
SWA_HEADS = 32
SWA_KV_HEADS = 4
SWA_HEAD_DIM = 64
WINDOW = 128
REL_BUCKETS = 32
REL_MAX_DIST = 128
BLOCK_Q = 128
EPS = 1e-6
NEG = -1e30

MLA_QK = MLA_NOPE + MLA_ROPE
SWA_GROUP = SWA_HEADS // SWA_KV_HEADS
SWA_WIDTH = SWA_HEADS * SWA_HEAD_DIM
SWA_KV_WIDTH = SWA_KV_HEADS * SWA_HEAD_DIM
MLA_WIDTH = MLA_HEADS * MLA_V

LANES = 128
MLA_HEAD_PAD = 2 * LANES
ROPE_PAD = LANES

U_CKV = MLA_Q_RANK
U_KR = U_CKV + MLA_KV_RANK
U_MLA = U_KR + ROPE_PAD
IN_SWA = U_KR + MLA_ROPE
S_K = SWA_WIDTH
S_V = S_K + SWA_KV_WIDTH

LOG2E = math.log2(math.e)
KEY_BLOCK = 256

V7X_SCOPED_VMEM_BYTES = 60000 * 1024


def _tile(n, pref, align=8):
    if n <= pref:
        return n
    t = (pref // align) * align
    while t >= align:
        if n % t == 0:
            return t
        t -= align
    return n


def _nbytes(shape, dtype):
    n = 1
    for s in shape:
        n *= s
    return n * jnp.dtype(dtype).itemsize


def _cparams(semantics, pipelined_bytes, temp_bytes=0):
    need = 2 * pipelined_bytes + temp_bytes + (4 << 20)
    return pltpu.CompilerParams(
        dimension_semantics=semantics,
        vmem_limit_bytes=int(min(max(need, 16 << 20), V7X_SCOPED_VMEM_BYTES)),
    )


def _rms_rows(x, g, n=None):
    n = x.shape[-1] if n is None else n
    r = lax.rsqrt(jnp.sum(x * x, axis=-1, keepdims=True) / float(n) + EPS)
    return x * r * g


def _norm_kernel(x_ref, g_ref, o_ref):
    o_ref[...] = _rms_rows(x_ref[...], g_ref[...]).astype(o_ref.dtype)


def _norm(x, g, out_dtype=BF16):
    rows, d = x.shape
    tr = _tile(rows, 512)
    blk = _nbytes((tr, d), F32) + _nbytes((tr, d), out_dtype)
    return pl.pallas_call(
        _norm_kernel,
        grid=(rows // tr,),
        in_specs=[pl.BlockSpec((tr, d), lambda i: (i, 0)), pl.BlockSpec((1, d), lambda i: (0, 0))],
        out_specs=pl.BlockSpec((tr, d), lambda i: (i, 0)),
        out_shape=jax.ShapeDtypeStruct((rows, d), out_dtype),
        compiler_params=_cparams(("parallel",), blk, _nbytes((tr, d), F32)),
        name="rmsnorm",
    )(x, g.reshape(1, d))


def _mm_kernel(a_ref, w_ref, o_ref):
    o_ref[...] = jnp.dot(a_ref[...], w_ref[...], preferred_element_type=F32).astype(o_ref.dtype)


def _mm_res_kernel(a_ref, w_ref, r_ref, o_ref):
    o_ref[...] = r_ref[...] + jnp.dot(a_ref[...], w_ref[...], preferred_element_type=F32)


def _mm(a, w, res=None, *, tm, tn, out_dtype=F32, name="matmul"):
    m, k = a.shape
    n = w.shape[1]
    tm, tn = _tile(m, tm), _tile(n, tn, LANES)
    blk = _nbytes((tm, k), a.dtype) + _nbytes((k, tn), w.dtype) + _nbytes((tm, tn), out_dtype)
    in_specs = [pl.BlockSpec((tm, k), lambda i, j: (i, 0)), pl.BlockSpec((k, tn), lambda i, j: (0, j))]
    args = [a, w]
    kern = _mm_kernel
    if res is not None:
        in_specs.append(pl.BlockSpec((tm, tn), lambda i, j: (i, j)))
        args.append(res)
        blk += _nbytes((tm, tn), F32)
        kern = _mm_res_kernel
    return pl.pallas_call(
        kern,
        grid=(m // tm, n // tn),
        in_specs=in_specs,
        out_specs=pl.BlockSpec((tm, tn), lambda i, j: (i, j)),
        out_shape=jax.ShapeDtypeStruct((m, n), out_dtype),
        compiler_params=_cparams(("parallel", "arbitrary"), blk, 2 * _nbytes((tm, tn), F32)),
        name=name,
    )(*args)


def _row_ssq(x):
    return jnp.broadcast_to(jnp.sum(x * x, axis=-1, keepdims=True), (x.shape[0], LANES))


def _inv_rms(ssq_block, n):
    parts = ssq_block.shape[1] // LANES
    total = ssq_block[:, :1] if parts == 1 else jnp.sum(ssq_block, axis=-1, keepdims=True) / float(LANES)
    return lax.rsqrt(total / float(n) + EPS)


def _mm_res_stats_kernel(a_ref, w_ref, r_ref, o_ref, ob_ref, ssq_ref):
    h = r_ref[...] + jnp.dot(a_ref[...], w_ref[...], preferred_element_type=F32)
    o_ref[...] = h
    ob_ref[...] = h.astype(ob_ref.dtype)
    part = _row_ssq(h)

    @pl.when(pl.program_id(1) == 0)
    def _():
        ssq_ref[...] = part

    @pl.when(pl.program_id(1) != 0)
    def _():
        ssq_ref[...] += part


def _mm_res_stats(a, w, res, *, tm, tn, name):
    m, k = a.shape
    n = w.shape[1]
    tm, tn = _tile(m, tm), _tile(n, tn, LANES)
    blk = (_nbytes((tm, k), a.dtype) + _nbytes((k, tn), w.dtype) + 2 * _nbytes((tm, tn), F32) + _nbytes((tm, tn), BF16)
           + _nbytes((tm, LANES), F32))
    return pl.pallas_call(
        _mm_res_stats_kernel,
        grid=(m // tm, n // tn),
        in_specs=[
            pl.BlockSpec((tm, k), lambda i, j: (i, 0)),
            pl.BlockSpec((k, tn), lambda i, j: (0, j)),
            pl.BlockSpec((tm, tn), lambda i, j: (i, j)),
        ],
        out_specs=[
            pl.BlockSpec((tm, tn), lambda i, j: (i, j)),
            pl.BlockSpec((tm, tn), lambda i, j: (i, j)),
            pl.BlockSpec((tm, LANES), lambda i, j: (i, 0)),
        ],
        out_shape=[jax.ShapeDtypeStruct((m, n), F32), jax.ShapeDtypeStruct((m, n), BF16),
                   jax.ShapeDtypeStruct((m, LANES), F32)],
        compiler_params=_cparams(("parallel", "arbitrary"), blk, 3 * _nbytes((tm, tn), F32)),
        name=name,
    )(a, w, res)


def _proj_t_kernel(a_ref, ssq_ref, g_ref, wt_ref, o_ref):
    wb = (wt_ref[...] * g_ref[...]).astype(BF16)
    acc = lax.dot_general(a_ref[...], wb, (((1,), (1,)), ((), ())), preferred_element_type=F32)
    o_ref[...] = acc * _inv_rms(ssq_ref[...], a_ref.shape[1])


def _proj_t(a, ssq, g, wt, *, n_first, second_start, n_second, tm, tn, name):
    m, k = a.shape
    tm = _tile(m, tm)
    t1 = n_first // tn
    nt = t1 + n_second // tn
    shift = second_start - n_first
    wt_map = lambda i, j: (pl.multiple_of(j * tn + jnp.where(j >= t1, shift, 0), 8), 0)
    blk = _nbytes((tn, k), F32) + _nbytes((tm, tn), F32) + _nbytes((tm, LANES), F32)
    return pl.pallas_call(
        _proj_t_kernel,
        grid=(m // tm, nt),
        in_specs=[
            pl.BlockSpec((tm, k), lambda i, j: (i, 0), pipeline_mode=pl.Buffered(1)),
            pl.BlockSpec((tm, LANES), lambda i, j: (i, 0)),
            pl.BlockSpec((1, k), lambda i, j: (0, 0)),
            pl.BlockSpec((pl.Element(tn), pl.Element(k)), wt_map),
        ],
        out_specs=pl.BlockSpec((tm, tn), lambda i, j: (i, j)),
        out_shape=jax.ShapeDtypeStruct((m, nt * tn), F32),
        compiler_params=_cparams(("parallel", "arbitrary"), blk,
                                 _nbytes((tm, k), BF16) + _nbytes((tn, k), BF16) + 2 * _nbytes((tm, tn), F32)),
        name=name,
    )(a, ssq, g.reshape(1, k), wt)


def _proj2_kernel(a_ref, b_ref, ssqa_ref, ssqb_ref, gt_ref, w_ref, r_ref, o_ref):
    ka = a_ref.shape[1]
    wb = (w_ref[...] * gt_ref[...]).astype(BF16)
    ya = jnp.dot(a_ref[...], wb[:ka], preferred_element_type=F32) * _inv_rms(ssqa_ref[...], ka)
    yb = jnp.dot(b_ref[...], wb[ka:], preferred_element_type=F32) * _inv_rms(ssqb_ref[...], b_ref.shape[1])
    o_ref[...] = r_ref[...] + ya + yb


def _proj2(a, b, ssqa, ssqb, ga, gb, w, res, *, tm, tn, name):
    m, ka = a.shape
    kb = b.shape[1]
    n = w.shape[1]
    tm, tn = _tile(m, tm), _tile(n, tn, LANES)
    gtab = jnp.broadcast_to(jnp.concatenate([ga, gb])[:, None], (ka + kb, tn))
    once = pl.Buffered(1)
    blk = _nbytes((ka + kb, tn), F32) * 2 + 2 * _nbytes((tm, tn), F32) + _nbytes((tm, ssqa.shape[1] + LANES), F32)
    return pl.pallas_call(
        _proj2_kernel,
        grid=(m // tm, n // tn),
        in_specs=[
            pl.BlockSpec((tm, ka), lambda i, j: (i, 0), pipeline_mode=once),
            pl.BlockSpec((tm, kb), lambda i, j: (i, 0), pipeline_mode=once),
            pl.BlockSpec((tm, ssqa.shape[1]), lambda i, j: (i, 0)),
            pl.BlockSpec((tm, LANES), lambda i, j: (i, 0)),
            pl.BlockSpec((ka + kb, tn), lambda i, j: (0, 0), pipeline_mode=once),
            pl.BlockSpec((ka + kb, tn), lambda i, j: (0, j)),
            pl.BlockSpec((tm, tn), lambda i, j: (i, j)),
        ],
        out_specs=pl.BlockSpec((tm, tn), lambda i, j: (i, j)),
        out_shape=jax.ShapeDtypeStruct((m, n), F32),
        compiler_params=_cparams(("parallel", "arbitrary"), blk,
                                 _nbytes((tm, ka + kb), BF16) + _nbytes((ka + kb, tn), BF16) + 3 * _nbytes((tm, tn), F32)),
        name=name,
    )(a, b, ssqa, ssqb, gtab, w, res)


def _mm_w32_kernel(a_ref, w_ref, *rest, has_res):
    if has_res:
        r_ref, o_ref, wb_ref = rest
    else:
        o_ref, wb_ref = rest

    @pl.when(pl.program_id(1) == 0)
    def _():
        wb_ref[...] = w_ref[...].astype(BF16)

    acc = jnp.dot(a_ref[...], wb_ref[...], preferred_element_type=F32)
    if has_res:
        acc = r_ref[...] + acc
    o_ref[...] = acc.astype(o_ref.dtype)


def _mm_w32(a, w, res=None, *, tm, tn, n=None, out_dtype=F32, name="matmul"):
    m, k = a.shape
    n = w.shape[1] if n is None else n
    tm, tn = _tile(m, tm), _tile(n, tn, LANES)
    blk = _nbytes((tm, k), a.dtype) + _nbytes((k, tn), F32) + _nbytes((tm, tn), out_dtype)
    in_specs = [pl.BlockSpec((tm, k), lambda j, i: (i, 0)), pl.BlockSpec((k, tn), lambda j, i: (0, j))]
    args = [a, w]
    if res is not None:
        in_specs.append(pl.BlockSpec((tm, tn), lambda j, i: (i, j)))
        args.append(res)
        blk += _nbytes((tm, tn), F32)
    return pl.pallas_call(
        functools.partial(_mm_w32_kernel, has_res=res is not None),
        grid=(n // tn, m // tm),
        in_specs=in_specs,
        out_specs=pl.BlockSpec((tm, tn), lambda j, i: (i, j)),
        out_shape=jax.ShapeDtypeStruct((m, n), out_dtype),
        scratch_shapes=[pltpu.VMEM((k, tn), BF16)],
        compiler_params=_cparams(("parallel", "arbitrary"), blk, _nbytes((k, tn), BF16) + 2 * _nbytes((tm, tn), F32)),
        name=name,
    )(*args)


def _gated_kernel(x_ref, wg_ref, wu_ref, wd_ref, o_ref, wdb_ref):
    @pl.when(pl.program_id(0) == 0)
    def _():
        wdb_ref[...] = wd_ref[...].astype(BF16)

    x = x_ref[...]
    g = jnp.dot(x, wg_ref[...].astype(BF16), preferred_element_type=F32)
    u = jnp.dot(x, wu_ref[...].astype(BF16), preferred_element_type=F32)
    o_ref[...] = (0.5 * (g / (1.0 + jnp.exp(-g))) * u).astype(o_ref.dtype)


def _gated(x, wg, wu, wd, *, tm, tn):
    m, k = x.shape
    n = wg.shape[1]
    d_out = wd.shape[1]
    tm, tn = _tile(m, tm), _tile(n, tn, LANES)
    nj = n // tn
    blk = (2 * _nbytes((k, tn), F32) + _nbytes((tm, tn), BF16)
           + _nbytes((tn, d_out), F32) + _nbytes((tn, d_out), BF16))
    wd_map = lambda i, j: (jnp.where(i == 0, j, nj - 1), 0)
    return pl.pallas_call(
        _gated_kernel,
        grid=(m // tm, nj),
        in_specs=[
            pl.BlockSpec((tm, k), lambda i, j: (i, 0), pipeline_mode=pl.Buffered(1)),
            pl.BlockSpec((k, tn), lambda i, j: (0, j)),
            pl.BlockSpec((k, tn), lambda i, j: (0, j)),
            pl.BlockSpec((tn, d_out), wd_map),
        ],
        out_specs=[pl.BlockSpec((tm, tn), lambda i, j: (i, j)), pl.BlockSpec((tn, d_out), wd_map)],
        out_shape=[jax.ShapeDtypeStruct((m, n), BF16), jax.ShapeDtypeStruct((n, d_out), BF16)],
        compiler_params=_cparams(("arbitrary", "arbitrary"), blk,
                                 _nbytes((tm, k), BF16) + 2 * _nbytes((k, tn), BF16) + 3 * _nbytes((tm, tn), F32)),
        name="ffn_gate_up",
    )(x, wg, wu, wd)


def _swiglu_half(h, xn, wg, wu, wd, stats=False):
    act, wd_bf = _gated(xn, wg, wu, wd, tm=2048, tn=256)
    if stats:
        return _mm_res_stats(act, wd_bf, h, tm=512, tn=512, name="ffn_down_stats")
    return _mm(act, wd_bf, h, tm=512, tn=512, name="ffn_down")


def _rope_chunk(y, cos, s1, s2):
    return y * cos + pltpu.roll(y, ROPE_PAD - MLA_ROPE // 2, 1) * s1 + pltpu.roll(y, MLA_ROPE // 2, 1) * s2


def _mla_q_kernel(u_ref, ga_ref, wt_ref, gt_ref, cos_ref, sin_ref, o_ref):
    cn = _rms_rows(u_ref[:, :MLA_Q_RANK], ga_ref[...]).astype(BF16)
    qt = lax.dot_general(wt_ref[...], cn, (((1,), (1,)), ((), ())), preferred_element_type=F32)
    cos, sin, gain = cos_ref[...], sin_ref[...], gt_ref[...]
    half = MLA_ROPE // 2
    zeros = jnp.zeros((MLA_HEAD_PAD - MLA_QK, qt.shape[1]), o_ref.dtype)
    for h in range(MLA_HEADS):
        lo = h * MLA_HEAD_PAD
        c = qt[lo:lo + MLA_HEAD_PAD, :]
        y = c * lax.rsqrt(jnp.sum(c * c, axis=0, keepdims=True) / float(MLA_QK) + EPS) * gain
        x1, x2 = y[MLA_NOPE:MLA_NOPE + half, :], y[MLA_NOPE + half:MLA_QK, :]
        o_ref[lo:lo + MLA_NOPE, :] = y[:MLA_NOPE, :].astype(o_ref.dtype)
        o_ref[lo + MLA_NOPE:lo + MLA_NOPE + half, :] = (x1 * cos - x2 * sin).astype(o_ref.dtype)
        o_ref[lo + MLA_NOPE + half:lo + MLA_QK, :] = (x2 * cos + x1 * sin).astype(o_ref.dtype)
        o_ref[lo + MLA_QK:lo + MLA_HEAD_PAD, :] = zeros


def _mla_kv_kernel(u_ref, ga_ref, wk_ref, wvt_ref, gn_ref, gr_ref, cos_ref, s1_ref, s2_ref, k_ref, vt_ref):
    cn = _rms_rows(u_ref[:, U_CKV:U_KR], ga_ref[...]).astype(BF16)
    kv = jnp.dot(cn, wk_ref[...], preferred_element_type=F32)
    vt = lax.dot_general(wvt_ref[...], cn, (((1,), (1,)), ((), ())), preferred_element_type=F32)
    vt_ref[...] = vt.astype(vt_ref.dtype)
    kr = u_ref[:, U_KR:U_MLA]
    kr = jnp.where(lax.broadcasted_iota(jnp.int32, kr.shape, 1) < MLA_ROPE, kr, 0.0)
    ss_r = jnp.sum(kr * kr, axis=-1, keepdims=True)
    cos, s1, s2 = cos_ref[...], s1_ref[...], s2_ref[...]
    gn, gr = gn_ref[...], gr_ref[...]
    for h in range(MLA_HEADS):
        kn = kv[:, h * MLA_NOPE:(h + 1) * MLA_NOPE]
        r = lax.rsqrt((jnp.sum(kn * kn, axis=-1, keepdims=True) + ss_r) / float(MLA_QK) + EPS)
        lo = h * MLA_HEAD_PAD
        k_ref[:, lo:lo + MLA_NOPE] = (kn * r * gn).astype(k_ref.dtype)
        k_ref[:, lo + MLA_NOPE:lo + MLA_HEAD_PAD] = _rope_chunk(kr * r * gr, cos, s1, s2).astype(k_ref.dtype)


def _mla_prep_kernel(u_ref, gaq_ref, wqt_ref, gt_ref, cost_ref, sint_ref, gakv_ref, wk_ref, wvt_ref, gn_ref, gr_ref,
                     cos_ref, s1_ref, s2_ref, qt_ref, k_ref, vt_ref):
    _mla_q_kernel(u_ref, gaq_ref, wqt_ref, gt_ref, cost_ref, sint_ref, qt_ref)
    _mla_kv_kernel(u_ref, gakv_ref, wk_ref, wvt_ref, gn_ref, gr_ref, cos_ref, s1_ref, s2_ref, k_ref, vt_ref)


def _rope_tables(seq):
    half = MLA_ROPE // 2
    inv = ROPE_THETA ** (-jnp.arange(half, dtype=F32) / half)
    ang = jnp.arange(seq).astype(F32)[:, None] * inv[None, :]
    cos, sin = jnp.cos(ang), jnp.sin(ang)
    z = jnp.zeros_like(cos)
    pad = jnp.zeros((seq, ROPE_PAD - MLA_ROPE), F32)
    return (jnp.concatenate([cos, cos, pad], 1), jnp.concatenate([-sin, z, pad], 1),
            jnp.concatenate([z, sin, pad], 1), cos.T, sin.T)


def _mla_prep(u, seq, q_a_norm, kv_a_norm, w_uq, w_ukv, q_norm, k_norm):
    t = u.shape[0]
    tm = _tile(seq, KEY_BLOCK)
    spb = seq // tm
    cos, s1, s2, cos_t, sin_t = _rope_tables(seq)
    tab_spec = pl.BlockSpec((tm, ROPE_PAD), lambda i: (i % spb, 0))
    tab_t_spec = pl.BlockSpec((MLA_ROPE // 2, tm), lambda i: (0, i % spb))
    qw = MLA_HEADS * MLA_HEAD_PAD
    vw = MLA_HEADS * MLA_V

    wq_t = jnp.pad(w_uq.reshape(MLA_Q_RANK, MLA_HEADS, MLA_QK), ((0, 0), (0, 0), (0, MLA_HEAD_PAD - MLA_QK)))
    wq_t = wq_t.reshape(MLA_Q_RANK, qw).T.astype(BF16)
    gain_t = jnp.concatenate([q_norm * (MLA_QK ** -0.5 * LOG2E), jnp.zeros((MLA_HEAD_PAD - MLA_QK,), F32)])
    gain_t = jnp.broadcast_to(gain_t[:, None], (MLA_HEAD_PAD, tm))
    wkv = w_ukv.reshape(MLA_KV_RANK, MLA_HEADS, MLA_NOPE + MLA_V)
    wk = wkv[:, :, :MLA_NOPE].reshape(MLA_KV_RANK, -1).astype(BF16)
    wv_t = wkv[:, :, MLA_NOPE:].reshape(MLA_KV_RANK, -1).T.astype(BF16)
    blk = (_nbytes((tm, U_MLA), F32) + _nbytes(wq_t.shape, BF16) + _nbytes((qw, tm), BF16)
           + _nbytes((MLA_HEAD_PAD, tm), F32) + 2 * _nbytes((MLA_ROPE // 2, tm), F32)
           + _nbytes(wk.shape, BF16) + _nbytes(wv_t.shape, BF16) + _nbytes((tm, qw), BF16)
           + _nbytes((vw, tm), BF16) + 3 * _nbytes((tm, ROPE_PAD), F32))
    const = lambda shape: pl.BlockSpec(shape, lambda i: (0, 0))
    return pl.pallas_call(
        _mla_prep_kernel,
        grid=(t // tm,),
        in_specs=[
            pl.BlockSpec((tm, U_MLA), lambda i: (i, 0)),
            const((1, MLA_Q_RANK)), const(wq_t.shape), const((MLA_HEAD_PAD, tm)), tab_t_spec, tab_t_spec,
            const((1, MLA_KV_RANK)), const(wk.shape), const(wv_t.shape), const((1, MLA_NOPE)), const((1, ROPE_PAD)),
            tab_spec, tab_spec, tab_spec,
        ],
        out_specs=[pl.BlockSpec((qw, tm), lambda i: (0, i)), pl.BlockSpec((tm, qw), lambda i: (i, 0)),
                   pl.BlockSpec((None, vw, tm), lambda i: (i, 0, 0))],
        out_shape=[jax.ShapeDtypeStruct((qw, t), BF16), jax.ShapeDtypeStruct((t, qw), BF16),
                   jax.ShapeDtypeStruct((t // tm, vw, tm), BF16)],
        compiler_params=_cparams(("parallel",), blk, 4 * _nbytes((qw, tm), F32)),
        name="mla_prep",
    )(u, q_a_norm.reshape(1, -1), wq_t, gain_t, cos_t, sin_t,
      kv_a_norm.reshape(1, -1), wk, wv_t, k_norm[:MLA_NOPE].reshape(1, -1),
      jnp.concatenate([k_norm[MLA_NOPE:], jnp.zeros((ROPE_PAD - MLA_ROPE,), F32)]).reshape(1, -1), cos, s1, s2)


def _mla_attn_kernel(qt_ref, k_ref, vt_ref, o_ref, ssq_ref, sa_ref, sb_ref, *, tq, kb):
    per = tq // kb
    nq = qt_ref.shape[1] // tq
    nh = qt_ref.shape[0] // MLA_HEAD_PAD

    def scores(qi, blk, s_ref):
        for hh in range(nh):
            qt = qt_ref[hh * MLA_HEAD_PAD:(hh + 1) * MLA_HEAD_PAD, qi * tq:(qi + 1) * tq]
            for d in range(per):
                start = (blk * per + d) * kb
                kblk = k_ref[start:start + kb, hh * MLA_HEAD_PAD:(hh + 1) * MLA_HEAD_PAD]
                s = jnp.dot(kblk, qt, preferred_element_type=F32)
                if blk == qi:
                    key = d * kb + lax.broadcasted_iota(jnp.int32, s.shape, 0)
                    qry = lax.broadcasted_iota(jnp.int32, s.shape, 1)
                    s = jnp.where(key <= qry, s, NEG)
                s_ref[hh, d * kb:(d + 1) * kb, :] = s

    def softmax_pv(blk, s_ref, carries):
        out = []
        for hh in range(nh):
            m, l, acc = carries[hh]
            ss = [s_ref[hh, d * kb:(d + 1) * kb, :] for d in range(per)]
            m_new = m
            for s in ss:
                m_new = jnp.maximum(m_new, jnp.max(s, axis=0, keepdims=True))
            alpha = jnp.exp2(m - m_new)
            l = alpha * l
            acc = alpha * acc
            for d, s in enumerate(ss):
                p = jnp.exp2(s - m_new)
                l = l + jnp.sum(p, axis=0, keepdims=True)
                vt = vt_ref[blk * per + d, hh * MLA_V:(hh + 1) * MLA_V, :]
                acc = acc + jnp.dot(vt, p.astype(BF16), preferred_element_type=F32)
            out.append((m_new, l, acc))
        return tuple(out)

    items = [(qi, blk) for qi in range(nq) for blk in range(qi + 1)]
    bufs = (sa_ref, sb_ref)
    scores(*items[0], bufs[0])
    carries = None
    for idx, (qi, blk) in enumerate(items):
        if idx + 1 < len(items):
            scores(*items[idx + 1], bufs[(idx + 1) % 2])
        if blk == 0:
            carries = tuple((jnp.full((1, tq), NEG, F32), jnp.zeros((1, tq), F32), jnp.zeros((MLA_V, tq), F32))
                            for _ in range(nh))
        carries = softmax_pv(blk, bufs[idx % 2], carries)
        if blk == qi:
            ssq = None
            for hh in range(nh):
                _, l, acc = carries[hh]
                o = (acc * (1.0 / l)).T
                o_ref[qi * tq:(qi + 1) * tq, hh * MLA_V:(hh + 1) * MLA_V] = o.astype(o_ref.dtype)
                ssq = _row_ssq(o) if ssq is None else ssq + _row_ssq(o)
            ssq_ref[qi * tq:(qi + 1) * tq, :] = ssq


def _mla_attention(q_t, k, v_t, batch, seq):
    t = k.shape[0]
    kb = v_t.shape[2]
    tq = _tile(seq, 512)
    nkb = seq // kb
    nh = 2
    blk = nh * (2 * _nbytes((seq, MLA_HEAD_PAD), BF16) + _nbytes((seq, MLA_V), BF16) + _nbytes((seq, MLA_V), F32))
    return pl.pallas_call(
        functools.partial(_mla_attn_kernel, tq=tq, kb=kb),
        grid=(batch, MLA_HEADS // nh),
        in_specs=[
            pl.BlockSpec((nh * MLA_HEAD_PAD, seq), lambda b, h: (h, b)),
            pl.BlockSpec((seq, nh * MLA_HEAD_PAD), lambda b, h: (b, h)),
            pl.BlockSpec((nkb, nh * MLA_V, kb), lambda b, h: (b, h, 0)),
        ],
        out_specs=[pl.BlockSpec((seq, nh * MLA_V), lambda b, h: (b, h)), pl.BlockSpec((seq, LANES), lambda b, h: (b, h))],
        out_shape=[jax.ShapeDtypeStruct((t, MLA_WIDTH), BF16),
                   jax.ShapeDtypeStruct((t, (MLA_HEADS // nh) * LANES), F32)],
        scratch_shapes=[pltpu.VMEM((nh, tq, tq), F32), pltpu.VMEM((nh, tq, tq), F32)],
        compiler_params=_cparams(("parallel", "parallel"), blk, 10 * nh * _nbytes((tq, tq), F32)),
        name="mla_attention",
    )(q_t, k, v_t)


def _t5_bucket(dist):
    max_exact = REL_BUCKETS // 2
    d = jnp.maximum(dist, 1).astype(F32)
    large = max_exact + (jnp.log(d / max_exact) / math.log(REL_MAX_DIST / max_exact)
                         * (REL_BUCKETS - max_exact)).astype(jnp.int32)
    large = jnp.minimum(large, REL_BUCKETS - 1)
    return jnp.where(dist < max_exact, dist, large)


def _swa_kernel(sink_ref, q0_ref, q1_ref, q2_ref, q3_ref, kp_ref, kc_ref, vp_ref, vc_ref, gfold_ref, bias_ref,
                o_ref, ssq_ref):
    n = pl.program_id(1)
    nk, hd, kvw = 2 * BLOCK_Q, SWA_HEAD_DIM, SWA_KV_WIDTH
    key = lax.broadcasted_iota(jnp.int32, (nk, BLOCK_Q), 0)
    qry = lax.broadcasted_iota(jnp.int32, (nk, BLOCK_Q), 1)
    dist = qry + BLOCK_Q - key
    valid = (dist >= 0) & (dist < WINDOW) & ((key >= BLOCK_Q) | (n > 0))

    k2 = jnp.concatenate([kp_ref[...], kc_ref[...]], axis=0)
    sq = k2 * k2
    sq_hi = sq.astype(BF16)
    sq_lo = (sq - sq_hi.astype(F32)).astype(BF16)
    shift = hd.bit_length() - 1
    ri = lax.broadcasted_iota(jnp.int32, (kvw, kvw), 0) >> shift
    ci = lax.broadcasted_iota(jnp.int32, (kvw, kvw), 1) >> shift
    ind = jnp.where(ri == ci, 1.0, 0.0).astype(BF16)
    ss = jnp.dot(sq_hi, ind, preferred_element_type=F32) + jnp.dot(sq_lo, ind, preferred_element_type=F32)
    k2n = (k2 * lax.rsqrt(ss / float(hd) + EPS) * gfold_ref[...]).astype(BF16)
    v2t = jnp.concatenate([vp_ref[...], vc_ref[...]], axis=0).T.astype(BF16)

    qt = jnp.concatenate([q0_ref[...], q1_ref[...], q2_ref[...], q3_ref[...]], axis=1).T
    gw = SWA_GROUP * BLOCK_Q
    rows = []
    for kh in range(SWA_KV_HEADS):
        pieces = []
        if kh > 0:
            pieces.append(jnp.zeros((hd, kh * gw), BF16))
        for g in range(SWA_GROUP):
            j = kh * SWA_GROUP + g
            c = qt[j * hd:(j + 1) * hd, :]
            pieces.append((c * lax.rsqrt(jnp.sum(c * c, axis=0, keepdims=True) / float(hd) + EPS)).astype(BF16))
        if kh < SWA_KV_HEADS - 1:
            pieces.append(jnp.zeros((hd, (SWA_KV_HEADS - 1 - kh) * gw), BF16))
        rows.append(jnp.concatenate(pieces, axis=1))
    q_bd = jnp.concatenate(rows, axis=0)
    s_all = jnp.dot(k2n, q_bd, preferred_element_type=F32)

    es, invs = [], []
    for j in range(SWA_HEADS):
        lo = j * BLOCK_Q
        s = jnp.where(valid, s_all[:, lo:lo + BLOCK_Q] + bias_ref[:, lo:lo + BLOCK_Q], NEG)
        sink = sink_ref[j]
        m = jnp.maximum(jnp.max(s, axis=0, keepdims=True), sink)
        e = jnp.exp2(s - m)
        invs.append(1.0 / (jnp.sum(e, axis=0, keepdims=True) + jnp.exp2(sink - m)))
        es.append(e.astype(BF16))
    e_all = jnp.concatenate(es, axis=1)
    o_all = jnp.dot(v2t, e_all, preferred_element_type=F32)
    outs = []
    for j in range(SWA_HEADS):
        kh = j // SWA_GROUP
        outs.append(o_all[kh * hd:(kh + 1) * hd, j * BLOCK_Q:(j + 1) * BLOCK_Q] * invs[j])
    o = jnp.concatenate(outs, axis=0).T
    o_ref[...] = o.astype(o_ref.dtype)
    ssq_ref[...] = _row_ssq(o)


def _swa_attention(u, batch, seq, q_norm, k_norm, sinks, rel_bias):
    t = u.shape[0]
    nb = seq // BLOCK_Q
    nk = 2 * BLOCK_Q
    by_dist = rel_bias.astype(F32)[_t5_bucket(jnp.arange(WINDOW))].T
    row = jnp.concatenate([jnp.zeros((SWA_HEADS, 1), F32), by_dist], 1)
    bias = jnp.broadcast_to(row[:, None, :], (SWA_HEADS, nk, BLOCK_Q + 1)).reshape(SWA_HEADS, -1)
    bias = bias[:, :nk * BLOCK_Q].reshape(SWA_HEADS, nk, BLOCK_Q)
    bias = bias.transpose(1, 0, 2).reshape(nk, SWA_HEADS * BLOCK_Q) * LOG2E
    sinks = sinks * LOG2E
    gfold = jnp.tile(k_norm * q_norm * (SWA_HEAD_DIM ** -0.5 * LOG2E), SWA_KV_HEADS).reshape(1, SWA_KV_WIDTH)
    cur = lambda c: pl.BlockSpec((BLOCK_Q, SWA_KV_WIDTH), lambda b, n: (b * nb + n, c))
    prev = lambda c: pl.BlockSpec((BLOCK_Q, SWA_KV_WIDTH), lambda b, n: (b * nb + jnp.maximum(n - 1, 0), c))
    ck, cv = (U_MLA + S_K) // SWA_KV_WIDTH, (U_MLA + S_V) // SWA_KV_WIDTH
    gw = SWA_GROUP * SWA_HEAD_DIM
    q_specs = [pl.BlockSpec((BLOCK_Q, gw), functools.partial(lambda b, n, c: (b * nb + n, c), c=U_MLA // gw + kh))
               for kh in range(SWA_KV_HEADS)]
    blk = (_nbytes((BLOCK_Q, SWA_WIDTH), F32) + _nbytes((BLOCK_Q, SWA_WIDTH), BF16) + _nbytes((BLOCK_Q, LANES), F32)
           + 4 * _nbytes((BLOCK_Q, SWA_KV_WIDTH), F32) + _nbytes(bias.shape, F32))
    return pl.pallas_call(
        _swa_kernel,
        grid=(batch, nb),
        in_specs=[pl.BlockSpec(memory_space=pltpu.SMEM)] + q_specs + [
            prev(ck), cur(ck), prev(cv), cur(cv),
            pl.BlockSpec((1, SWA_KV_WIDTH), lambda b, n: (0, 0)),
            pl.BlockSpec(bias.shape, lambda b, n: (0, 0)),
        ],
        out_specs=[pl.BlockSpec((BLOCK_Q, SWA_WIDTH), lambda b, n: (b * nb + n, 0)),
                   pl.BlockSpec((BLOCK_Q, LANES), lambda b, n: (b * nb + n, 0))],
        out_shape=[jax.ShapeDtypeStruct((t, SWA_WIDTH), BF16), jax.ShapeDtypeStruct((t, LANES), F32)],
        compiler_params=_cparams(("parallel", "arbitrary"), blk, 4 * _nbytes((nk, SWA_HEADS * BLOCK_Q), F32)),
        name="swa_attention",
    )(sinks, u, u, u, u, u, u, u, u, gfold, bias)


def _mem_block_kernel(h_ref, kv_ref, gn_ref, wq_ref, wo_ref, gq_ref, gk_ref, gnext_ref, h_out_ref, xn_out_ref):
    h = h_ref[...]
    xn = _rms_rows(h, gn_ref[...]).astype(BF16)
    qm = jnp.dot(xn, wq_ref[...], preferred_element_type=F32)
    gq = gq_ref[...] * (MEM_HEAD_DIM ** -0.5)
    gk = gk_ref[...]
    outs = []
    for hd in range(MEM_HEADS):
        q = _rms_rows(qm[:, hd * MEM_HEAD_DIM:(hd + 1) * MEM_HEAD_DIM], gq).astype(BF16)
        lo = 2 * hd * MEM_HEAD_DIM
        k = _rms_rows(kv_ref[:, lo:lo + MEM_HEAD_DIM], gk).astype(BF16)
        v = kv_ref[:, lo + MEM_HEAD_DIM:lo + 2 * MEM_HEAD_DIM].astype(BF16)
        s = lax.dot_general(q, k, (((1,), (1,)), ((), ())), preferred_element_type=F32)
        e = jnp.exp(s - jnp.max(s, axis=-1, keepdims=True))
        p = (e / jnp.sum(e, axis=-1, keepdims=True)).astype(BF16)
        outs.append(jnp.dot(p, v, preferred_element_type=F32).astype(BF16))
    o = jnp.concatenate(outs, axis=-1)
    h_new = h + jnp.dot(o, wo_ref[...], preferred_element_type=F32)
    h_out_ref[...] = h_new
    xn_out_ref[...] = _rms_rows(h_new, gnext_ref[...]).astype(xn_out_ref.dtype)


def _mem_block(h, kvm, batch, seq, mem_len, norm_g, w_q, w_o, q_norm, k_norm, next_norm_g):
    t, d = h.shape
    w = w_q.shape[1]
    tm = _tile(seq, 256)
    nq = seq // tm
    once = pl.Buffered(1)
    blk = 2 * _nbytes((tm, d), F32) + _nbytes((tm, d), BF16) + _nbytes((mem_len, 2 * w), F32)
    temp = 2 * _nbytes((d, w), BF16) + 3 * _nbytes((tm, d), F32)
    return pl.pallas_call(
        _mem_block_kernel,
        grid=(batch, nq),
        in_specs=[
            pl.BlockSpec((tm, d), lambda b, i: (b * nq + i, 0)),
            pl.BlockSpec((mem_len, 2 * w), lambda b, i: (b, 0)),
            pl.BlockSpec((1, d), lambda b, i: (0, 0)),
            pl.BlockSpec((d, w), lambda b, i: (0, 0), pipeline_mode=once),
            pl.BlockSpec((w, d), lambda b, i: (0, 0), pipeline_mode=once),
            pl.BlockSpec((1, MEM_HEAD_DIM), lambda b, i: (0, 0)),
            pl.BlockSpec((1, MEM_HEAD_DIM), lambda b, i: (0, 0)),
            pl.BlockSpec((1, d), lambda b, i: (0, 0)),
        ],
        out_specs=[pl.BlockSpec((tm, d), lambda b, i: (b * nq + i, 0)), pl.BlockSpec((tm, d), lambda b, i: (b * nq + i, 0))],
        out_shape=[jax.ShapeDtypeStruct((t, d), F32), jax.ShapeDtypeStruct((t, d), BF16)],
        compiler_params=_cparams(("parallel", "arbitrary"), blk, temp),
        name="mem_block",
    )(h, kvm, norm_g.reshape(1, d), w_q.astype(BF16), w_o.astype(BF16), q_norm.reshape(1, -1), k_norm.reshape(1, -1),
      next_norm_g.reshape(1, d))


def _layer(h, mem, rel_bias, p, batch, seq):
    h, hb, ssq = _swiglu_half(h, _norm(h, p['ffn_a_norm']), p['ffn_a_gate'], p['ffn_a_up'], p['ffn_a_down'],
                              stats=True)

    w_in_t = p['w_in'].T
    u = _proj_t(hb, ssq, p['mix_norm'], w_in_t, n_first=U_MLA, second_start=IN_SWA,
                n_second=w_in_t.shape[0] - IN_SWA, tm=2048, tn=512, name="in_proj")

    q_t, k, v_t = _mla_prep(u, seq, p['mla_q_a_norm'], p['mla_kv_a_norm'], p['mla_w_uq'], p['mla_w_ukv'],
                            p['mla_q_norm'], p['mla_k_norm'])
    o_a, ssq_a = _mla_attention(q_t, k, v_t, batch, seq)
    ssq_a = ssq_a.reshape(ssq_a.shape[0], -1, LANES).sum(axis=1)
    o_b, ssq_b = _swa_attention(u, batch, seq, p['swa_q_norm'], p['swa_k_norm'], p['swa_sinks'], rel_bias)
    h = _proj2(o_a, o_b, ssq_a, ssq_b, p['out_norm_mla'], p['out_norm_swa'], p['w_out'], h,
               tm=2048, tn=256, name="out_proj")

    mem_len = mem.shape[0] // batch
    kvm = _mm_w32(_norm(mem, p['mem_norm']), p['mem_w_kv'], tm=512, tn=512, name="mem_kv_proj")
    h, xn = _mem_block(h, kvm, batch, seq, mem_len, p['mem_attn_norm'], p['mem_w_q'], p['mem_w_o'],
                       p['mem_q_norm'], p['mem_k_norm'], p['ffn_b_norm'])

    return _swiglu_half(h, xn, p['ffn_b_gate'], p['ffn_b_up'], p['ffn_b_down'])


_PARAM_NAMES = (
    'ffn_a_norm', 'ffn_a_gate', 'ffn_a_up', 'ffn_a_down', 'mix_norm', 'w_in',
    'mla_q_a_norm', 'mla_kv_a_norm', 'mla_w_uq', 'mla_w_ukv', 'mla_q_norm', 'mla_k_norm',
    'swa_q_norm', 'swa_k_norm', 'swa_sinks', 'out_norm_mla', 'out_norm_swa', 'w_out',
    'mem_attn_norm', 'mem_norm', 'mem_w_q', 'mem_w_kv', 'mem_q_norm', 'mem_k_norm', 'mem_w_o',
    'ffn_b_norm', 'ffn_b_gate', 'ffn_b_up', 'ffn_b_down',
)


def kernel(x, mem, rel_bias, ffn_a_norm, ffn_a_gate, ffn_a_up, ffn_a_down, mix_norm, w_in, mla_q_a_norm, mla_kv_a_norm, mla_w_uq, mla_w_ukv, mla_q_norm, mla_k_norm, swa_q_norm, swa_k_norm, swa_sinks, out_norm_mla, out_norm_swa, w_out, mem_attn_norm, mem_norm, mem_w_q, mem_w_kv, mem_q_norm, mem_k_norm, mem_w_o, ffn_b_norm, ffn_b_gate, ffn_b_up, ffn_b_down):
    stacked = (ffn_a_norm, ffn_a_gate, ffn_a_up, ffn_a_down, mix_norm, w_in, mla_q_a_norm, mla_kv_a_norm,
               mla_w_uq, mla_w_ukv, mla_q_norm, mla_k_norm, swa_q_norm, swa_k_norm, swa_sinks, out_norm_mla,
               out_norm_swa, w_out, mem_attn_norm, mem_norm, mem_w_q, mem_w_kv, mem_q_norm, mem_k_norm, mem_w_o,
               ffn_b_norm, ffn_b_gate, ffn_b_up, ffn_b_down)
    batch, seq, d = x.shape
    h = x.reshape(batch * seq, d)
    mem2 = mem.reshape(-1, d)
    for layer in range(ffn_a_norm.shape[0]):
        p = {name: arr[layer] for name, arr in zip(_PARAM_NAMES, stacked)}
        h = _layer(h, mem2, rel_bias, p, batch, seq)
    return h.reshape(batch, seq, d)
```

```python
import functools
import math

import jax
import jax.numpy as jnp
from jax import lax
from jax.experimental import pallas as pl
from jax.experimental.pallas import tpu as pltpu

F32 = jnp.float32
BF16 = jnp.bfloat16

MEM_HEADS = 4
MEM_HEAD_DIM = 128
MLA_HEADS = 16
MLA_Q_RANK = 896
MLA_KV_RANK = 512
MLA_NOPE = 128
MLA_ROPE = 64
MLA_V = 128
ROPE_THETA = 10000.0
SWA_HEADS = 32
SWA_KV_HEADS = 4
SWA_HEAD_DIM = 64
WINDOW = 128
REL_BUCKETS = 32
REL_MAX_DIST = 128
BLOCK_Q = 128
EPS = 1e-6
NEG = -1e30

MLA_QK = MLA_NOPE + MLA_ROPE
SWA_GROUP = SWA_HEADS // SWA_KV_HEADS
SWA_WIDTH = SWA_HEADS * SWA_HEAD_DIM
SWA_KV_WIDTH = SWA_KV_HEADS * SWA_HEAD_DIM
MLA_WIDTH = MLA_HEADS * MLA_V

LANES = 128
MLA_HEAD_PAD = 2 * LANES
ROPE_PAD = LANES

U_CKV = MLA_Q_RANK
U_KR = U_CKV + MLA_KV_RANK
U_MLA = U_KR + ROPE_PAD
IN_SWA = U_KR + MLA_ROPE
S_K = SWA_WIDTH
S_V = S_K + SWA_KV_WIDTH

LOG2E = math.log2(math.e)
KEY_BLOCK = 256

V7X_SCOPED_VMEM_BYTES = 60000 * 1024


def _tile(n, pref, align=8):
    if n <= pref:
        return n
    t = (pref // align) * align
    while t >= align:
        if n % t == 0:
            return t
        t -= align
    return n


def _nbytes(shape, dtype):
    n = 1
    for s in shape:
        n *= s
    return n * jnp.dtype(dtype).itemsize


def _cparams(semantics, pipelined_bytes, temp_bytes=0):
    need = 2 * pipelined_bytes + temp_bytes + (4 << 20)
    return pltpu.CompilerParams(
        dimension_semantics=semantics,
        vmem_limit_bytes=int(min(max(need, 16 << 20), V7X_SCOPED_VMEM_BYTES)),
    )


def _rms_rows(x, g, n=None):
    n = x.shape[-1] if n is None else n
    r = lax.rsqrt(jnp.sum(x * x, axis=-1, keepdims=True) / float(n) + EPS)
    return x * r * g


def _norm_kernel(x_ref, g_ref, o_ref):
    o_ref[...] = _rms_rows(x_ref[...], g_ref[...]).astype(o_ref.dtype)


def _norm(x, g, out_dtype=BF16):
    rows, d = x.shape
    tr = _tile(rows, 512)
    blk = _nbytes((tr, d), F32) + _nbytes((tr, d), out_dtype)
    return pl.pallas_call(
        _norm_kernel,
        grid=(rows // tr,),
        in_specs=[pl.BlockSpec((tr, d), lambda i: (i, 0)), pl.BlockSpec((1, d), lambda i: (0, 0))],
        out_specs=pl.BlockSpec((tr, d), lambda i: (i, 0)),
        out_shape=jax.ShapeDtypeStruct((rows, d), out_dtype),
        compiler_params=_cparams(("parallel",), blk, _nbytes((tr, d), F32)),
        name="rmsnorm",
    )(x, g.reshape(1, d))


def _mm_kernel(a_ref, w_ref, o_ref):
    o_ref[...] = jnp.dot(a_ref[...], w_ref[...], preferred_element_type=F32).astype(o_ref.dtype)


def _mm_res_kernel(a_ref, w_ref, r_ref, o_ref):
    o_ref[...] = r_ref[...] + jnp.dot(a_ref[...], w_ref[...], preferred_element_type=F32)


def _mm(a, w, res=None, *, tm, tn, out_dtype=F32, name="matmul"):
    m, k = a.shape
    n = w.shape[1]
    tm, tn = _tile(m, tm), _tile(n, tn, LANES)
    blk = _nbytes((tm, k), a.dtype) + _nbytes((k, tn), w.dtype) + _nbytes((tm, tn), out_dtype)
    in_specs = [pl.BlockSpec((tm, k), lambda j, i: (i, 0)), pl.BlockSpec((k, tn), lambda j, i: (0, j))]
    args = [a, w]
    kern = _mm_kernel
    if res is not None:
        in_specs.append(pl.BlockSpec((tm, tn), lambda j, i: (i, j)))
        args.append(res)
        blk += _nbytes((tm, tn), F32)
        kern = _mm_res_kernel
    return pl.pallas_call(
        kern,
        grid=(n // tn, m // tm),
        in_specs=in_specs,
        out_specs=pl.BlockSpec((tm, tn), lambda j, i: (i, j)),
        out_shape=jax.ShapeDtypeStruct((m, n), out_dtype),
        compiler_params=_cparams(("parallel", "arbitrary"), blk, 2 * _nbytes((tm, tn), F32)),
        name=name,
    )(*args)


def _row_ssq(x):
    sq = x * x
    part = sq[:, :LANES]
    for c in range(1, x.shape[1] // LANES):
        part = part + sq[:, c * LANES:(c + 1) * LANES]
    return part


def _inv_rms(ssq_block, n):
    return lax.rsqrt(jnp.sum(ssq_block, axis=-1, keepdims=True) / float(n) + EPS)


def _mm_res_stats_kernel(a_ref, w_ref, r_ref, o_ref, ob_ref, ssq_ref):
    @pl.when(pl.program_id(1) == 0)
    def _():
        ssq_ref[...] = jnp.zeros_like(ssq_ref)

    h = r_ref[...] + jnp.dot(a_ref[...], w_ref[...], preferred_element_type=F32)
    o_ref[...] = h
    ob_ref[...] = h.astype(ob_ref.dtype)
    ssq_ref[...] += _row_ssq(h)


def _mm_res_stats(a, w, res, *, tm, tn, name):
    m, k = a.shape
    n = w.shape[1]
    tm, tn = _tile(m, tm), _tile(n, tn, LANES)
    blk = (_nbytes((tm, k), a.dtype) + _nbytes((k, tn), w.dtype) + 2 * _nbytes((tm, tn), F32) + _nbytes((tm, tn), BF16)
           + _nbytes((tm, LANES), F32))
    return pl.pallas_call(
        _mm_res_stats_kernel,
        grid=(m // tm, n // tn),
        in_specs=[
            pl.BlockSpec((tm, k), lambda i, j: (i, 0)),
            pl.BlockSpec((k, tn), lambda i, j: (0, j)),
            pl.BlockSpec((tm, tn), lambda i, j: (i, j)),
        ],
        out_specs=[
            pl.BlockSpec((tm, tn), lambda i, j: (i, j)),
            pl.BlockSpec((tm, tn), lambda i, j: (i, j)),
            pl.BlockSpec((tm, LANES), lambda i, j: (i, 0)),
        ],
        out_shape=[jax.ShapeDtypeStruct((m, n), F32), jax.ShapeDtypeStruct((m, n), BF16),
                   jax.ShapeDtypeStruct((m, LANES), F32)],
        compiler_params=_cparams(("parallel", "arbitrary"), blk, 3 * _nbytes((tm, tn), F32)),
        name=name,
    )(a, w, res)


def _proj_t_kernel(a_ref, ssq_ref, g_ref, wt_ref, o_ref):
    wb = (wt_ref[...] * g_ref[...]).astype(BF16)
    acc = lax.dot_general(a_ref[...], wb, (((1,), (1,)), ((), ())), preferred_element_type=F32)
    o_ref[...] = acc * _inv_rms(ssq_ref[...], a_ref.shape[1])


def _proj_t(a, ssq, g, wt, *, n_first, second_start, n_second, tm, tn, name):
    m, k = a.shape
    tm = _tile(m, tm)
    t1 = n_first // tn
    nt = t1 + n_second // tn
    shift = second_start - n_first
    wt_map = lambda i, j: (pl.multiple_of(j * tn + jnp.where(j >= t1, shift, 0), 8), 0)
    blk = _nbytes((tn, k), F32) + _nbytes((tm, tn), F32) + _nbytes((tm, LANES), F32)
    return pl.pallas_call(
        _proj_t_kernel,
        grid=(m // tm, nt),
        in_specs=[
            pl.BlockSpec((tm, k), lambda i, j: (i, 0), pipeline_mode=pl.Buffered(1)),
            pl.BlockSpec((tm, LANES), lambda i, j: (i, 0)),
            pl.BlockSpec((1, k), lambda i, j: (0, 0)),
            pl.BlockSpec((pl.Element(tn), pl.Element(k)), wt_map),
        ],
        out_specs=pl.BlockSpec((tm, tn), lambda i, j: (i, j)),
        out_shape=jax.ShapeDtypeStruct((m, nt * tn), F32),
        compiler_params=_cparams(("parallel", "arbitrary"), blk,
                                 _nbytes((tm, k), BF16) + _nbytes((tn, k), BF16) + 2 * _nbytes((tm, tn), F32)),
        name=name,
    )(a, ssq, g.reshape(1, k), wt)


def _proj2_kernel(a_ref, b_ref, ssqa_ref, ssqb_ref, gt_ref, w_ref, r_ref, o_ref):
    ka = a_ref.shape[1]
    wb = (w_ref[...] * gt_ref[...]).astype(BF16)
    ya = jnp.dot(a_ref[...], wb[:ka], preferred_element_type=F32) * _inv_rms(ssqa_ref[...], ka)
    yb = jnp.dot(b_ref[...], wb[ka:], preferred_element_type=F32) * _inv_rms(ssqb_ref[...], b_ref.shape[1])
    o_ref[...] = r_ref[...] + ya + yb


def _proj2(a, b, ssqa, ssqb, ga, gb, w, res, *, tm, tn, name):
    m, ka = a.shape
    kb = b.shape[1]
    n = w.shape[1]
    tm, tn = _tile(m, tm), _tile(n, tn, LANES)
    gtab = jnp.broadcast_to(jnp.concatenate([ga, gb])[:, None], (ka + kb, tn))
    once = pl.Buffered(1)
    blk = _nbytes((ka + kb, tn), F32) * 2 + 2 * _nbytes((tm, tn), F32) + _nbytes((tm, ssqa.shape[1] + LANES), F32)
    return pl.pallas_call(
        _proj2_kernel,
        grid=(m // tm, n // tn),
        in_specs=[
            pl.BlockSpec((tm, ka), lambda i, j: (i, 0), pipeline_mode=once),
            pl.BlockSpec((tm, kb), lambda i, j: (i, 0), pipeline_mode=once),
            pl.BlockSpec((tm, ssqa.shape[1]), lambda i, j: (i, 0)),
            pl.BlockSpec((tm, LANES), lambda i, j: (i, 0)),
            pl.BlockSpec((ka + kb, tn), lambda i, j: (0, 0), pipeline_mode=once),
            pl.BlockSpec((ka + kb, tn), lambda i, j: (0, j)),
            pl.BlockSpec((tm, tn), lambda i, j: (i, j)),
        ],
        out_specs=pl.BlockSpec((tm, tn), lambda i, j: (i, j)),
        out_shape=jax.ShapeDtypeStruct((m, n), F32),
        compiler_params=_cparams(("parallel", "arbitrary"), blk,
                                 _nbytes((tm, ka + kb), BF16) + _nbytes((ka + kb, tn), BF16) + 3 * _nbytes((tm, tn), F32)),
        name=name,
    )(a, b, ssqa, ssqb, gtab, w, res)


def _mm_w32_kernel(a_ref, w_ref, *rest, has_res):
    if has_res:
        r_ref, o_ref, wb_ref = rest
    else:
        o_ref, wb_ref = rest

    @pl.when(pl.program_id(1) == 0)
    def _():
        wb_ref[...] = w_ref[...].astype(BF16)

    acc = jnp.dot(a_ref[...], wb_ref[...], preferred_element_type=F32)
    if has_res:
        acc = r_ref[...] + acc
    o_ref[...] = acc.astype(o_ref.dtype)


def _mm_w32(a, w, res=None, *, tm, tn, n=None, out_dtype=F32, name="matmul"):
    m, k = a.shape
    n = w.shape[1] if n is None else n
    tm, tn = _tile(m, tm), _tile(n, tn, LANES)
    blk = _nbytes((tm, k), a.dtype) + _nbytes((k, tn), F32) + _nbytes((tm, tn), out_dtype)
    in_specs = [pl.BlockSpec((tm, k), lambda j, i: (i, 0)), pl.BlockSpec((k, tn), lambda j, i: (0, j))]
    args = [a, w]
    if res is not None:
        in_specs.append(pl.BlockSpec((tm, tn), lambda j, i: (i, j)))
        args.append(res)
        blk += _nbytes((tm, tn), F32)
    return pl.pallas_call(
        functools.partial(_mm_w32_kernel, has_res=res is not None),
        grid=(n // tn, m // tm),
        in_specs=in_specs,
        out_specs=pl.BlockSpec((tm, tn), lambda j, i: (i, j)),
        out_shape=jax.ShapeDtypeStruct((m, n), out_dtype),
        scratch_shapes=[pltpu.VMEM((k, tn), BF16)],
        compiler_params=_cparams(("parallel", "arbitrary"), blk, _nbytes((k, tn), BF16) + 2 * _nbytes((tm, tn), F32)),
        name=name,
    )(*args)


def _gated_kernel(x_ref, wg_ref, wu_ref, wd_ref, o_ref, wdb_ref):
    @pl.when(pl.program_id(0) == 0)
    def _():
        wdb_ref[...] = wd_ref[...].astype(BF16)

    x = x_ref[...]
    g = jnp.dot(x, wg_ref[...].astype(BF16), preferred_element_type=F32)
    u = jnp.dot(x, wu_ref[...].astype(BF16), preferred_element_type=F32)
    o_ref[...] = (0.5 * (g / (1.0 + jnp.exp(-g))) * u).astype(o_ref.dtype)


def _gated(x, wg, wu, wd, *, tm, tn):
    m, k = x.shape
    n = wg.shape[1]
    d_out = wd.shape[1]
    tm, tn = _tile(m, tm), _tile(n, tn, LANES)
    nj = n // tn
    blk = (2 * _nbytes((k, tn), F32) + _nbytes((tm, tn), BF16)
           + _nbytes((tn, d_out), F32) + _nbytes((tn, d_out), BF16))
    wd_map = lambda i, j: (jnp.where(i == 0, j, nj - 1), 0)
    return pl.pallas_call(
        _gated_kernel,
        grid=(m // tm, nj),
        in_specs=[
            pl.BlockSpec((tm, k), lambda i, j: (i, 0), pipeline_mode=pl.Buffered(1)),
            pl.BlockSpec((k, tn), lambda i, j: (0, j)),
            pl.BlockSpec((k, tn), lambda i, j: (0, j)),
            pl.BlockSpec((tn, d_out), wd_map),
        ],
        out_specs=[pl.BlockSpec((tm, tn), lambda i, j: (i, j)), pl.BlockSpec((tn, d_out), wd_map)],
        out_shape=[jax.ShapeDtypeStruct((m, n), BF16), jax.ShapeDtypeStruct((n, d_out), BF16)],
        compiler_params=_cparams(("arbitrary", "arbitrary"), blk,
                                 _nbytes((tm, k), BF16) + 2 * _nbytes((k, tn), BF16) + 3 * _nbytes((tm, tn), F32)),
        name="ffn_gate_up",
    )(x, wg, wu, wd)


def _swiglu_half(h, xn, wg, wu, wd, stats=False):
    act, wd_bf = _gated(xn, wg, wu, wd, tm=2048, tn=256)
    if stats:
        return _mm_res_stats(act, wd_bf, h, tm=512, tn=512, name="ffn_down_stats")
    return _mm(act, wd_bf, h, tm=512, tn=512, name="ffn_down")


def _rope_chunk(y, cos, s1, s2):
    return y * cos + pltpu.roll(y, ROPE_PAD - MLA_ROPE // 2, 1) * s1 + pltpu.roll(y, MLA_ROPE // 2, 1) * s2


def _mla_q_kernel(u_ref, ga_ref, wt_ref, gt_ref, cos_ref, sin_ref, o_ref):
    cn = _rms_rows(u_ref[:, :MLA_Q_RANK], ga_ref[...]).astype(BF16)
    qt = lax.dot_general(wt_ref[...], cn, (((1,), (1,)), ((), ())), preferred_element_type=F32)
    cos, sin, gain = cos_ref[...], sin_ref[...], gt_ref[...]
    half = MLA_ROPE // 2
    zeros = jnp.zeros((MLA_HEAD_PAD - MLA_QK, qt.shape[1]), o_ref.dtype)
    for h in range(MLA_HEADS):
        lo = h * MLA_HEAD_PAD
        c = qt[lo:lo + MLA_HEAD_PAD, :]
        y = c * lax.rsqrt(jnp.sum(c * c, axis=0, keepdims=True) / float(MLA_QK) + EPS) * gain
        x1, x2 = y[MLA_NOPE:MLA_NOPE + half, :], y[MLA_NOPE + half:MLA_QK, :]
        o_ref[lo:lo + MLA_NOPE, :] = y[:MLA_NOPE, :].astype(o_ref.dtype)
        o_ref[lo + MLA_NOPE:lo + MLA_NOPE + half, :] = (x1 * cos - x2 * sin).astype(o_ref.dtype)
        o_ref[lo + MLA_NOPE + half:lo + MLA_QK, :] = (x2 * cos + x1 * sin).astype(o_ref.dtype)
        o_ref[lo + MLA_QK:lo + MLA_HEAD_PAD, :] = zeros


def _mla_kv_kernel(u_ref, ga_ref, wk_ref, wvt_ref, gn_ref, gr_ref, cos_ref, s1_ref, s2_ref, k_ref, vt_ref):
    cn = _rms_rows(u_ref[:, U_CKV:U_KR], ga_ref[...]).astype(BF16)
    kv = jnp.dot(cn, wk_ref[...], preferred_element_type=F32)
    vt = lax.dot_general(wvt_ref[...], cn, (((1,), (1,)), ((), ())), preferred_element_type=F32)
    vt_ref[...] = vt.astype(vt_ref.dtype)
    kr = u_ref[:, U_KR:U_MLA]
    kr = jnp.where(lax.broadcasted_iota(jnp.int32, kr.shape, 1) < MLA_ROPE, kr, 0.0)
    ss_r = jnp.sum(kr * kr, axis=-1, keepdims=True)
    cos, s1, s2 = cos_ref[...], s1_ref[...], s2_ref[...]
    gn, gr = gn_ref[...], gr_ref[...]
    for h in range(MLA_HEADS):
        kn = kv[:, h * MLA_NOPE:(h + 1) * MLA_NOPE]
        r = lax.rsqrt((jnp.sum(kn * kn, axis=-1, keepdims=True) + ss_r) / float(MLA_QK) + EPS)
        lo = h * MLA_HEAD_PAD
        k_ref[:, lo:lo + MLA_NOPE] = (kn * r * gn).astype(k_ref.dtype)
        k_ref[:, lo + MLA_NOPE:lo + MLA_HEAD_PAD] = _rope_chunk(kr * r * gr, cos, s1, s2).astype(k_ref.dtype)


def _mla_prep_kernel(u_ref, gaq_ref, wqt_ref, gt_ref, cost_ref, sint_ref, gakv_ref, wk_ref, wvt_ref, gn_ref, gr_ref,
                     cos_ref, s1_ref, s2_ref, qt_ref, k_ref, vt_ref):
    _mla_q_kernel(u_ref, gaq_ref, wqt_ref, gt_ref, cost_ref, sint_ref, qt_ref)
    _mla_kv_kernel(u_ref, gakv_ref, wk_ref, wvt_ref, gn_ref, gr_ref, cos_ref, s1_ref, s2_ref, k_ref, vt_ref)


def _rope_tables(seq):
    half = MLA_ROPE // 2
    inv = ROPE_THETA ** (-jnp.arange(half, dtype=F32) / half)
    ang = jnp.arange(seq).astype(F32)[:, None] * inv[None, :]
    cos, sin = jnp.cos(ang), jnp.sin(ang)
    z = jnp.zeros_like(cos)
    pad = jnp.zeros((seq, ROPE_PAD - MLA_ROPE), F32)
    return (jnp.concatenate([cos, cos, pad], 1), jnp.concatenate([-sin, z, pad], 1),
            jnp.concatenate([z, sin, pad], 1), cos.T, sin.T)


def _mla_prep(u, seq, q_a_norm, kv_a_norm, w_uq, w_ukv, q_norm, k_norm):
    t = u.shape[0]
    tm = _tile(seq, KEY_BLOCK)
    spb = seq // tm
    cos, s1, s2, cos_t, sin_t = _rope_tables(seq)
    tab_spec = pl.BlockSpec((tm, ROPE_PAD), lambda i: (i % spb, 0))
    tab_t_spec = pl.BlockSpec((MLA_ROPE // 2, tm), lambda i: (0, i % spb))
    qw = MLA_HEADS * MLA_HEAD_PAD
    vw = MLA_HEADS * MLA_V

    wq_t = jnp.pad(w_uq.reshape(MLA_Q_RANK, MLA_HEADS, MLA_QK), ((0, 0), (0, 0), (0, MLA_HEAD_PAD - MLA_QK)))
    wq_t = wq_t.reshape(MLA_Q_RANK, qw).T.astype(BF16)
    gain_t = jnp.concatenate([q_norm * (MLA_QK ** -0.5 * LOG2E), jnp.zeros((MLA_HEAD_PAD - MLA_QK,), F32)])
    gain_t = jnp.broadcast_to(gain_t[:, None], (MLA_HEAD_PAD, tm))
    wkv = w_ukv.reshape(MLA_KV_RANK, MLA_HEADS, MLA_NOPE + MLA_V)
    wk = wkv[:, :, :MLA_NOPE].reshape(MLA_KV_RANK, -1).astype(BF16)
    wv_t = wkv[:, :, MLA_NOPE:].reshape(MLA_KV_RANK, -1).T.astype(BF16)
    blk = (_nbytes((tm, U_MLA), F32) + _nbytes(wq_t.shape, BF16) + _nbytes((qw, tm), BF16)
           + _nbytes((MLA_HEAD_PAD, tm), F32) + 2 * _nbytes((MLA_ROPE // 2, tm), F32)
           + _nbytes(wk.shape, BF16) + _nbytes(wv_t.shape, BF16) + _nbytes((tm, qw), BF16)
           + _nbytes((vw, tm), BF16) + 3 * _nbytes((tm, ROPE_PAD), F32))
    const = lambda shape: pl.BlockSpec(shape, lambda i: (0, 0))
    return pl.pallas_call(
        _mla_prep_kernel,
        grid=(t // tm,),
        in_specs=[
            pl.BlockSpec((tm, U_MLA), lambda i: (i, 0)),
            const((1, MLA_Q_RANK)), const(wq_t.shape), const((MLA_HEAD_PAD, tm)), tab_t_spec, tab_t_spec,
            const((1, MLA_KV_RANK)), const(wk.shape), const(wv_t.shape), const((1, MLA_NOPE)), const((1, ROPE_PAD)),
            tab_spec, tab_spec, tab_spec,
        ],
        out_specs=[pl.BlockSpec((qw, tm), lambda i: (0, i)), pl.BlockSpec((tm, qw), lambda i: (i, 0)),
                   pl.BlockSpec((None, vw, tm), lambda i: (i, 0, 0))],
        out_shape=[jax.ShapeDtypeStruct((qw, t), BF16), jax.ShapeDtypeStruct((t, qw), BF16),
                   jax.ShapeDtypeStruct((t // tm, vw, tm), BF16)],
        compiler_params=_cparams(("parallel",), blk, 4 * _nbytes((qw, tm), F32)),
        name="mla_prep",
    )(u, q_a_norm.reshape(1, -1), wq_t, gain_t, cos_t, sin_t,
      kv_a_norm.reshape(1, -1), wk, wv_t, k_norm[:MLA_NOPE].reshape(1, -1),
      jnp.concatenate([k_norm[MLA_NOPE:], jnp.zeros((ROPE_PAD - MLA_ROPE,), F32)]).reshape(1, -1), cos, s1, s2)


def _mla_attn_kernel(qt_ref, k_ref, vt_ref, o_ref, ssq_ref, sa_ref, sb_ref, *, tq, kb):
    per = tq // kb
    nq = qt_ref.shape[1] // tq
    nh = qt_ref.shape[0] // MLA_HEAD_PAD

    def scores(qi, blk, s_ref):
        for hh in range(nh):
            qt = qt_ref[hh * MLA_HEAD_PAD:(hh + 1) * MLA_HEAD_PAD, qi * tq:(qi + 1) * tq]
            for d in range(per):
                start = (blk * per + d) * kb
                kblk = k_ref[start:start + kb, hh * MLA_HEAD_PAD:(hh + 1) * MLA_HEAD_PAD]
                s = jnp.dot(kblk, qt, preferred_element_type=F32)
                if blk == qi:
                    key = d * kb + lax.broadcasted_iota(jnp.int32, s.shape, 0)
                    qry = lax.broadcasted_iota(jnp.int32, s.shape, 1)
                    s = jnp.where(key <= qry, s, NEG)
                s_ref[hh, d * kb:(d + 1) * kb, :] = s

    def softmax_pv(blk, s_ref, carries):
        out = []
        for hh in range(nh):
            m, l, acc = carries[hh]
            ss = [s_ref[hh, d * kb:(d + 1) * kb, :] for d in range(per)]
            m_new = m
            for s in ss:
                m_new = jnp.maximum(m_new, jnp.max(s, axis=0, keepdims=True))
            alpha = jnp.exp2(m - m_new)
            l = alpha * l
            acc = alpha * acc
            for d, s in enumerate(ss):
                p = jnp.exp2(s - m_new)
                l = l + jnp.sum(p, axis=0, keepdims=True)
                vt = vt_ref[blk * per + d, hh * MLA_V:(hh + 1) * MLA_V, :]
                acc = acc + jnp.dot(vt, p.astype(BF16), preferred_element_type=F32)
            out.append((m_new, l, acc))
        return tuple(out)

    items = [(qi, blk) for qi in range(nq) for blk in range(qi + 1)]
    bufs = (sa_ref, sb_ref)
    scores(*items[0], bufs[0])
    carries = None
    for idx, (qi, blk) in enumerate(items):
        if idx + 1 < len(items):
            scores(*items[idx + 1], bufs[(idx + 1) % 2])
        if blk == 0:
            carries = tuple((jnp.full((1, tq), NEG, F32), jnp.zeros((1, tq), F32), jnp.zeros((MLA_V, tq), F32))
                            for _ in range(nh))
        carries = softmax_pv(blk, bufs[idx % 2], carries)
        if blk == qi:
            ssq = None
            for hh in range(nh):
                _, l, acc = carries[hh]
                o = (acc * (1.0 / l)).T
                o_ref[qi * tq:(qi + 1) * tq, hh * MLA_V:(hh + 1) * MLA_V] = o.astype(o_ref.dtype)
                ssq = _row_ssq(o) if ssq is None else ssq + _row_ssq(o)
            ssq_ref[qi * tq:(qi + 1) * tq, :] = ssq


def _mla_attention(q_t, k, v_t, batch, seq):
    t = k.shape[0]
    kb = v_t.shape[2]
    tq = _tile(seq, 512)
    nkb = seq // kb
    nh = 2
    blk = nh * (2 * _nbytes((seq, MLA_HEAD_PAD), BF16) + _nbytes((seq, MLA_V), BF16) + _nbytes((seq, MLA_V), F32))
    return pl.pallas_call(
        functools.partial(_mla_attn_kernel, tq=tq, kb=kb),
        grid=(batch, MLA_HEADS // nh),
        in_specs=[
            pl.BlockSpec((nh * MLA_HEAD_PAD, seq), lambda b, h: (h, b)),
            pl.BlockSpec((seq, nh * MLA_HEAD_PAD), lambda b, h: (b, h)),
            pl.BlockSpec((nkb, nh * MLA_V, kb), lambda b, h: (b, h, 0)),
        ],
        out_specs=[pl.BlockSpec((seq, nh * MLA_V), lambda b, h: (b, h)), pl.BlockSpec((seq, LANES), lambda b, h: (b, h))],
        out_shape=[jax.ShapeDtypeStruct((t, MLA_WIDTH), BF16),
                   jax.ShapeDtypeStruct((t, (MLA_HEADS // nh) * LANES), F32)],
        scratch_shapes=[pltpu.VMEM((nh, tq, tq), F32), pltpu.VMEM((nh, tq, tq), F32)],
        compiler_params=_cparams(("parallel", "parallel"), blk, 10 * nh * _nbytes((tq, tq), F32)),
        name="mla_attention",
    )(q_t, k, v_t)


def _t5_bucket(dist):
    max_exact = REL_BUCKETS // 2
    d = jnp.maximum(dist, 1).astype(F32)
    large = max_exact + (jnp.log(d / max_exact) / math.log(REL_MAX_DIST / max_exact)
                         * (REL_BUCKETS - max_exact)).astype(jnp.int32)
    large = jnp.minimum(large, REL_BUCKETS - 1)
    return jnp.where(dist < max_exact, dist, large)


def _swa_kernel(sink_ref, q0_ref, q1_ref, q2_ref, q3_ref, kp_ref, kc_ref, vp_ref, vc_ref, gfold_ref, bias_ref,
                o_ref, ssq_ref):
    n = pl.program_id(1)
    nk, hd, kvw = 2 * BLOCK_Q, SWA_HEAD_DIM, SWA_KV_WIDTH
    key = lax.broadcasted_iota(jnp.int32, (nk, BLOCK_Q), 0)
    qry = lax.broadcasted_iota(jnp.int32, (nk, BLOCK_Q), 1)
    dist = qry + BLOCK_Q - key
    valid = (dist >= 0) & (dist < WINDOW) & ((key >= BLOCK_Q) | (n > 0))

    k2 = jnp.concatenate([kp_ref[...], kc_ref[...]], axis=0)
    sq = k2 * k2
    sq_hi = sq.astype(BF16)
    sq_lo = (sq - sq_hi.astype(F32)).astype(BF16)
    shift = hd.bit_length() - 1
    ri = lax.broadcasted_iota(jnp.int32, (kvw, kvw), 0) >> shift
    ci = lax.broadcasted_iota(jnp.int32, (kvw, kvw), 1) >> shift
    ind = jnp.where(ri == ci, 1.0, 0.0).astype(BF16)
    ss = jnp.dot(sq_hi, ind, preferred_element_type=F32) + jnp.dot(sq_lo, ind, preferred_element_type=F32)
    k2n = (k2 * lax.rsqrt(ss / float(hd) + EPS) * gfold_ref[...]).astype(BF16)
    v2t = jnp.concatenate([vp_ref[...], vc_ref[...]], axis=0).T.astype(BF16)

    qt = jnp.concatenate([q0_ref[...], q1_ref[...], q2_ref[...], q3_ref[...]], axis=1).T
    gw = SWA_GROUP * BLOCK_Q
    rows = []
    for kh in range(SWA_KV_HEADS):
        pieces = []
        if kh > 0:
            pieces.append(jnp.zeros((hd, kh * gw), BF16))
        for g in range(SWA_GROUP):
            j = kh * SWA_GROUP + g
            c = qt[j * hd:(j + 1) * hd, :]
            pieces.append((c * lax.rsqrt(jnp.sum(c * c, axis=0, keepdims=True) / float(hd) + EPS)).astype(BF16))
        if kh < SWA_KV_HEADS - 1:
            pieces.append(jnp.zeros((hd, (SWA_KV_HEADS - 1 - kh) * gw), BF16))
        rows.append(jnp.concatenate(pieces, axis=1))
    q_bd = jnp.concatenate(rows, axis=0)
    s_all = jnp.dot(k2n, q_bd, preferred_element_type=F32)

    es, invs = [], []
    for j in range(SWA_HEADS):
        lo = j * BLOCK_Q
        s = jnp.where(valid, s_all[:, lo:lo + BLOCK_Q] + bias_ref[:, lo:lo + BLOCK_Q], NEG)
        sink = sink_ref[j]
        m = jnp.maximum(jnp.max(s, axis=0, keepdims=True), sink)
        e = jnp.exp2(s - m)
        invs.append(1.0 / (jnp.sum(e, axis=0, keepdims=True) + jnp.exp2(sink - m)))
        es.append(e.astype(BF16))
    e_all = jnp.concatenate(es, axis=1)
    o_all = jnp.dot(v2t, e_all, preferred_element_type=F32)
    outs = []
    for j in range(SWA_HEADS):
        kh = j // SWA_GROUP
        outs.append(o_all[kh * hd:(kh + 1) * hd, j * BLOCK_Q:(j + 1) * BLOCK_Q] * invs[j])
    o = jnp.concatenate(outs, axis=0).T
    o_ref[...] = o.astype(o_ref.dtype)
    ssq_ref[...] = _row_ssq(o)


def _swa_attention(u, batch, seq, q_norm, k_norm, sinks, rel_bias):
    t = u.shape[0]
    nb = seq // BLOCK_Q
    nk = 2 * BLOCK_Q
    by_dist = rel_bias.astype(F32)[_t5_bucket(jnp.arange(WINDOW))].T
    row = jnp.concatenate([jnp.zeros((SWA_HEADS, 1), F32), by_dist], 1)
    bias = jnp.broadcast_to(row[:, None, :], (SWA_HEADS, nk, BLOCK_Q + 1)).reshape(SWA_HEADS, -1)
    bias = bias[:, :nk * BLOCK_Q].reshape(SWA_HEADS, nk, BLOCK_Q)
    bias = bias.transpose(1, 0, 2).reshape(nk, SWA_HEADS * BLOCK_Q) * LOG2E
    sinks = sinks * LOG2E
    gfold = jnp.tile(k_norm * q_norm * (SWA_HEAD_DIM ** -0.5 * LOG2E), SWA_KV_HEADS).reshape(1, SWA_KV_WIDTH)
    cur = lambda c: pl.BlockSpec((BLOCK_Q, SWA_KV_WIDTH), lambda b, n: (b * nb + n, c))
    prev = lambda c: pl.BlockSpec((BLOCK_Q, SWA_KV_WIDTH), lambda b, n: (b * nb + jnp.maximum(n - 1, 0), c))
    ck, cv = (U_MLA + S_K) // SWA_KV_WIDTH, (U_MLA + S_V) // SWA_KV_WIDTH
    gw = SWA_GROUP * SWA_HEAD_DIM
    q_specs = [pl.BlockSpec((BLOCK_Q, gw), functools.partial(lambda b, n, c: (b * nb + n, c), c=U_MLA // gw + kh))
               for kh in range(SWA_KV_HEADS)]
    blk = (_nbytes((BLOCK_Q, SWA_WIDTH), F32) + _nbytes((BLOCK_Q, SWA_WIDTH), BF16) + _nbytes((BLOCK_Q, LANES), F32)
           + 4 * _nbytes((BLOCK_Q, SWA_KV_WIDTH), F32) + _nbytes(bias.shape, F32))
    return pl.pallas_call(
        _swa_kernel,
        grid=(batch, nb),
        in_specs=[pl.BlockSpec(memory_space=pltpu.SMEM)] + q_specs + [
            prev(ck), cur(ck), prev(cv), cur(cv),
            pl.BlockSpec((1, SWA_KV_WIDTH), lambda b, n: (0, 0)),
            pl.BlockSpec(bias.shape, lambda b, n: (0, 0)),
        ],
        out_specs=[pl.BlockSpec((BLOCK_Q, SWA_WIDTH), lambda b, n: (b * nb + n, 0)),
                   pl.BlockSpec((BLOCK_Q, LANES), lambda b, n: (b * nb + n, 0))],
        out_shape=[jax.ShapeDtypeStruct((t, SWA_WIDTH), BF16), jax.ShapeDtypeStruct((t, LANES), F32)],
        compiler_params=_cparams(("parallel", "arbitrary"), blk, 4 * _nbytes((nk, SWA_HEADS * BLOCK_Q), F32)),
        name="swa_attention",
    )(sinks, u, u, u, u, u, u, u, u, gfold, bias)


def _mem_block_kernel(h_ref, kv_ref, gn_ref, wq_ref, wo_ref, gq_ref, gk_ref, gnext_ref, h_out_ref, xn_out_ref):
    h = h_ref[...]
    xn = _rms_rows(h, gn_ref[...]).astype(BF16)
    qm = jnp.dot(xn, wq_ref[...], preferred_element_type=F32)
    gq = gq_ref[...] * (MEM_HEAD_DIM ** -0.5)
    gk = gk_ref[...]
    outs = []
    for hd in range(MEM_HEADS):
        q = _rms_rows(qm[:, hd * MEM_HEAD_DIM:(hd + 1) * MEM_HEAD_DIM], gq).astype(BF16)
        lo = 2 * hd * MEM_HEAD_DIM
        k = _rms_rows(kv_ref[:, lo:lo + MEM_HEAD_DIM], gk).astype(BF16)
        v = kv_ref[:, lo + MEM_HEAD_DIM:lo + 2 * MEM_HEAD_DIM].astype(BF16)
        s = lax.dot_general(q, k, (((1,), (1,)), ((), ())), preferred_element_type=F32)
        e = jnp.exp(s - jnp.max(s, axis=-1, keepdims=True))
        p = (e / jnp.sum(e, axis=-1, keepdims=True)).astype(BF16)
        outs.append(jnp.dot(p, v, preferred_element_type=F32).astype(BF16))
    o = jnp.concatenate(outs, axis=-1)
    h_new = h + jnp.dot(o, wo_ref[...], preferred_element_type=F32)
    h_out_ref[...] = h_new
    xn_out_ref[...] = _rms_rows(h_new, gnext_ref[...]).astype(xn_out_ref.dtype)


def _mem_block(h, kvm, batch, seq, mem_len, norm_g, w_q, w_o, q_norm, k_norm, next_norm_g):
    t, d = h.shape
    w = w_q.shape[1]
    tm = _tile(seq, 256)
    nq = seq // tm
    once = pl.Buffered(1)
    blk = 2 * _nbytes((tm, d), F32) + _nbytes((tm, d), BF16) + _nbytes((mem_len, 2 * w), F32)
    temp = 2 * _nbytes((d, w), BF16) + 3 * _nbytes((tm, d), F32)
    return pl.pallas_call(
        _mem_block_kernel,
        grid=(batch, nq),
        in_specs=[
            pl.BlockSpec((tm, d), lambda b, i: (b * nq + i, 0)),
            pl.BlockSpec((mem_len, 2 * w), lambda b, i: (b, 0)),
            pl.BlockSpec((1, d), lambda b, i: (0, 0)),
            pl.BlockSpec((d, w), lambda b, i: (0, 0), pipeline_mode=once),
            pl.BlockSpec((w, d), lambda b, i: (0, 0), pipeline_mode=once),
            pl.BlockSpec((1, MEM_HEAD_DIM), lambda b, i: (0, 0)),
            pl.BlockSpec((1, MEM_HEAD_DIM), lambda b, i: (0, 0)),
            pl.BlockSpec((1, d), lambda b, i: (0, 0)),
        ],
        out_specs=[pl.BlockSpec((tm, d), lambda b, i: (b * nq + i, 0)), pl.BlockSpec((tm, d), lambda b, i: (b * nq + i, 0))],
        out_shape=[jax.ShapeDtypeStruct((t, d), F32), jax.ShapeDtypeStruct((t, d), BF16)],
        compiler_params=_cparams(("parallel", "arbitrary"), blk, temp),
        name="mem_block",
    )(h, kvm, norm_g.reshape(1, d), w_q.astype(BF16), w_o.astype(BF16), q_norm.reshape(1, -1), k_norm.reshape(1, -1),
      next_norm_g.reshape(1, d))


def _layer(h, mem, rel_bias, p, batch, seq):
    h, hb, ssq = _swiglu_half(h, _norm(h, p['ffn_a_norm']), p['ffn_a_gate'], p['ffn_a_up'], p['ffn_a_down'],
                              stats=True)

    w_in_t = p['w_in'].T
    u = _proj_t(hb, ssq, p['mix_norm'], w_in_t, n_first=U_MLA, second_start=IN_SWA,
                n_second=w_in_t.shape[0] - IN_SWA, tm=2048, tn=512, name="in_proj")

    q_t, k, v_t = _mla_prep(u, seq, p['mla_q_a_norm'], p['mla_kv_a_norm'], p['mla_w_uq'], p['mla_w_ukv'],
                            p['mla_q_norm'], p['mla_k_norm'])
    o_a, ssq_a = _mla_attention(q_t, k, v_t, batch, seq)
    ssq_a = ssq_a.reshape(ssq_a.shape[0], -1, LANES).sum(axis=1)
    o_b, ssq_b = _swa_attention(u, batch, seq, p['swa_q_norm'], p['swa_k_norm'], p['swa_sinks'], rel_bias)
    h = _proj2(o_a, o_b, ssq_a, ssq_b, p['out_norm_mla'], p['out_norm_swa'], p['w_out'], h,
               tm=2048, tn=256, name="out_proj")

    mem_len = mem.shape[0] // batch
    kvm = _mm_w32(_norm(mem, p['mem_norm']), p['mem_w_kv'], tm=512, tn=512, name="mem_kv_proj")
    h, xn = _mem_block(h, kvm, batch, seq, mem_len, p['mem_attn_norm'], p['mem_w_q'], p['mem_w_o'],
                       p['mem_q_norm'], p['mem_k_norm'], p['ffn_b_norm'])

    return _swiglu_half(h, xn, p['ffn_b_gate'], p['ffn_b_up'], p['ffn_b_down'])


_PARAM_NAMES = (
    'ffn_a_norm', 'ffn_a_gate', 'ffn_a_up', 'ffn_a_down', 'mix_norm', 'w_in',
    'mla_q_a_norm', 'mla_kv_a_norm', 'mla_w_uq', 'mla_w_ukv', 'mla_q_norm', 'mla_k_norm',
    'swa_q_norm', 'swa_k_norm', 'swa_sinks', 'out_norm_mla', 'out_norm_swa', 'w_out',
    'mem_attn_norm', 'mem_norm', 'mem_w_q', 'mem_w_kv', 'mem_q_norm', 'mem_k_norm', 'mem_w_o',
    'ffn_b_norm', 'ffn_b_gate', 'ffn_b_up', 'ffn_b_down',
)


def kernel(x, mem, rel_bias, ffn_a_norm, ffn_a_gate, ffn_a_up, ffn_a_down, mix_norm, w_in, mla_q_a_norm, mla_kv_a_norm, mla_w_uq, mla_w_ukv, mla_q_norm, mla_k_norm, swa_q_norm, swa_k_norm, swa_sinks, out_norm_mla, out_norm_swa, w_out, mem_attn_norm, mem_norm, mem_w_q, mem_w_kv, mem_q_norm, mem_k_norm, mem_w_o, ffn_b_norm, ffn_b_gate, ffn_b_up, ffn_b_down):
    stacked = (ffn_a_norm, ffn_a_gate, ffn_a_up, ffn_a_down, mix_norm, w_in, mla_q_a_norm, mla_kv_a_norm,
               mla_w_uq, mla_w_ukv, mla_q_norm, mla_k_norm, swa_q_norm, swa_k_norm, swa_sinks, out_norm_mla,
               out_norm_swa, w_out, mem_attn_norm, mem_norm, mem_w_q, mem_w_kv, mem_q_norm, mem_k_norm, mem_w_o,
               ffn_b_norm, ffn_b_gate, ffn_b_up, ffn_b_down)
    batch, seq, d = x.shape
    h = x.reshape(batch * seq, d)
    mem2 = mem.reshape(-1, d)
    for layer in range(ffn_a_norm.shape[0]):
        p = {name: arr[layer] for name, arr in zip(_PARAM_NAMES, stacked)}
        h = _layer(h, mem2, rel_bias, p, batch, seq)
    return h.reshape(batch, seq, d)
```

```python
import functools
import math

import jax
import jax.numpy as jnp
from jax import lax
from jax.experimental import pallas as pl
from jax.experimental.pallas import tpu as pltpu

F32 = jnp.float32
BF16 = jnp.bfloat16

MEM_HEADS = 4
MEM_HEAD_DIM = 128
MLA_HEADS = 16
MLA_Q_RANK = 896
MLA_KV_RANK = 512
MLA_NOPE = 128
MLA_ROPE = 64
MLA_V = 128
ROPE_THETA = 10000.0
SWA_HEADS = 32
SWA_KV_HEADS = 4
SWA_HEAD_DIM = 64
WINDOW = 128
REL_BUCKETS = 32
REL_MAX_DIST = 128
BLOCK_Q = 128
EPS = 1e-6
NEG = -1e30

MLA_QK = MLA_NOPE + MLA_ROPE
SWA_GROUP = SWA_HEADS // SWA_KV_HEADS
SWA_WIDTH = SWA_HEADS * SWA_HEAD_DIM
SWA_KV_WIDTH = SWA_KV_HEADS * SWA_HEAD_DIM
MLA_WIDTH = MLA_HEADS * MLA_V

LANES = 128
MLA_HEAD_PAD = 2 * LANES
ROPE_PAD = LANES

U_CKV = MLA_Q_RANK
U_KR = U_CKV + MLA_KV_RANK
U_MLA = U_KR + ROPE_PAD
IN_SWA = U_KR + MLA_ROPE
S_K = SWA_WIDTH
S_V = S_K + SWA_KV_WIDTH

LOG2E = math.log2(math.e)
KEY_BLOCK = 256

V7X_SCOPED_VMEM_BYTES = 60000 * 1024


def _tile(n, pref, align=8):
    if n <= pref:
        return n
    t = (pref // align) * align
    while t >= align:
        if n % t == 0:
            return t
        t -= align
    return n


def _nbytes(shape, dtype):
    n = 1
    for s in shape:
        n *= s
    return n * jnp.dtype(dtype).itemsize


def _cparams(semantics, pipelined_bytes, temp_bytes=0):
    need = 2 * pipelined_bytes + temp_bytes + (4 << 20)
    return pltpu.CompilerParams(
        dimension_semantics=semantics,
        vmem_limit_bytes=int(min(max(need, 16 << 20), V7X_SCOPED_VMEM_BYTES)),
    )


def _rms_rows(x, g, n=None):
    n = x.shape[-1] if n is None else n
    r = lax.rsqrt(jnp.sum(x * x, axis=-1, keepdims=True) / float(n) + EPS)
    return x * r * g


def _norm_kernel(x_ref, g_ref, o_ref):
    o_ref[...] = _rms_rows(x_ref[...], g_ref[...]).astype(o_ref.dtype)


def _norm(x, g, out_dtype=BF16):
    rows, d = x.shape
    tr = _tile(rows, 512)
    blk = _nbytes((tr, d), F32) + _nbytes((tr, d), out_dtype)
    return pl.pallas_call(
        _norm_kernel,
        grid=(rows // tr,),
        in_specs=[pl.BlockSpec((tr, d), lambda i: (i, 0)), pl.BlockSpec((1, d), lambda i: (0, 0))],
        out_specs=pl.BlockSpec((tr, d), lambda i: (i, 0)),
        out_shape=jax.ShapeDtypeStruct((rows, d), out_dtype),
        compiler_params=_cparams(("parallel",), blk, _nbytes((tr, d), F32)),
        name="rmsnorm",
    )(x, g.reshape(1, d))


def _mm_kernel(a_ref, w_ref, o_ref):
    o_ref[...] = jnp.dot(a_ref[...], w_ref[...], preferred_element_type=F32).astype(o_ref.dtype)


def _mm_res_kernel(a_ref, w_ref, r_ref, o_ref):
    o_ref[...] = r_ref[...] + jnp.dot(a_ref[...], w_ref[...], preferred_element_type=F32)


def _mm(a, w, res=None, *, tm, tn, out_dtype=F32, name="matmul"):
    m, k = a.shape
    n = w.shape[1]
    tm, tn = _tile(m, tm), _tile(n, tn, LANES)
    blk = _nbytes((tm, k), a.dtype) + _nbytes((k, tn), w.dtype) + _nbytes((tm, tn), out_dtype)
    in_specs = [pl.BlockSpec((tm, k), lambda j, i: (i, 0)), pl.BlockSpec((k, tn), lambda j, i: (0, j))]
    args = [a, w]
    kern = _mm_kernel
    if res is not None:
        in_specs.append(pl.BlockSpec((tm, tn), lambda j, i: (i, j)))
        args.append(res)
        blk += _nbytes((tm, tn), F32)
        kern = _mm_res_kernel
    return pl.pallas_call(
        kern,
        grid=(n // tn, m // tm),
        in_specs=in_specs,
        out_specs=pl.BlockSpec((tm, tn), lambda j, i: (i, j)),
        out_shape=jax.ShapeDtypeStruct((m, n), out_dtype),
        compiler_params=_cparams(("parallel", "arbitrary"), blk, 2 * _nbytes((tm, tn), F32)),
        name=name,
    )(*args)


def _row_ssq(x):
    sq = x * x
    part = sq[:, :LANES]
    for c in range(1, x.shape[1] // LANES):
        part = part + sq[:, c * LANES:(c + 1) * LANES]
    return part


def _inv_rms(ssq_block, n):
    return lax.rsqrt(jnp.sum(ssq_block, axis=-1, keepdims=True) / float(n) + EPS)


def _mm_res_stats_kernel(a_ref, w_ref, r_ref, o_ref, ob_ref, ssq_ref):
    h = r_ref[...] + jnp.dot(a_ref[...], w_ref[...], preferred_element_type=F32)
    o_ref[...] = h
    ob_ref[...] = h.astype(ob_ref.dtype)
    ssq_ref[...] = _row_ssq(h)


def _mm_res_stats(a, w, res, *, tm, tn, name):
    m, k = a.shape
    n = w.shape[1]
    tm, tn = _tile(m, tm), _tile(n, tn, LANES)
    blk = (_nbytes((tm, k), a.dtype) + _nbytes((k, tn), w.dtype) + 2 * _nbytes((tm, tn), F32) + _nbytes((tm, tn), BF16)
           + _nbytes((tm, LANES), F32))
    return pl.pallas_call(
        _mm_res_stats_kernel,
        grid=(n // tn, m // tm),
        in_specs=[
            pl.BlockSpec((tm, k), lambda j, i: (i, 0)),
            pl.BlockSpec((k, tn), lambda j, i: (0, j)),
            pl.BlockSpec((tm, tn), lambda j, i: (i, j)),
        ],
        out_specs=[
            pl.BlockSpec((tm, tn), lambda j, i: (i, j)),
            pl.BlockSpec((tm, tn), lambda j, i: (i, j)),
            pl.BlockSpec((tm, LANES), lambda j, i: (i, j)),
        ],
        out_shape=[jax.ShapeDtypeStruct((m, n), F32), jax.ShapeDtypeStruct((m, n), BF16),
                   jax.ShapeDtypeStruct((m, (n // tn) * LANES), F32)],
        compiler_params=_cparams(("parallel", "arbitrary"), blk, 3 * _nbytes((tm, tn), F32)),
        name=name,
    )(a, w, res)


def _proj_t_kernel(a_ref, ssq_ref, g_ref, wt_ref, o_ref):
    wb = (wt_ref[...] * g_ref[...]).astype(BF16)
    acc = lax.dot_general(a_ref[...], wb, (((1,), (1,)), ((), ())), preferred_element_type=F32)
    o_ref[...] = acc * _inv_rms(ssq_ref[...], a_ref.shape[1])


def _proj_t(a, ssq, g, wt, *, n_first, second_start, n_second, tm, tn, name):
    m, k = a.shape
    tm = _tile(m, tm)
    t1 = n_first // tn
    nt = t1 + n_second // tn
    shift = second_start - n_first
    wt_map = lambda i, j: (pl.multiple_of(j * tn + jnp.where(j >= t1, shift, 0), 8), 0)
    once = pl.Buffered(1)
    blk = _nbytes((tn, k), F32) + _nbytes((tm, tn), F32)
    return pl.pallas_call(
        _proj_t_kernel,
        grid=(m // tm, nt),
        in_specs=[
            pl.BlockSpec((tm, k), lambda i, j: (i, 0), pipeline_mode=once),
            pl.BlockSpec((tm, ssq.shape[1]), lambda i, j: (i, 0), pipeline_mode=once),
            pl.BlockSpec((1, k), lambda i, j: (0, 0)),
            pl.BlockSpec((pl.Element(tn), pl.Element(k)), wt_map),
        ],
        out_specs=pl.BlockSpec((tm, tn), lambda i, j: (i, j)),
        out_shape=jax.ShapeDtypeStruct((m, nt * tn), F32),
        compiler_params=_cparams(("parallel", "arbitrary"), blk,
                                 _nbytes((tm, k), BF16) + _nbytes((tm, ssq.shape[1]), F32) + _nbytes((tn, k), BF16)
                                 + 2 * _nbytes((tm, tn), F32)),
        name=name,
    )(a, ssq, g.reshape(1, k), wt)


def _proj2_kernel(a_ref, b_ref, ssqa_ref, ssqb_ref, gt_ref, w_ref, r_ref, o_ref):
    ka = a_ref.shape[1]
    wb = (w_ref[...] * gt_ref[...]).astype(BF16)
    ya = jnp.dot(a_ref[...], wb[:ka], preferred_element_type=F32) * _inv_rms(ssqa_ref[...], ka)
    yb = jnp.dot(b_ref[...], wb[ka:], preferred_element_type=F32) * _inv_rms(ssqb_ref[...], b_ref.shape[1])
    o_ref[...] = r_ref[...] + ya + yb


def _proj2(a, b, ssqa, ssqb, ga, gb, w, res, *, tm, tn, name):
    m, ka = a.shape
    kb = b.shape[1]
    n = w.shape[1]
    tm, tn = _tile(m, tm), _tile(n, tn, LANES)
    gtab = jnp.broadcast_to(jnp.concatenate([ga, gb])[:, None], (ka + kb, tn))
    once = pl.Buffered(1)
    blk = _nbytes((ka + kb, tn), F32) * 2 + 2 * _nbytes((tm, tn), F32) + _nbytes((tm, ssqa.shape[1] + LANES), F32)
    return pl.pallas_call(
        _proj2_kernel,
        grid=(m // tm, n // tn),
        in_specs=[
            pl.BlockSpec((tm, ka), lambda i, j: (i, 0), pipeline_mode=once),
            pl.BlockSpec((tm, kb), lambda i, j: (i, 0), pipeline_mode=once),
            pl.BlockSpec((tm, ssqa.shape[1]), lambda i, j: (i, 0)),
            pl.BlockSpec((tm, LANES), lambda i, j: (i, 0)),
            pl.BlockSpec((ka + kb, tn), lambda i, j: (0, 0), pipeline_mode=once),
            pl.BlockSpec((ka + kb, tn), lambda i, j: (0, j)),
            pl.BlockSpec((tm, tn), lambda i, j: (i, j)),
        ],
        out_specs=pl.BlockSpec((tm, tn), lambda i, j: (i, j)),
        out_shape=jax.ShapeDtypeStruct((m, n), F32),
        compiler_params=_cparams(("parallel", "arbitrary"), blk,
                                 _nbytes((tm, ka + kb), BF16) + _nbytes((ka + kb, tn), BF16) + 3 * _nbytes((tm, tn), F32)),
        name=name,
    )(a, b, ssqa, ssqb, gtab, w, res)


def _mm_w32_kernel(a_ref, w_ref, *rest, has_res):
    if has_res:
        r_ref, o_ref, wb_ref = rest
    else:
        o_ref, wb_ref = rest

    @pl.when(pl.program_id(1) == 0)
    def _():
        wb_ref[...] = w_ref[...].astype(BF16)

    acc = jnp.dot(a_ref[...], wb_ref[...], preferred_element_type=F32)
    if has_res:
        acc = r_ref[...] + acc
    o_ref[...] = acc.astype(o_ref.dtype)


def _mm_w32(a, w, res=None, *, tm, tn, n=None, out_dtype=F32, name="matmul"):
    m, k = a.shape
    n = w.shape[1] if n is None else n
    tm, tn = _tile(m, tm), _tile(n, tn, LANES)
    blk = _nbytes((tm, k), a.dtype) + _nbytes((k, tn), F32) + _nbytes((tm, tn), out_dtype)
    in_specs = [pl.BlockSpec((tm, k), lambda j, i: (i, 0)), pl.BlockSpec((k, tn), lambda j, i: (0, j))]
    args = [a, w]
    if res is not None:
        in_specs.append(pl.BlockSpec((tm, tn), lambda j, i: (i, j)))
        args.append(res)
        blk += _nbytes((tm, tn), F32)
    return pl.pallas_call(
        functools.partial(_mm_w32_kernel, has_res=res is not None),
        grid=(n // tn, m // tm),
        in_specs=in_specs,
        out_specs=pl.BlockSpec((tm, tn), lambda j, i: (i, j)),
        out_shape=jax.ShapeDtypeStruct((m, n), out_dtype),
        scratch_shapes=[pltpu.VMEM((k, tn), BF16)],
        compiler_params=_cparams(("parallel", "arbitrary"), blk, _nbytes((k, tn), BF16) + 2 * _nbytes((tm, tn), F32)),
        name=name,
    )(*args)


def _gated_kernel(x_ref, wg_ref, wu_ref, wd_ref, o_ref, wdb_ref):
    @pl.when(pl.program_id(0) == 0)
    def _():
        wdb_ref[...] = wd_ref[...].astype(BF16)

    x = x_ref[...]
    g = jnp.dot(x, wg_ref[...].astype(BF16), preferred_element_type=F32)
    u = jnp.dot(x, wu_ref[...].astype(BF16), preferred_element_type=F32)
    o_ref[...] = (0.5 * (g / (1.0 + jnp.exp(-g))) * u).astype(o_ref.dtype)


def _gated(x, wg, wu, wd, *, tm, tn):
    m, k = x.shape
    n = wg.shape[1]
    d_out = wd.shape[1]
    tm, tn = _tile(m, tm), _tile(n, tn, LANES)
    nj = n // tn
    blk = (2 * _nbytes((k, tn), F32) + _nbytes((tm, tn), BF16)
           + _nbytes((tn, d_out), F32) + _nbytes((tn, d_out), BF16))
    wd_map = lambda i, j: (jnp.where(i == 0, j, nj - 1), 0)
    return pl.pallas_call(
        _gated_kernel,
        grid=(m // tm, nj),
        in_specs=[
            pl.BlockSpec((tm, k), lambda i, j: (i, 0), pipeline_mode=pl.Buffered(1)),
            pl.BlockSpec((k, tn), lambda i, j: (0, j)),
            pl.BlockSpec((k, tn), lambda i, j: (0, j)),
            pl.BlockSpec((tn, d_out), wd_map),
        ],
        out_specs=[pl.BlockSpec((tm, tn), lambda i, j: (i, j)), pl.BlockSpec((tn, d_out), wd_map)],
        out_shape=[jax.ShapeDtypeStruct((m, n), BF16), jax.ShapeDtypeStruct((n, d_out), BF16)],
        compiler_params=_cparams(("arbitrary", "arbitrary"), blk,
                                 _nbytes((tm, k), BF16) + 2 * _nbytes((k, tn), BF16) + 3 * _nbytes((tm, tn), F32)),
        name="ffn_gate_up",
    )(x, wg, wu, wd)


def _swiglu_half(h, xn, wg, wu, wd, stats=False):
    act, wd_bf = _gated(xn, wg, wu, wd, tm=2048, tn=256)
    if stats:
        return _mm_res_stats(act, wd_bf, h, tm=512, tn=512, name="ffn_down_stats")
    return _mm(act, wd_bf, h, tm=512, tn=512, name="ffn_down")


def _rope_chunk(y, cos, s1, s2):
    return y * cos + pltpu.roll(y, ROPE_PAD - MLA_ROPE // 2, 1) * s1 + pltpu.roll(y, MLA_ROPE // 2, 1) * s2


def _mla_q_kernel(u_ref, ga_ref, wt_ref, gt_ref, cos_ref, sin_ref, o_ref):
    cn = _rms_rows(u_ref[:, :MLA_Q_RANK], ga_ref[...]).astype(BF16)
    qt = lax.dot_general(wt_ref[...], cn, (((1,), (1,)), ((), ())), preferred_element_type=F32)
    cos, sin, gain = cos_ref[...], sin_ref[...], gt_ref[...]
    half = MLA_ROPE // 2
    zeros = jnp.zeros((MLA_HEAD_PAD - MLA_QK, qt.shape[1]), o_ref.dtype)
    for h in range(MLA_HEADS):
        lo = h * MLA_HEAD_PAD
        c = qt[lo:lo + MLA_HEAD_PAD, :]
        y = c * lax.rsqrt(jnp.sum(c * c, axis=0, keepdims=True) / float(MLA_QK) + EPS) * gain
        x1, x2 = y[MLA_NOPE:MLA_NOPE + half, :], y[MLA_NOPE + half:MLA_QK, :]
        o_ref[lo:lo + MLA_NOPE, :] = y[:MLA_NOPE, :].astype(o_ref.dtype)
        o_ref[lo + MLA_NOPE:lo + MLA_NOPE + half, :] = (x1 * cos - x2 * sin).astype(o_ref.dtype)
        o_ref[lo + MLA_NOPE + half:lo + MLA_QK, :] = (x2 * cos + x1 * sin).astype(o_ref.dtype)
        o_ref[lo + MLA_QK:lo + MLA_HEAD_PAD, :] = zeros


def _mla_kv_kernel(u_ref, ga_ref, wk_ref, wvt_ref, gn_ref, gr_ref, cos_ref, s1_ref, s2_ref, k_ref, vt_ref):
    cn = _rms_rows(u_ref[:, U_CKV:U_KR], ga_ref[...]).astype(BF16)
    kv = jnp.dot(cn, wk_ref[...], preferred_element_type=F32)
    vt = lax.dot_general(wvt_ref[...], cn, (((1,), (1,)), ((), ())), preferred_element_type=F32)
    vt_ref[...] = vt.astype(vt_ref.dtype)
    kr = u_ref[:, U_KR:U_MLA]
    kr = jnp.where(lax.broadcasted_iota(jnp.int32, kr.shape, 1) < MLA_ROPE, kr, 0.0)
    ss_r = jnp.sum(kr * kr, axis=-1, keepdims=True)
    cos, s1, s2 = cos_ref[...], s1_ref[...], s2_ref[...]
    gn, gr = gn_ref[...], gr_ref[...]
    for h in range(MLA_HEADS):
        kn = kv[:, h * MLA_NOPE:(h + 1) * MLA_NOPE]
        r = lax.rsqrt((jnp.sum(kn * kn, axis=-1, keepdims=True) + ss_r) / float(MLA_QK) + EPS)
        lo = h * MLA_HEAD_PAD
        k_ref[:, lo:lo + MLA_NOPE] = (kn * r * gn).astype(k_ref.dtype)
        k_ref[:, lo + MLA_NOPE:lo + MLA_HEAD_PAD] = _rope_chunk(kr * r * gr, cos, s1, s2).astype(k_ref.dtype)


def _mla_prep_kernel(u_ref, gaq_ref, wqt_ref, gt_ref, cost_ref, sint_ref, gakv_ref, wk_ref, wvt_ref, gn_ref, gr_ref,
                     cos_ref, s1_ref, s2_ref, qt_ref, k_ref, vt_ref):
    _mla_q_kernel(u_ref, gaq_ref, wqt_ref, gt_ref, cost_ref, sint_ref, qt_ref)
    _mla_kv_kernel(u_ref, gakv_ref, wk_ref, wvt_ref, gn_ref, gr_ref, cos_ref, s1_ref, s2_ref, k_ref, vt_ref)


def _rope_tables(seq):
    half = MLA_ROPE // 2
    inv = ROPE_THETA ** (-jnp.arange(half, dtype=F32) / half)
    ang = jnp.arange(seq).astype(F32)[:, None] * inv[None, :]
    cos, sin = jnp.cos(ang), jnp.sin(ang)
    z = jnp.zeros_like(cos)
    pad = jnp.zeros((seq, ROPE_PAD - MLA_ROPE), F32)
    return (jnp.concatenate([cos, cos, pad], 1), jnp.concatenate([-sin, z, pad], 1),
            jnp.concatenate([z, sin, pad], 1), cos.T, sin.T)


def _mla_prep(u, seq, q_a_norm, kv_a_norm, w_uq, w_ukv, q_norm, k_norm):
    t = u.shape[0]
    tm = _tile(seq, KEY_BLOCK)
    spb = seq // tm
    cos, s1, s2, cos_t, sin_t = _rope_tables(seq)
    tab_spec = pl.BlockSpec((tm, ROPE_PAD), lambda i: (i % spb, 0))
    tab_t_spec = pl.BlockSpec((MLA_ROPE // 2, tm), lambda i: (0, i % spb))
    qw = MLA_HEADS * MLA_HEAD_PAD
    vw = MLA_HEADS * MLA_V

    wq_t = jnp.pad(w_uq.reshape(MLA_Q_RANK, MLA_HEADS, MLA_QK), ((0, 0), (0, 0), (0, MLA_HEAD_PAD - MLA_QK)))
    wq_t = wq_t.reshape(MLA_Q_RANK, qw).T.astype(BF16)
    gain_t = jnp.concatenate([q_norm * (MLA_QK ** -0.5 * LOG2E), jnp.zeros((MLA_HEAD_PAD - MLA_QK,), F32)])
    gain_t = jnp.broadcast_to(gain_t[:, None], (MLA_HEAD_PAD, tm))
    wkv = w_ukv.reshape(MLA_KV_RANK, MLA_HEADS, MLA_NOPE + MLA_V)
    wk = wkv[:, :, :MLA_NOPE].reshape(MLA_KV_RANK, -1).astype(BF16)
    wv_t = wkv[:, :, MLA_NOPE:].reshape(MLA_KV_RANK, -1).T.astype(BF16)
    blk = (_nbytes((tm, U_MLA), F32) + _nbytes(wq_t.shape, BF16) + _nbytes((qw, tm), BF16)
           + _nbytes((MLA_HEAD_PAD, tm), F32) + 2 * _nbytes((MLA_ROPE // 2, tm), F32)
           + _nbytes(wk.shape, BF16) + _nbytes(wv_t.shape, BF16) + _nbytes((tm, qw), BF16)
           + _nbytes((vw, tm), BF16) + 3 * _nbytes((tm, ROPE_PAD), F32))
    const = lambda shape: pl.BlockSpec(shape, lambda i: (0, 0))
    return pl.pallas_call(
        _mla_prep_kernel,
        grid=(t // tm,),
        in_specs=[
            pl.BlockSpec((tm, U_MLA), lambda i: (i, 0)),
            const((1, MLA_Q_RANK)), const(wq_t.shape), const((MLA_HEAD_PAD, tm)), tab_t_spec, tab_t_spec,
            const((1, MLA_KV_RANK)), const(wk.shape), const(wv_t.shape), const((1, MLA_NOPE)), const((1, ROPE_PAD)),
            tab_spec, tab_spec, tab_spec,
        ],
        out_specs=[pl.BlockSpec((qw, tm), lambda i: (0, i)), pl.BlockSpec((tm, qw), lambda i: (i, 0)),
                   pl.BlockSpec((None, vw, tm), lambda i: (i, 0, 0))],
        out_shape=[jax.ShapeDtypeStruct((qw, t), BF16), jax.ShapeDtypeStruct((t, qw), BF16),
                   jax.ShapeDtypeStruct((t // tm, vw, tm), BF16)],
        compiler_params=_cparams(("parallel",), blk, 4 * _nbytes((qw, tm), F32)),
        name="mla_prep",
    )(u, q_a_norm.reshape(1, -1), wq_t, gain_t, cos_t, sin_t,
      kv_a_norm.reshape(1, -1), wk, wv_t, k_norm[:MLA_NOPE].reshape(1, -1),
      jnp.concatenate([k_norm[MLA_NOPE:], jnp.zeros((ROPE_PAD - MLA_ROPE,), F32)]).reshape(1, -1), cos, s1, s2)


def _mla_attn_kernel(qt_ref, k_ref, vt_ref, o_ref, ssq_ref, sa_ref, sb_ref, *, tq, kb):
    per = tq // kb
    nq = qt_ref.shape[1] // tq
    nh = qt_ref.shape[0] // MLA_HEAD_PAD

    def scores(qi, blk, s_ref):
        for hh in range(nh):
            qt = qt_ref[hh * MLA_HEAD_PAD:(hh + 1) * MLA_HEAD_PAD, qi * tq:(qi + 1) * tq]
            for d in range(per):
                start = (blk * per + d) * kb
                kblk = k_ref[start:start + kb, hh * MLA_HEAD_PAD:(hh + 1) * MLA_HEAD_PAD]
                s = jnp.dot(kblk, qt, preferred_element_type=F32)
                if blk == qi:
                    key = d * kb + lax.broadcasted_iota(jnp.int32, s.shape, 0)
                    qry = lax.broadcasted_iota(jnp.int32, s.shape, 1)
                    s = jnp.where(key <= qry, s, NEG)
                s_ref[hh, d * kb:(d + 1) * kb, :] = s

    def softmax_pv(blk, s_ref, carries):
        out = []
        for hh in range(nh):
            m, l, acc = carries[hh]
            ss = [s_ref[hh, d * kb:(d + 1) * kb, :] for d in range(per)]
            m_new = m
            for s in ss:
                m_new = jnp.maximum(m_new, jnp.max(s, axis=0, keepdims=True))
            alpha = jnp.exp2(m - m_new)
            l = alpha * l
            acc = alpha * acc
            for d, s in enumerate(ss):
                p = jnp.exp2(s - m_new)
                l = l + jnp.sum(p, axis=0, keepdims=True)
                vt = vt_ref[blk * per + d, hh * MLA_V:(hh + 1) * MLA_V, :]
                acc = acc + jnp.dot(vt, p.astype(BF16), preferred_element_type=F32)
            out.append((m_new, l, acc))
        return tuple(out)

    items = [(qi, blk) for qi in range(nq) for blk in range(qi + 1)]
    bufs = (sa_ref, sb_ref)
    scores(*items[0], bufs[0])
    carries = None
    for idx, (qi, blk) in enumerate(items):
        if idx + 1 < len(items):
            scores(*items[idx + 1], bufs[(idx + 1) % 2])
        if blk == 0:
            carries = tuple((jnp.full((1, tq), NEG, F32), jnp.zeros((1, tq), F32), jnp.zeros((MLA_V, tq), F32))
                            for _ in range(nh))
        carries = softmax_pv(blk, bufs[idx % 2], carries)
        if blk == qi:
            ssq = None
            for hh in range(nh):
                _, l, acc = carries[hh]
                o = (acc * (1.0 / l)).T
                o_ref[qi * tq:(qi + 1) * tq, hh * MLA_V:(hh + 1) * MLA_V] = o.astype(o_ref.dtype)
                ssq = _row_ssq(o) if ssq is None else ssq + _row_ssq(o)
            ssq_ref[qi * tq:(qi + 1) * tq, :] = ssq


def _mla_attention(q_t, k, v_t, batch, seq):
    t = k.shape[0]
    kb = v_t.shape[2]
    tq = _tile(seq, 512)
    nkb = seq // kb
    nh = 2
    blk = nh * (2 * _nbytes((seq, MLA_HEAD_PAD), BF16) + _nbytes((seq, MLA_V), BF16) + _nbytes((seq, MLA_V), F32))
    return pl.pallas_call(
        functools.partial(_mla_attn_kernel, tq=tq, kb=kb),
        grid=(batch, MLA_HEADS // nh),
        in_specs=[
            pl.BlockSpec((nh * MLA_HEAD_PAD, seq), lambda b, h: (h, b)),
            pl.BlockSpec((seq, nh * MLA_HEAD_PAD), lambda b, h: (b, h)),
            pl.BlockSpec((nkb, nh * MLA_V, kb), lambda b, h: (b, h, 0)),
        ],
        out_specs=[pl.BlockSpec((seq, nh * MLA_V), lambda b, h: (b, h)), pl.BlockSpec((seq, LANES), lambda b, h: (b, h))],
        out_shape=[jax.ShapeDtypeStruct((t, MLA_WIDTH), BF16),
                   jax.ShapeDtypeStruct((t, (MLA_HEADS // nh) * LANES), F32)],
        scratch_shapes=[pltpu.VMEM((nh, tq, tq), F32), pltpu.VMEM((nh, tq, tq), F32)],
        compiler_params=_cparams(("parallel", "parallel"), blk, 10 * nh * _nbytes((tq, tq), F32)),
        name="mla_attention",
    )(q_t, k, v_t)


def _t5_bucket(dist):
    max_exact = REL_BUCKETS // 2
    d = jnp.maximum(dist, 1).astype(F32)
    large = max_exact + (jnp.log(d / max_exact) / math.log(REL_MAX_DIST / max_exact)
                         * (REL_BUCKETS - max_exact)).astype(jnp.int32)
    large = jnp.minimum(large, REL_BUCKETS - 1)
    return jnp.where(dist < max_exact, dist, large)


def _swa_kernel(sink_ref, q0_ref, q1_ref, q2_ref, q3_ref, kp_ref, kc_ref, vp_ref, vc_ref, gfold_ref, bias_ref,
                o_ref, ssq_ref):
    n = pl.program_id(1)
    nk, hd, kvw = 2 * BLOCK_Q, SWA_HEAD_DIM, SWA_KV_WIDTH
    key = lax.broadcasted_iota(jnp.int32, (nk, BLOCK_Q), 0)
    qry = lax.broadcasted_iota(jnp.int32, (nk, BLOCK_Q), 1)
    dist = qry + BLOCK_Q - key
    valid = (dist >= 0) & (dist < WINDOW) & ((key >= BLOCK_Q) | (n > 0))

    k2 = jnp.concatenate([kp_ref[...], kc_ref[...]], axis=0)
    sq = k2 * k2
    sq_hi = sq.astype(BF16)
    sq_lo = (sq - sq_hi.astype(F32)).astype(BF16)
    shift = hd.bit_length() - 1
    ri = lax.broadcasted_iota(jnp.int32, (kvw, kvw), 0) >> shift
    ci = lax.broadcasted_iota(jnp.int32, (kvw, kvw), 1) >> shift
    ind = jnp.where(ri == ci, 1.0, 0.0).astype(BF16)
    ss = jnp.dot(sq_hi, ind, preferred_element_type=F32) + jnp.dot(sq_lo, ind, preferred_element_type=F32)
    k2n = (k2 * lax.rsqrt(ss / float(hd) + EPS) * gfold_ref[...]).astype(BF16)
    v2t = jnp.concatenate([vp_ref[...], vc_ref[...]], axis=0).T.astype(BF16)

    qt = jnp.concatenate([q0_ref[...], q1_ref[...], q2_ref[...], q3_ref[...]], axis=1).T
    gw = SWA_GROUP * BLOCK_Q
    rows = []
    for kh in range(SWA_KV_HEADS):
        pieces = []
        if kh > 0:
            pieces.append(jnp.zeros((hd, kh * gw), BF16))
        for g in range(SWA_GROUP):
            j = kh * SWA_GROUP + g
            c = qt[j * hd:(j + 1) * hd, :]
            pieces.append((c * lax.rsqrt(jnp.sum(c * c, axis=0, keepdims=True) / float(hd) + EPS)).astype(BF16))
        if kh < SWA_KV_HEADS - 1:
            pieces.append(jnp.zeros((hd, (SWA_KV_HEADS - 1 - kh) * gw), BF16))
        rows.append(jnp.concatenate(pieces, axis=1))
    q_bd = jnp.concatenate(rows, axis=0)
    s_all = jnp.dot(k2n, q_bd, preferred_element_type=F32)

    es, invs = [], []
    for j in range(SWA_HEADS):
        lo = j * BLOCK_Q
        s = jnp.where(valid, s_all[:, lo:lo + BLOCK_Q] + bias_ref[:, lo:lo + BLOCK_Q], NEG)
        sink = sink_ref[j]
        m = jnp.maximum(jnp.max(s, axis=0, keepdims=True), sink)
        e = jnp.exp2(s - m)
        invs.append(1.0 / (jnp.sum(e, axis=0, keepdims=True) + jnp.exp2(sink - m)))
        es.append(e.astype(BF16))
    e_all = jnp.concatenate(es, axis=1)
    o_all = jnp.dot(v2t, e_all, preferred_element_type=F32)
    outs = []
    for j in range(SWA_HEADS):
        kh = j // SWA_GROUP
        outs.append(o_all[kh * hd:(kh + 1) * hd, j * BLOCK_Q:(j + 1) * BLOCK_Q] * invs[j])
    o = jnp.concatenate(outs, axis=0).T
    o_ref[...] = o.astype(o_ref.dtype)
    ssq_ref[...] = _row_ssq(o)


def _swa_attention(u, batch, seq, q_norm, k_norm, sinks, rel_bias):
    t = u.shape[0]
    nb = seq // BLOCK_Q
    nk = 2 * BLOCK_Q
    by_dist = rel_bias.astype(F32)[_t5_bucket(jnp.arange(WINDOW))].T
    row = jnp.concatenate([jnp.zeros((SWA_HEADS, 1), F32), by_dist], 1)
    bias = jnp.broadcast_to(row[:, None, :], (SWA_HEADS, nk, BLOCK_Q + 1)).reshape(SWA_HEADS, -1)
    bias = bias[:, :nk * BLOCK_Q].reshape(SWA_HEADS, nk, BLOCK_Q)
    bias = bias.transpose(1, 0, 2).reshape(nk, SWA_HEADS * BLOCK_Q) * LOG2E
    sinks = sinks * LOG2E
    gfold = jnp.tile(k_norm * q_norm * (SWA_HEAD_DIM ** -0.5 * LOG2E), SWA_KV_HEADS).reshape(1, SWA_KV_WIDTH)
    cur = lambda c: pl.BlockSpec((BLOCK_Q, SWA_KV_WIDTH), lambda b, n: (b * nb + n, c))
    prev = lambda c: pl.BlockSpec((BLOCK_Q, SWA_KV_WIDTH), lambda b, n: (b * nb + jnp.maximum(n - 1, 0), c))
    ck, cv = (U_MLA + S_K) // SWA_KV_WIDTH, (U_MLA + S_V) // SWA_KV_WIDTH
    gw = SWA_GROUP * SWA_HEAD_DIM
    q_specs = [pl.BlockSpec((BLOCK_Q, gw), functools.partial(lambda b, n, c: (b * nb + n, c), c=U_MLA // gw + kh))
               for kh in range(SWA_KV_HEADS)]
    blk = (_nbytes((BLOCK_Q, SWA_WIDTH), F32) + _nbytes((BLOCK_Q, SWA_WIDTH), BF16) + _nbytes((BLOCK_Q, LANES), F32)
           + 4 * _nbytes((BLOCK_Q, SWA_KV_WIDTH), F32) + _nbytes(bias.shape, F32))
    return pl.pallas_call(
        _swa_kernel,
        grid=(batch, nb),
        in_specs=[pl.BlockSpec(memory_space=pltpu.SMEM)] + q_specs + [
            prev(ck), cur(ck), prev(cv), cur(cv),
            pl.BlockSpec((1, SWA_KV_WIDTH), lambda b, n: (0, 0)),
            pl.BlockSpec(bias.shape, lambda b, n: (0, 0)),
        ],
        out_specs=[pl.BlockSpec((BLOCK_Q, SWA_WIDTH), lambda b, n: (b * nb + n, 0)),
                   pl.BlockSpec((BLOCK_Q, LANES), lambda b, n: (b * nb + n, 0))],
        out_shape=[jax.ShapeDtypeStruct((t, SWA_WIDTH), BF16), jax.ShapeDtypeStruct((t, LANES), F32)],
        compiler_params=_cparams(("parallel", "arbitrary"), blk, 4 * _nbytes((nk, SWA_HEADS * BLOCK_Q), F32)),
        name="swa_attention",
    )(sinks, u, u, u, u, u, u, u, u, gfold, bias)


def _mem_block_kernel(h_ref, kv_ref, gn_ref, wq_ref, wo_ref, gq_ref, gk_ref, gnext_ref, h_out_ref, xn_out_ref):
    h = h_ref[...]
    xn = _rms_rows(h, gn_ref[...]).astype(BF16)
    qm = jnp.dot(xn, wq_ref[...], preferred_element_type=F32)
    gq = gq_ref[...] * (MEM_HEAD_DIM ** -0.5)
    gk = gk_ref[...]
    outs = []
    for hd in range(MEM_HEADS):
        q = _rms_rows(qm[:, hd * MEM_HEAD_DIM:(hd + 1) * MEM_HEAD_DIM], gq).astype(BF16)
        lo = 2 * hd * MEM_HEAD_DIM
        k = _rms_rows(kv_ref[:, lo:lo + MEM_HEAD_DIM], gk).astype(BF16)
        v = kv_ref[:, lo + MEM_HEAD_DIM:lo + 2 * MEM_HEAD_DIM].astype(BF16)
        s = lax.dot_general(q, k, (((1,), (1,)), ((), ())), preferred_element_type=F32)
        e = jnp.exp(s - jnp.max(s, axis=-1, keepdims=True))
        p = (e / jnp.sum(e, axis=-1, keepdims=True)).astype(BF16)
        outs.append(jnp.dot(p, v, preferred_element_type=F32).astype(BF16))
    o = jnp.concatenate(outs, axis=-1)
    h_new = h + jnp.dot(o, wo_ref[...], preferred_element_type=F32)
    h_out_ref[...] = h_new
    xn_out_ref[...] = _rms_rows(h_new, gnext_ref[...]).astype(xn_out_ref.dtype)


def _mem_block(h, kvm, batch, seq, mem_len, norm_g, w_q, w_o, q_norm, k_norm, next_norm_g):
    t, d = h.shape
    w = w_q.shape[1]
    tm = _tile(seq, 256)
    nq = seq // tm
    once = pl.Buffered(1)
    blk = 2 * _nbytes((tm, d), F32) + _nbytes((tm, d), BF16) + _nbytes((mem_len, 2 * w), F32)
    temp = 2 * _nbytes((d, w), BF16) + 3 * _nbytes((tm, d), F32)
    return pl.pallas_call(
        _mem_block_kernel,
        grid=(batch, nq),
        in_specs=[
            pl.BlockSpec((tm, d), lambda b, i: (b * nq + i, 0)),
            pl.BlockSpec((mem_len, 2 * w), lambda b, i: (b, 0)),
            pl.BlockSpec((1, d), lambda b, i: (0, 0)),
            pl.BlockSpec((d, w), lambda b, i: (0, 0), pipeline_mode=once),
            pl.BlockSpec((w, d), lambda b, i: (0, 0), pipeline_mode=once),
            pl.BlockSpec((1, MEM_HEAD_DIM), lambda b, i: (0, 0)),
            pl.BlockSpec((1, MEM_HEAD_DIM), lambda b, i: (0, 0)),
            pl.BlockSpec((1, d), lambda b, i: (0, 0)),
        ],
        out_specs=[pl.BlockSpec((tm, d), lambda b, i: (b * nq + i, 0)), pl.BlockSpec((tm, d), lambda b, i: (b * nq + i, 0))],
        out_shape=[jax.ShapeDtypeStruct((t, d), F32), jax.ShapeDtypeStruct((t, d), BF16)],
        compiler_params=_cparams(("parallel", "arbitrary"), blk, temp),
        name="mem_block",
    )(h, kvm, norm_g.reshape(1, d), w_q.astype(BF16), w_o.astype(BF16), q_norm.reshape(1, -1), k_norm.reshape(1, -1),
      next_norm_g.reshape(1, d))


def _layer(h, mem, rel_bias, p, batch, seq):
    h, hb, ssq = _swiglu_half(h, _norm(h, p['ffn_a_norm']), p['ffn_a_gate'], p['ffn_a_up'], p['ffn_a_down'],
                              stats=True)

    w_in_t = p['w_in'].T
    u = _proj_t(hb, ssq, p['mix_norm'], w_in_t, n_first=U_MLA, second_start=IN_SWA,
                n_second=w_in_t.shape[0] - IN_SWA, tm=2048, tn=512, name="in_proj")

    q_t, k, v_t = _mla_prep(u, seq, p['mla_q_a_norm'], p['mla_kv_a_norm'], p['mla_w_uq'], p['mla_w_ukv'],
                            p['mla_q_norm'], p['mla_k_norm'])
    o_a, ssq_a = _mla_attention(q_t, k, v_t, batch, seq)
    ssq_a = ssq_a.reshape(ssq_a.shape[0], -1, LANES).sum(axis=1)
    o_b, ssq_b = _swa_attention(u, batch, seq, p['swa_q_norm'], p['swa_k_norm'], p['swa_sinks'], rel_bias)
    h = _proj2(o_a, o_b, ssq_a, ssq_b, p['out_norm_mla'], p['out_norm_swa'], p['w_out'], h,
               tm=2048, tn=256, name="out_proj")

    mem_len = mem.shape[0] // batch
    kvm = _mm_w32(_norm(mem, p['mem_norm']), p['mem_w_kv'], tm=512, tn=512, name="mem_kv_proj")
    h, xn = _mem_block(h, kvm, batch, seq, mem_len, p['mem_attn_norm'], p['mem_w_q'], p['mem_w_o'],
                       p['mem_q_norm'], p['mem_k_norm'], p['ffn_b_norm'])

    return _swiglu_half(h, xn, p['ffn_b_gate'], p['ffn_b_up'], p['ffn_b_down'])


_PARAM_NAMES = (
    'ffn_a_norm', 'ffn_a_gate', 'ffn_a_up', 'ffn_a_down', 'mix_norm', 'w_in',
    'mla_q_a_norm', 'mla_kv_a_norm', 'mla_w_uq', 'mla_w_ukv', 'mla_q_norm', 'mla_k_norm',
    'swa_q_norm', 'swa_k_norm', 'swa_sinks', 'out_norm_mla', 'out_norm_swa', 'w_out',
    'mem_attn_norm', 'mem_norm', 'mem_w_q', 'mem_w_kv', 'mem_q_norm', 'mem_k_norm', 'mem_w_o',
    'ffn_b_norm', 'ffn_b_gate', 'ffn_b_up', 'ffn_b_down',
)


def kernel(x, mem, rel_bias, ffn_a_norm, ffn_a_gate, ffn_a_up, ffn_a_down, mix_norm, w_in, mla_q_a_norm, mla_kv_a_norm, mla_w_uq, mla_w_ukv, mla_q_norm, mla_k_norm, swa_q_norm, swa_k_norm, swa_sinks, out_norm_mla, out_norm_swa, w_out, mem_attn_norm, mem_norm, mem_w_q, mem_w_kv, mem_q_norm, mem_k_norm, mem_w_o, ffn_b_norm, ffn_b_gate, ffn_b_up, ffn_b_down):
    stacked = (ffn_a_norm, ffn_a_gate, ffn_a_up, ffn_a_down, mix_norm, w_in, mla_q_a_norm, mla_kv_a_norm,
               mla_w_uq, mla_w_ukv, mla_q_norm, mla_k_norm, swa_q_norm, swa_k_norm, swa_sinks, out_norm_mla,
               out_norm_swa, w_out, mem_attn_norm, mem_norm, mem_w_q, mem_w_kv, mem_q_norm, mem_k_norm, mem_w_o,
               ffn_b_norm, ffn_b_gate, ffn_b_up, ffn_b_down)
    batch, seq, d = x.shape
    h = x.reshape(batch * seq, d)
    mem2 = mem.reshape(-1, d)
    for layer in range(ffn_a_norm.shape[0]):
        p = {name: arr[layer] for name, arr in zip(_PARAM_NAMES, stacked)}
        h = _layer(h, mem2, rel_bias, p, batch, seq)
    return h.reshape(batch, seq, d)
```

```python
import functools
import math

import jax
import jax.numpy as jnp
from jax import lax
from jax.experimental import pallas as pl
from jax.experimental.pallas import tpu as pltpu

F32 = jnp.float32
BF16 = jnp.bfloat16

MEM_HEADS = 4
MEM_HEAD_DIM = 128
MLA_HEADS = 16
MLA_Q_RANK = 896
MLA_KV_RANK = 512
MLA_NOPE = 128
MLA_ROPE = 64
MLA_V = 128
ROPE_THETA = 10000.0
SWA_HEADS = 32
SWA_KV_HEADS = 4
SWA_HEAD_DIM = 64
WINDOW = 128
REL_BUCKETS = 32
REL_MAX_DIST = 128
BLOCK_Q = 128
EPS = 1e-6
NEG = -1e30

MLA_QK = MLA_NOPE + MLA_ROPE
SWA_GROUP = SWA_HEADS // SWA_KV_HEADS
SWA_WIDTH = SWA_HEADS * SWA_HEAD_DIM
SWA_KV_WIDTH = SWA_KV_HEADS * SWA_HEAD_DIM
MLA_WIDTH = MLA_HEADS * MLA_V

LANES = 128
MLA_HEAD_PAD = 2 * LANES
ROPE_PAD = LANES

U_CKV = MLA_Q_RANK
U_KR = U_CKV + MLA_KV_RANK
U_MLA = U_KR + ROPE_PAD
IN_SWA = U_KR + MLA_ROPE
S_K = SWA_WIDTH
S_V = S_K + SWA_KV_WIDTH

LOG2E = math.log2(math.e)
KEY_BLOCK = 256

V7X_SCOPED_VMEM_BYTES = 60000 * 1024


def _tile(n, pref, align=8):
    if n <= pref:
        return n
    t = (pref // align) * align
    while t >= align:
        if n % t == 0:
            return t
        t -= align
    return n


def _nbytes(shape, dtype):
    n = 1
    for s in shape:
        n *= s
    return n * jnp.dtype(dtype).itemsize


def _cparams(semantics, pipelined_bytes, temp_bytes=0):
    need = 2 * pipelined_bytes + temp_bytes + (4 << 20)
    return pltpu.CompilerParams(
        dimension_semantics=semantics,
        vmem_limit_bytes=int(min(max(need, 16 << 20), V7X_SCOPED_VMEM_BYTES)),
    )


def _rms_rows(x, g, n=None):
    n = x.shape[-1] if n is None else n
    r = lax.rsqrt(jnp.sum(x * x, axis=-1, keepdims=True) / float(n) + EPS)
    return x * r * g


def _norm_kernel(x_ref, g_ref, o_ref):
    o_ref[...] = _rms_rows(x_ref[...], g_ref[...]).astype(o_ref.dtype)


def _norm(x, g, out_dtype=BF16):
    rows, d = x.shape
    tr = _tile(rows, 512)
    blk = _nbytes((tr, d), F32) + _nbytes((tr, d), out_dtype)
    return pl.pallas_call(
        _norm_kernel,
        grid=(rows // tr,),
        in_specs=[pl.BlockSpec((tr, d), lambda i: (i, 0)), pl.BlockSpec((1, d), lambda i: (0, 0))],
        out_specs=pl.BlockSpec((tr, d), lambda i: (i, 0)),
        out_shape=jax.ShapeDtypeStruct((rows, d), out_dtype),
        compiler_params=_cparams(("parallel",), blk, _nbytes((tr, d), F32)),
        name="rmsnorm",
    )(x, g.reshape(1, d))


def _mm_kernel(a_ref, w_ref, o_ref):
    o_ref[...] = jnp.dot(a_ref[...], w_ref[...], preferred_element_type=F32).astype(o_ref.dtype)


def _mm_res_kernel(a_ref, w_ref, r_ref, o_ref):
    o_ref[...] = r_ref[...] + jnp.dot(a_ref[...], w_ref[...], preferred_element_type=F32)


def _mm(a, w, res=None, *, tm, tn, out_dtype=F32, name="matmul"):
    m, k = a.shape
    n = w.shape[1]
    tm, tn = _tile(m, tm), _tile(n, tn, LANES)
    blk = _nbytes((tm, k), a.dtype) + _nbytes((k, tn), w.dtype) + _nbytes((tm, tn), out_dtype)
    in_specs = [pl.BlockSpec((tm, k), lambda j, i: (i, 0)), pl.BlockSpec((k, tn), lambda j, i: (0, j))]
    args = [a, w]
    kern = _mm_kernel
    if res is not None:
        in_specs.append(pl.BlockSpec((tm, tn), lambda j, i: (i, j)))
        args.append(res)
        blk += _nbytes((tm, tn), F32)
        kern = _mm_res_kernel
    return pl.pallas_call(
        kern,
        grid=(n // tn, m // tm),
        in_specs=in_specs,
        out_specs=pl.BlockSpec((tm, tn), lambda j, i: (i, j)),
        out_shape=jax.ShapeDtypeStruct((m, n), out_dtype),
        compiler_params=_cparams(("parallel", "arbitrary"), blk, 2 * _nbytes((tm, tn), F32)),
        name=name,
    )(*args)


def _row_ssq(x):
    sq = x * x
    part = sq[:, :LANES]
    for c in range(1, x.shape[1] // LANES):
        part = part + sq[:, c * LANES:(c + 1) * LANES]
    return part


def _inv_rms(ssq_block, n):
    return lax.rsqrt(jnp.sum(ssq_block, axis=-1, keepdims=True) / float(n) + EPS)


def _mm_res_stats_kernel(a_ref, w_ref, r_ref, o_ref, ob_ref, ssq_ref):
    h = r_ref[...] + jnp.dot(a_ref[...], w_ref[...], preferred_element_type=F32)
    o_ref[...] = h
    ob_ref[...] = h.astype(ob_ref.dtype)
    ssq_ref[...] = _row_ssq(h)


def _mm_res_stats(a, w, res, *, tm, tn, name):
    m, k = a.shape
    n = w.shape[1]
    tm, tn = _tile(m, tm), _tile(n, tn, LANES)
    blk = (_nbytes((tm, k), a.dtype) + _nbytes((k, tn), w.dtype) + 2 * _nbytes((tm, tn), F32) + _nbytes((tm, tn), BF16)
           + _nbytes((tm, LANES), F32))
    return pl.pallas_call(
        _mm_res_stats_kernel,
        grid=(n // tn, m // tm),
        in_specs=[
            pl.BlockSpec((tm, k), lambda j, i: (i, 0)),
            pl.BlockSpec((k, tn), lambda j, i: (0, j)),
            pl.BlockSpec((tm, tn), lambda j, i: (i, j)),
        ],
        out_specs=[
            pl.BlockSpec((tm, tn), lambda j, i: (i, j)),
            pl.BlockSpec((tm, tn), lambda j, i: (i, j)),
            pl.BlockSpec((tm, LANES), lambda j, i: (i, j)),
        ],
        out_shape=[jax.ShapeDtypeStruct((m, n), F32), jax.ShapeDtypeStruct((m, n), BF16),
                   jax.ShapeDtypeStruct((m, (n // tn) * LANES), F32)],
        compiler_params=_cparams(("parallel", "arbitrary"), blk, 3 * _nbytes((tm, tn), F32)),
        name=name,
    )(a, w, res)


def _proj_t_kernel(a_ref, ssq_ref, g_ref, wt_ref, o_ref, r_ref):
    @pl.when(pl.program_id(1) == 0)
    def _():
        r_ref[...] = jnp.broadcast_to(_inv_rms(ssq_ref[...], a_ref.shape[1]), r_ref.shape)

    wb = (wt_ref[...] * g_ref[...]).astype(BF16)
    acc = lax.dot_general(a_ref[...], wb, (((1,), (1,)), ((), ())), preferred_element_type=F32)
    o_ref[...] = acc * r_ref[:, :1]


def _proj_t(a, ssq, g, wt, *, n_first, second_start, n_second, tm, tn, name):
    m, k = a.shape
    tm = _tile(m, tm)
    t1 = n_first // tn
    nt = t1 + n_second // tn
    shift = second_start - n_first
    wt_map = lambda i, j: (pl.multiple_of(j * tn + jnp.where(j >= t1, shift, 0), 8), 0)
    once = pl.Buffered(1)
    blk = _nbytes((tn, k), F32) + _nbytes((tm, tn), F32)
    return pl.pallas_call(
        _proj_t_kernel,
        grid=(m // tm, nt),
        in_specs=[
            pl.BlockSpec((tm, k), lambda i, j: (i, 0), pipeline_mode=once),
            pl.BlockSpec((tm, ssq.shape[1]), lambda i, j: (i, 0), pipeline_mode=once),
            pl.BlockSpec((1, k), lambda i, j: (0, 0)),
            pl.BlockSpec((pl.Element(tn), pl.Element(k)), wt_map),
        ],
        out_specs=pl.BlockSpec((tm, tn), lambda i, j: (i, j)),
        out_shape=jax.ShapeDtypeStruct((m, nt * tn), F32),
        scratch_shapes=[pltpu.VMEM((tm, LANES), F32)],
        compiler_params=_cparams(("parallel", "arbitrary"), blk,
                                 _nbytes((tm, k), BF16) + _nbytes((tm, ssq.shape[1]), F32) + _nbytes((tn, k), BF16)
                                 + 2 * _nbytes((tm, tn), F32)),
        name=name,
    )(a, ssq, g.reshape(1, k), wt)


def _proj2_kernel(a_ref, b_ref, ssqa_ref, ssqb_ref, gt_ref, w_ref, r_ref, o_ref):
    ka = a_ref.shape[1]
    wb = (w_ref[...] * gt_ref[...]).astype(BF16)
    ya = jnp.dot(a_ref[...], wb[:ka], preferred_element_type=F32) * _inv_rms(ssqa_ref[...], ka)
    yb = jnp.dot(b_ref[...], wb[ka:], preferred_element_type=F32) * _inv_rms(ssqb_ref[...], b_ref.shape[1])
    o_ref[...] = r_ref[...] + ya + yb


def _proj2(a, b, ssqa, ssqb, ga, gb, w, res, *, tm, tn, name):
    m, ka = a.shape
    kb = b.shape[1]
    n = w.shape[1]
    tm, tn = _tile(m, tm), _tile(n, tn, LANES)
    gtab = jnp.broadcast_to(jnp.concatenate([ga, gb])[:, None], (ka + kb, tn))
    once = pl.Buffered(1)
    blk = _nbytes((ka + kb, tn), F32) * 2 + 2 * _nbytes((tm, tn), F32) + _nbytes((tm, ssqa.shape[1] + LANES), F32)
    return pl.pallas_call(
        _proj2_kernel,
        grid=(m // tm, n // tn),
        in_specs=[
            pl.BlockSpec((tm, ka), lambda i, j: (i, 0), pipeline_mode=once),
            pl.BlockSpec((tm, kb), lambda i, j: (i, 0), pipeline_mode=once),
            pl.BlockSpec((tm, ssqa.shape[1]), lambda i, j: (i, 0)),
            pl.BlockSpec((tm, LANES), lambda i, j: (i, 0)),
            pl.BlockSpec((ka + kb, tn), lambda i, j: (0, 0), pipeline_mode=once),
            pl.BlockSpec((ka + kb, tn), lambda i, j: (0, j)),
            pl.BlockSpec((tm, tn), lambda i, j: (i, j)),
        ],
        out_specs=pl.BlockSpec((tm, tn), lambda i, j: (i, j)),
        out_shape=jax.ShapeDtypeStruct((m, n), F32),
        compiler_params=_cparams(("parallel", "arbitrary"), blk,
                                 _nbytes((tm, ka + kb), BF16) + _nbytes((ka + kb, tn), BF16) + 3 * _nbytes((tm, tn), F32)),
        name=name,
    )(a, b, ssqa, ssqb, gtab, w, res)


def _mm_w32_kernel(a_ref, w_ref, *rest, has_res):
    if has_res:
        r_ref, o_ref, wb_ref = rest
    else:
        o_ref, wb_ref = rest

    @pl.when(pl.program_id(1) == 0)
    def _():
        wb_ref[...] = w_ref[...].astype(BF16)

    acc = jnp.dot(a_ref[...], wb_ref[...], preferred_element_type=F32)
    if has_res:
        acc = r_ref[...] + acc
    o_ref[...] = acc.astype(o_ref.dtype)


def _mm_w32(a, w, res=None, *, tm, tn, n=None, out_dtype=F32, name="matmul"):
    m, k = a.shape
    n = w.shape[1] if n is None else n
    tm, tn = _tile(m, tm), _tile(n, tn, LANES)
    blk = _nbytes((tm, k), a.dtype) + _nbytes((k, tn), F32) + _nbytes((tm, tn), out_dtype)
    in_specs = [pl.BlockSpec((tm, k), lambda j, i: (i, 0)), pl.BlockSpec((k, tn), lambda j, i: (0, j))]
    args = [a, w]
    if res is not None:
        in_specs.append(pl.BlockSpec((tm, tn), lambda j, i: (i, j)))
        args.append(res)
        blk += _nbytes((tm, tn), F32)
    return pl.pallas_call(
        functools.partial(_mm_w32_kernel, has_res=res is not None),
        grid=(n // tn, m // tm),
        in_specs=in_specs,
        out_specs=pl.BlockSpec((tm, tn), lambda j, i: (i, j)),
        out_shape=jax.ShapeDtypeStruct((m, n), out_dtype),
        scratch_shapes=[pltpu.VMEM((k, tn), BF16)],
        compiler_params=_cparams(("parallel", "arbitrary"), blk, _nbytes((k, tn), BF16) + 2 * _nbytes((tm, tn), F32)),
        name=name,
    )(*args)


def _gated_kernel(x_ref, wg_ref, wu_ref, wd_ref, o_ref, wdb_ref):
    @pl.when(pl.program_id(0) == 0)
    def _():
        wdb_ref[...] = wd_ref[...].astype(BF16)

    x = x_ref[...]
    g = jnp.dot(x, wg_ref[...].astype(BF16), preferred_element_type=F32)
    u = jnp.dot(x, wu_ref[...].astype(BF16), preferred_element_type=F32)
    o_ref[...] = (0.5 * (g / (1.0 + jnp.exp(-g))) * u).astype(o_ref.dtype)


def _gated(x, wg, wu, wd, *, tm, tn):
    m, k = x.shape
    n = wg.shape[1]
    d_out = wd.shape[1]
    tm, tn = _tile(m, tm), _tile(n, tn, LANES)
    nj = n // tn
    blk = (2 * _nbytes((k, tn), F32) + _nbytes((tm, tn), BF16)
           + _nbytes((tn, d_out), F32) + _nbytes((tn, d_out), BF16))
    wd_map = lambda i, j: (jnp.where(i == 0, j, nj - 1), 0)
    return pl.pallas_call(
        _gated_kernel,
        grid=(m // tm, nj),
        in_specs=[
            pl.BlockSpec((tm, k), lambda i, j: (i, 0), pipeline_mode=pl.Buffered(1)),
            pl.BlockSpec((k, tn), lambda i, j: (0, j)),
            pl.BlockSpec((k, tn), lambda i, j: (0, j)),
            pl.BlockSpec((tn, d_out), wd_map),
        ],
        out_specs=[pl.BlockSpec((tm, tn), lambda i, j: (i, j)), pl.BlockSpec((tn, d_out), wd_map)],
        out_shape=[jax.ShapeDtypeStruct((m, n), BF16), jax.ShapeDtypeStruct((n, d_out), BF16)],
        compiler_params=_cparams(("arbitrary", "arbitrary"), blk,
                                 _nbytes((tm, k), BF16) + 2 * _nbytes((k, tn), BF16) + 3 * _nbytes((tm, tn), F32)),
        name="ffn_gate_up",
    )(x, wg, wu, wd)


def _swiglu_half(h, xn, wg, wu, wd, stats=False):
    act, wd_bf = _gated(xn, wg, wu, wd, tm=2048, tn=256)
    if stats:
        return _mm_res_stats(act, wd_bf, h, tm=512, tn=512, name="ffn_down_stats")
    return _mm(act, wd_bf, h, tm=512, tn=512, name="ffn_down")


def _rope_chunk(y, cos, s1, s2):
    return y * cos + pltpu.roll(y, ROPE_PAD - MLA_ROPE // 2, 1) * s1 + pltpu.roll(y, MLA_ROPE // 2, 1) * s2


def _mla_q_kernel(u_ref, ga_ref, wt_ref, gt_ref, cos_ref, sin_ref, o_ref):
    cn = _rms_rows(u_ref[:, :MLA_Q_RANK], ga_ref[...]).astype(BF16)
    qt = lax.dot_general(wt_ref[...], cn, (((1,), (1,)), ((), ())), preferred_element_type=F32)
    cos, sin, gain = cos_ref[...], sin_ref[...], gt_ref[...]
    half = MLA_ROPE // 2
    zeros = jnp.zeros((MLA_HEAD_PAD - MLA_QK, qt.shape[1]), o_ref.dtype)
    for h in range(MLA_HEADS):
        lo = h * MLA_HEAD_PAD
        c = qt[lo:lo + MLA_HEAD_PAD, :]
        y = c * lax.rsqrt(jnp.sum(c * c, axis=0, keepdims=True) / float(MLA_QK) + EPS) * gain
        x1, x2 = y[MLA_NOPE:MLA_NOPE + half, :], y[MLA_NOPE + half:MLA_QK, :]
        o_ref[lo:lo + MLA_NOPE, :] = y[:MLA_NOPE, :].astype(o_ref.dtype)
        o_ref[lo + MLA_NOPE:lo + MLA_NOPE + half, :] = (x1 * cos - x2 * sin).astype(o_ref.dtype)
        o_ref[lo + MLA_NOPE + half:lo + MLA_QK, :] = (x2 * cos + x1 * sin).astype(o_ref.dtype)
        o_ref[lo + MLA_QK:lo + MLA_HEAD_PAD, :] = zeros


def _mla_kv_kernel(u_ref, ga_ref, wk_ref, wvt_ref, gn_ref, gr_ref, cos_ref, s1_ref, s2_ref, k_ref, vt_ref):
    cn = _rms_rows(u_ref[:, U_CKV:U_KR], ga_ref[...]).astype(BF16)
    kv = jnp.dot(cn, wk_ref[...], preferred_element_type=F32)
    vt = lax.dot_general(wvt_ref[...], cn, (((1,), (1,)), ((), ())), preferred_element_type=F32)
    vt_ref[...] = vt.astype(vt_ref.dtype)
    kr = u_ref[:, U_KR:U_MLA]
    kr = jnp.where(lax.broadcasted_iota(jnp.int32, kr.shape, 1) < MLA_ROPE, kr, 0.0)
    ss_r = jnp.sum(kr * kr, axis=-1, keepdims=True)
    cos, s1, s2 = cos_ref[...], s1_ref[...], s2_ref[...]
    gn, gr = gn_ref[...], gr_ref[...]
    for h in range(MLA_HEADS):
        kn = kv[:, h * MLA_NOPE:(h + 1) * MLA_NOPE]
        r = lax.rsqrt((jnp.sum(kn * kn, axis=-1, keepdims=True) + ss_r) / float(MLA_QK) + EPS)
        lo = h * MLA_HEAD_PAD
        k_ref[:, lo:lo + MLA_NOPE] = (kn * r * gn).astype(k_ref.dtype)
        k_ref[:, lo + MLA_NOPE:lo + MLA_HEAD_PAD] = _rope_chunk(kr * r * gr, cos, s1, s2).astype(k_ref.dtype)


def _mla_prep_kernel(u_ref, gaq_ref, wqt_ref, gt_ref, cost_ref, sint_ref, gakv_ref, wk_ref, wvt_ref, gn_ref, gr_ref,
                     cos_ref, s1_ref, s2_ref, qt_ref, k_ref, vt_ref):
    _mla_q_kernel(u_ref, gaq_ref, wqt_ref, gt_ref, cost_ref, sint_ref, qt_ref)
    _mla_kv_kernel(u_ref, gakv_ref, wk_ref, wvt_ref, gn_ref, gr_ref, cos_ref, s1_ref, s2_ref, k_ref, vt_ref)


def _rope_tables(seq):
    half = MLA_ROPE // 2
    inv = ROPE_THETA ** (-jnp.arange(half, dtype=F32) / half)
    ang = jnp.arange(seq).astype(F32)[:, None] * inv[None, :]
    cos, sin = jnp.cos(ang), jnp.sin(ang)
    z = jnp.zeros_like(cos)
    pad = jnp.zeros((seq, ROPE_PAD - MLA_ROPE), F32)
    return (jnp.concatenate([cos, cos, pad], 1), jnp.concatenate([-sin, z, pad], 1),
            jnp.concatenate([z, sin, pad], 1), cos.T, sin.T)


def _mla_prep(u, seq, q_a_norm, kv_a_norm, w_uq, w_ukv, q_norm, k_norm):
    t = u.shape[0]
    tm = _tile(seq, KEY_BLOCK)
    spb = seq // tm
    cos, s1, s2, cos_t, sin_t = _rope_tables(seq)
    tab_spec = pl.BlockSpec((tm, ROPE_PAD), lambda i: (i % spb, 0))
    tab_t_spec = pl.BlockSpec((MLA_ROPE // 2, tm), lambda i: (0, i % spb))
    qw = MLA_HEADS * MLA_HEAD_PAD
    vw = MLA_HEADS * MLA_V

    wq_t = jnp.pad(w_uq.reshape(MLA_Q_RANK, MLA_HEADS, MLA_QK), ((0, 0), (0, 0), (0, MLA_HEAD_PAD - MLA_QK)))
    wq_t = wq_t.reshape(MLA_Q_RANK, qw).T.astype(BF16)
    gain_t = jnp.concatenate([q_norm * (MLA_QK ** -0.5 * LOG2E), jnp.zeros((MLA_HEAD_PAD - MLA_QK,), F32)])
    gain_t = jnp.broadcast_to(gain_t[:, None], (MLA_HEAD_PAD, tm))
    wkv = w_ukv.reshape(MLA_KV_RANK, MLA_HEADS, MLA_NOPE + MLA_V)
    wk = wkv[:, :, :MLA_NOPE].reshape(MLA_KV_RANK, -1).astype(BF16)
    wv_t = wkv[:, :, MLA_NOPE:].reshape(MLA_KV_RANK, -1).T.astype(BF16)
    blk = (_nbytes((tm, U_MLA), F32) + _nbytes(wq_t.shape, BF16) + _nbytes((qw, tm), BF16)
           + _nbytes((MLA_HEAD_PAD, tm), F32) + 2 * _nbytes((MLA_ROPE // 2, tm), F32)
           + _nbytes(wk.shape, BF16) + _nbytes(wv_t.shape, BF16) + _nbytes((tm, qw), BF16)
           + _nbytes((vw, tm), BF16) + 3 * _nbytes((tm, ROPE_PAD), F32))
    const = lambda shape: pl.BlockSpec(shape, lambda i: (0, 0))
    return pl.pallas_call(
        _mla_prep_kernel,
        grid=(t // tm,),
        in_specs=[
            pl.BlockSpec((tm, U_MLA), lambda i: (i, 0)),
            const((1, MLA_Q_RANK)), const(wq_t.shape), const((MLA_HEAD_PAD, tm)), tab_t_spec, tab_t_spec,
            const((1, MLA_KV_RANK)), const(wk.shape), const(wv_t.shape), const((1, MLA_NOPE)), const((1, ROPE_PAD)),
            tab_spec, tab_spec, tab_spec,
        ],
        out_specs=[pl.BlockSpec((qw, tm), lambda i: (0, i)), pl.BlockSpec((tm, qw), lambda i: (i, 0)),
                   pl.BlockSpec((None, vw, tm), lambda i: (i, 0, 0))],
        out_shape=[jax.ShapeDtypeStruct((qw, t), BF16), jax.ShapeDtypeStruct((t, qw), BF16),
                   jax.ShapeDtypeStruct((t // tm, vw, tm), BF16)],
        compiler_params=_cparams(("parallel",), blk, 4 * _nbytes((qw, tm), F32)),
        name="mla_prep",
    )(u, q_a_norm.reshape(1, -1), wq_t, gain_t, cos_t, sin_t,
      kv_a_norm.reshape(1, -1), wk, wv_t, k_norm[:MLA_NOPE].reshape(1, -1),
      jnp.concatenate([k_norm[MLA_NOPE:], jnp.zeros((ROPE_PAD - MLA_ROPE,), F32)]).reshape(1, -1), cos, s1, s2)


def _mla_attn_kernel(qt_ref, k_ref, vt_ref, o_ref, ssq_ref, sa_ref, sb_ref, *, tq, kb):
    per = tq // kb
    nq = qt_ref.shape[1] // tq
    nh = qt_ref.shape[0] // MLA_HEAD_PAD

    def scores(qi, blk, s_ref):
        for hh in range(nh):
            qt = qt_ref[hh * MLA_HEAD_PAD:(hh + 1) * MLA_HEAD_PAD, qi * tq:(qi + 1) * tq]
            for d in range(per):
                start = (blk * per + d) * kb
                kblk = k_ref[start:start + kb, hh * MLA_HEAD_PAD:(hh + 1) * MLA_HEAD_PAD]
                s = jnp.dot(kblk, qt, preferred_element_type=F32)
                if blk == qi:
                    key = d * kb + lax.broadcasted_iota(jnp.int32, s.shape, 0)
                    qry = lax.broadcasted_iota(jnp.int32, s.shape, 1)
                    s = jnp.where(key <= qry, s, NEG)
                s_ref[hh, d * kb:(d + 1) * kb, :] = s

    def softmax_pv(blk, s_ref, carries):
        out = []
        for hh in range(nh):
            m, l, acc = carries[hh]
            ss = [s_ref[hh, d * kb:(d + 1) * kb, :] for d in range(per)]
            m_new = m
            for s in ss:
                m_new = jnp.maximum(m_new, jnp.max(s, axis=0, keepdims=True))
            alpha = jnp.exp2(m - m_new)
            l = alpha * l
            acc = alpha * acc
            for d, s in enumerate(ss):
                p = jnp.exp2(s - m_new)
                l = l + jnp.sum(p, axis=0, keepdims=True)
                vt = vt_ref[blk * per + d, hh * MLA_V:(hh + 1) * MLA_V, :]
                acc = acc + jnp.dot(vt, p.astype(BF16), preferred_element_type=F32)
            out.append((m_new, l, acc))
        return tuple(out)

    items = [(qi, blk) for qi in range(nq) for blk in range(qi + 1)]
    bufs = (sa_ref, sb_ref)
    scores(*items[0], bufs[0])
    carries = None
    for idx, (qi, blk) in enumerate(items):
        if idx + 1 < len(items):
            scores(*items[idx + 1], bufs[(idx + 1) % 2])
        if blk == 0:
            carries = tuple((jnp.full((1, tq), NEG, F32), jnp.zeros((1, tq), F32), jnp.zeros((MLA_V, tq), F32))
                            for _ in range(nh))
        carries = softmax_pv(blk, bufs[idx % 2], carries)
        if blk == qi:
            ssq = None
            for hh in range(nh):
                _, l, acc = carries[hh]
                o = (acc * (1.0 / l)).T
                o_ref[qi * tq:(qi + 1) * tq, hh * MLA_V:(hh + 1) * MLA_V] = o.astype(o_ref.dtype)
                ssq = _row_ssq(o) if ssq is None else ssq + _row_ssq(o)
            ssq_ref[qi * tq:(qi + 1) * tq, :] = ssq


def _mla_attention(q_t, k, v_t, batch, seq):
    t = k.shape[0]
    kb = v_t.shape[2]
    tq = _tile(seq, 512)
    nkb = seq // kb
    nh = 2
    blk = nh * (2 * _nbytes((seq, MLA_HEAD_PAD), BF16) + _nbytes((seq, MLA_V), BF16) + _nbytes((seq, MLA_V), F32))
    return pl.pallas_call(
        functools.partial(_mla_attn_kernel, tq=tq, kb=kb),
        grid=(batch, MLA_HEADS // nh),
        in_specs=[
            pl.BlockSpec((nh * MLA_HEAD_PAD, seq), lambda b, h: (h, b)),
            pl.BlockSpec((seq, nh * MLA_HEAD_PAD), lambda b, h: (b, h)),
            pl.BlockSpec((nkb, nh * MLA_V, kb), lambda b, h: (b, h, 0)),
        ],
        out_specs=[pl.BlockSpec((seq, nh * MLA_V), lambda b, h: (b, h)), pl.BlockSpec((seq, LANES), lambda b, h: (b, h))],
        out_shape=[jax.ShapeDtypeStruct((t, MLA_WIDTH), BF16),
                   jax.ShapeDtypeStruct((t, (MLA_HEADS // nh) * LANES), F32)],
        scratch_shapes=[pltpu.VMEM((nh, tq, tq), F32), pltpu.VMEM((nh, tq, tq), F32)],
        compiler_params=_cparams(("parallel", "parallel"), blk, 10 * nh * _nbytes((tq, tq), F32)),
        name="mla_attention",
    )(q_t, k, v_t)


def _t5_bucket(dist):
    max_exact = REL_BUCKETS // 2
    d = jnp.maximum(dist, 1).astype(F32)
    large = max_exact + (jnp.log(d / max_exact) / math.log(REL_MAX_DIST / max_exact)
                         * (REL_BUCKETS - max_exact)).astype(jnp.int32)
    large = jnp.minimum(large, REL_BUCKETS - 1)
    return jnp.where(dist < max_exact, dist, large)


def _swa_kernel(sink_ref, q0_ref, q1_ref, q2_ref, q3_ref, kp_ref, kc_ref, vp_ref, vc_ref, gfold_ref, bias_ref,
                o_ref, ssq_ref):
    n = pl.program_id(1)
    nk, hd, kvw = 2 * BLOCK_Q, SWA_HEAD_DIM, SWA_KV_WIDTH
    key = lax.broadcasted_iota(jnp.int32, (nk, BLOCK_Q), 0)
    qry = lax.broadcasted_iota(jnp.int32, (nk, BLOCK_Q), 1)
    dist = qry + BLOCK_Q - key
    valid = (dist >= 0) & (dist < WINDOW) & ((key >= BLOCK_Q) | (n > 0))

    k2 = jnp.concatenate([kp_ref[...], kc_ref[...]], axis=0)
    sq = k2 * k2
    sq_hi = sq.astype(BF16)
    sq_lo = (sq - sq_hi.astype(F32)).astype(BF16)
    shift = hd.bit_length() - 1
    ri = lax.broadcasted_iota(jnp.int32, (kvw, kvw), 0) >> shift
    ci = lax.broadcasted_iota(jnp.int32, (kvw, kvw), 1) >> shift
    ind = jnp.where(ri == ci, 1.0, 0.0).astype(BF16)
    ss = jnp.dot(sq_hi, ind, preferred_element_type=F32) + jnp.dot(sq_lo, ind, preferred_element_type=F32)
    k2n = (k2 * lax.rsqrt(ss / float(hd) + EPS) * gfold_ref[...]).astype(BF16)
    v2t = jnp.concatenate([vp_ref[...], vc_ref[...]], axis=0).T.astype(BF16)

    qt = jnp.concatenate([q0_ref[...], q1_ref[...], q2_ref[...], q3_ref[...]], axis=1).T
    gw = SWA_GROUP * BLOCK_Q
    rows = []
    for kh in range(SWA_KV_HEADS):
        pieces = []
        if kh > 0:
            pieces.append(jnp.zeros((hd, kh * gw), BF16))
        for g in range(SWA_GROUP):
            j = kh * SWA_GROUP + g
            c = qt[j * hd:(j + 1) * hd, :]
            pieces.append((c * lax.rsqrt(jnp.sum(c * c, axis=0, keepdims=True) / float(hd) + EPS)).astype(BF16))
        if kh < SWA_KV_HEADS - 1:
            pieces.append(jnp.zeros((hd, (SWA_KV_HEADS - 1 - kh) * gw), BF16))
        rows.append(jnp.concatenate(pieces, axis=1))
    q_bd = jnp.concatenate(rows, axis=0)
    s_all = jnp.dot(k2n, q_bd, preferred_element_type=F32)

    es, invs = [], []
    for j in range(SWA_HEADS):
        lo = j * BLOCK_Q
        s = jnp.where(valid, s_all[:, lo:lo + BLOCK_Q] + bias_ref[:, lo:lo + BLOCK_Q], NEG)
        sink = sink_ref[j]
        m = jnp.maximum(jnp.max(s, axis=0, keepdims=True), sink)
        e = jnp.exp2(s - m)
        invs.append(1.0 / (jnp.sum(e, axis=0, keepdims=True) + jnp.exp2(sink - m)))
        es.append(e.astype(BF16))
    e_all = jnp.concatenate(es, axis=1)
    o_all = jnp.dot(v2t, e_all, preferred_element_type=F32)
    outs = []
    for j in range(SWA_HEADS):
        kh = j // SWA_GROUP
        outs.append(o_all[kh * hd:(kh + 1) * hd, j * BLOCK_Q:(j + 1) * BLOCK_Q] * invs[j])
    o = jnp.concatenate(outs, axis=0).T
    o_ref[...] = o.astype(o_ref.dtype)
    ssq_ref[...] = _row_ssq(o)


def _swa_attention(u, batch, seq, q_norm, k_norm, sinks, rel_bias):
    t = u.shape[0]
    nb = seq // BLOCK_Q
    nk = 2 * BLOCK_Q
    by_dist = rel_bias.astype(F32)[_t5_bucket(jnp.arange(WINDOW))].T
    row = jnp.concatenate([jnp.zeros((SWA_HEADS, 1), F32), by_dist], 1)
    bias = jnp.broadcast_to(row[:, None, :], (SWA_HEADS, nk, BLOCK_Q + 1)).reshape(SWA_HEADS, -1)
    bias = bias[:, :nk * BLOCK_Q].reshape(SWA_HEADS, nk, BLOCK_Q)
    bias = bias.transpose(1, 0, 2).reshape(nk, SWA_HEADS * BLOCK_Q) * LOG2E
    sinks = sinks * LOG2E
    gfold = jnp.tile(k_norm * q_norm * (SWA_HEAD_DIM ** -0.5 * LOG2E), SWA_KV_HEADS).reshape(1, SWA_KV_WIDTH)
    cur = lambda c: pl.BlockSpec((BLOCK_Q, SWA_KV_WIDTH), lambda b, n: (b * nb + n, c))
    prev = lambda c: pl.BlockSpec((BLOCK_Q, SWA_KV_WIDTH), lambda b, n: (b * nb + jnp.maximum(n - 1, 0), c))
    ck, cv = (U_MLA + S_K) // SWA_KV_WIDTH, (U_MLA + S_V) // SWA_KV_WIDTH
    gw = SWA_GROUP * SWA_HEAD_DIM
    q_specs = [pl.BlockSpec((BLOCK_Q, gw), functools.partial(lambda b, n, c: (b * nb + n, c), c=U_MLA // gw + kh))
               for kh in range(SWA_KV_HEADS)]
    blk = (_nbytes((BLOCK_Q, SWA_WIDTH), F32) + _nbytes((BLOCK_Q, SWA_WIDTH), BF16) + _nbytes((BLOCK_Q, LANES), F32)
           + 4 * _nbytes((BLOCK_Q, SWA_KV_WIDTH), F32) + _nbytes(bias.shape, F32))
    return pl.pallas_call(
        _swa_kernel,
        grid=(batch, nb),
        in_specs=[pl.BlockSpec(memory_space=pltpu.SMEM)] + q_specs + [
            prev(ck), cur(ck), prev(cv), cur(cv),
            pl.BlockSpec((1, SWA_KV_WIDTH), lambda b, n: (0, 0)),
            pl.BlockSpec(bias.shape, lambda b, n: (0, 0)),
        ],
        out_specs=[pl.BlockSpec((BLOCK_Q, SWA_WIDTH), lambda b, n: (b * nb + n, 0)),
                   pl.BlockSpec((BLOCK_Q, LANES), lambda b, n: (b * nb + n, 0))],
        out_shape=[jax.ShapeDtypeStruct((t, SWA_WIDTH), BF16), jax.ShapeDtypeStruct((t, LANES), F32)],
        compiler_params=_cparams(("parallel", "arbitrary"), blk, 4 * _nbytes((nk, SWA_HEADS * BLOCK_Q), F32)),
        name="swa_attention",
    )(sinks, u, u, u, u, u, u, u, u, gfold, bias)


def _mem_block_kernel(h_ref, kv_ref, gn_ref, wq_ref, wo_ref, gq_ref, gk_ref, gnext_ref, h_out_ref, xn_out_ref):
    h = h_ref[...]
    xn = _rms_rows(h, gn_ref[...]).astype(BF16)
    qm = jnp.dot(xn, wq_ref[...], preferred_element_type=F32)
    gq = gq_ref[...] * (MEM_HEAD_DIM ** -0.5)
    gk = gk_ref[...]
    outs = []
    for hd in range(MEM_HEADS):
        q = _rms_rows(qm[:, hd * MEM_HEAD_DIM:(hd + 1) * MEM_HEAD_DIM], gq).astype(BF16)
        lo = 2 * hd * MEM_HEAD_DIM
        k = _rms_rows(kv_ref[:, lo:lo + MEM_HEAD_DIM], gk).astype(BF16)
        v = kv_ref[:, lo + MEM_HEAD_DIM:lo + 2 * MEM_HEAD_DIM].astype(BF16)
        s = lax.dot_general(q, k, (((1,), (1,)), ((), ())), preferred_element_type=F32)
        e = jnp.exp(s - jnp.max(s, axis=-1, keepdims=True))
        p = (e / jnp.sum(e, axis=-1, keepdims=True)).astype(BF16)
        outs.append(jnp.dot(p, v, preferred_element_type=F32).astype(BF16))
    o = jnp.concatenate(outs, axis=-1)
    h_new = h + jnp.dot(o, wo_ref[...], preferred_element_type=F32)
    h_out_ref[...] = h_new
    xn_out_ref[...] = _rms_rows(h_new, gnext_ref[...]).astype(xn_out_ref.dtype)


def _mem_block(h, kvm, batch, seq, mem_len, norm_g, w_q, w_o, q_norm, k_norm, next_norm_g):
    t, d = h.shape
    w = w_q.shape[1]
    tm = _tile(seq, 256)
    nq = seq // tm
    once = pl.Buffered(1)
    blk = 2 * _nbytes((tm, d), F32) + _nbytes((tm, d), BF16) + _nbytes((mem_len, 2 * w), F32)
    temp = 2 * _nbytes((d, w), BF16) + 3 * _nbytes((tm, d), F32)
    return pl.pallas_call(
        _mem_block_kernel,
        grid=(batch, nq),
        in_specs=[
            pl.BlockSpec((tm, d), lambda b, i: (b * nq + i, 0)),
            pl.BlockSpec((mem_len, 2 * w), lambda b, i: (b, 0)),
            pl.BlockSpec((1, d), lambda b, i: (0, 0)),
            pl.BlockSpec((d, w), lambda b, i: (0, 0), pipeline_mode=once),
            pl.BlockSpec((w, d), lambda b, i: (0, 0), pipeline_mode=once),
            pl.BlockSpec((1, MEM_HEAD_DIM), lambda b, i: (0, 0)),
            pl.BlockSpec((1, MEM_HEAD_DIM), lambda b, i: (0, 0)),
            pl.BlockSpec((1, d), lambda b, i: (0, 0)),
        ],
        out_specs=[pl.BlockSpec((tm, d), lambda b, i: (b * nq + i, 0)), pl.BlockSpec((tm, d), lambda b, i: (b * nq + i, 0))],
        out_shape=[jax.ShapeDtypeStruct((t, d), F32), jax.ShapeDtypeStruct((t, d), BF16)],
        compiler_params=_cparams(("parallel", "arbitrary"), blk, temp),
        name="mem_block",
    )(h, kvm, norm_g.reshape(1, d), w_q.astype(BF16), w_o.astype(BF16), q_norm.reshape(1, -1), k_norm.reshape(1, -1),
      next_norm_g.reshape(1, d))


def _layer(h, mem, rel_bias, p, batch, seq):
    h, hb, ssq = _swiglu_half(h, _norm(h, p['ffn_a_norm']), p['ffn_a_gate'], p['ffn_a_up'], p['ffn_a_down'],
                              stats=True)

    w_in_t = p['w_in'].T
    u = _proj_t(hb, ssq, p['mix_norm'], w_in_t, n_first=U_MLA, second_start=IN_SWA,
                n_second=w_in_t.shape[0] - IN_SWA, tm=2048, tn=512, name="in_proj")

    q_t, k, v_t = _mla_prep(u, seq, p['mla_q_a_norm'], p['mla_kv_a_norm'], p['mla_w_uq'], p['mla_w_ukv'],
                            p['mla_q_norm'], p['mla_k_norm'])
    o_a, ssq_a = _mla_attention(q_t, k, v_t, batch, seq)
    ssq_a = ssq_a.reshape(ssq_a.shape[0], -1, LANES).sum(axis=1)
    o_b, ssq_b = _swa_attention(u, batch, seq, p['swa_q_norm'], p['swa_k_norm'], p['swa_sinks'], rel_bias)
    h = _proj2(o_a, o_b, ssq_a, ssq_b, p['out_norm_mla'], p['out_norm_swa'], p['w_out'], h,
               tm=2048, tn=256, name="out_proj")

    mem_len = mem.shape[0] // batch
    kvm = _mm_w32(_norm(mem, p['mem_norm']), p['mem_w_kv'], tm=512, tn=512, name="mem_kv_proj")
    h, xn = _mem_block(h, kvm, batch, seq, mem_len, p['mem_attn_norm'], p['mem_w_q'], p['mem_w_o'],
                       p['mem_q_norm'], p['mem_k_norm'], p['ffn_b_norm'])

    return _swiglu_half(h, xn, p['ffn_b_gate'], p['ffn_b_up'], p['ffn_b_down'])


_PARAM_NAMES = (
    'ffn_a_norm', 'ffn_a_gate', 'ffn_a_up', 'ffn_a_down', 'mix_norm', 'w_in',
    'mla_q_a_norm', 'mla_kv_a_norm', 'mla_w_uq', 'mla_w_ukv', 'mla_q_norm', 'mla_k_norm',
    'swa_q_norm', 'swa_k_norm', 'swa_sinks', 'out_norm_mla', 'out_norm_swa', 'w_out',
    'mem_attn_norm', 'mem_norm', 'mem_w_q', 'mem_w_kv', 'mem_q_norm', 'mem_k_norm', 'mem_w_o',
    'ffn_b_norm', 'ffn_b_gate', 'ffn_b_up', 'ffn_b_down',
)


def kernel(x, mem, rel_bias, ffn_a_norm, ffn_a_gate, ffn_a_up, ffn_a_down, mix_norm, w_in, mla_q_a_norm, mla_kv_a_norm, mla_w_uq, mla_w_ukv, mla_q_norm, mla_k_norm, swa_q_norm, swa_k_norm, swa_sinks, out_norm_mla, out_norm_swa, w_out, mem_attn_norm, mem_norm, mem_w_q, mem_w_kv, mem_q_norm, mem_k_norm, mem_w_o, ffn_b_norm, ffn_b_gate, ffn_b_up, ffn_b_down):
    stacked = (ffn_a_norm, ffn_a_gate, ffn_a_up, ffn_a_down, mix_norm, w_in, mla_q_a_norm, mla_kv_a_norm,
               mla_w_uq, mla_w_ukv, mla_q_norm, mla_k_norm, swa_q_norm, swa_k_norm, swa_sinks, out_norm_mla,
               out_norm_swa, w_out, mem_attn_norm, mem_norm, mem_w_q, mem_w_kv, mem_q_norm, mem_k_norm, mem_w_o,
               ffn_b_norm, ffn_b_gate, ffn_b_up, ffn_b_down)
    batch, seq, d = x.shape
    h = x.reshape(batch * seq, d)
    mem2 = mem.reshape(-1, d)
    for layer in range(ffn_a_norm.shape[0]):
        p = {name: arr[layer] for name, arr in zip(_PARAM_NAMES, stacked)}
        h = _layer(h, mem2, rel_bias, p, batch, seq)
    return h.reshape(batch, seq, d)
```

```python
import functools
import math

import jax
import jax.numpy as jnp
from jax import lax
from jax.experimental import pallas as pl
from jax.experimental.pallas import tpu as pltpu

F32 = jnp.float32
BF16 = jnp.bfloat16

MEM_HEADS = 4
MEM_HEAD_DIM = 128
MLA_HEADS = 16
MLA_Q_RANK = 896
MLA_KV_RANK = 512
MLA_NOPE = 128
MLA_ROPE = 64
MLA_V = 128
ROPE_THETA = 10000.0
SWA_HEADS = 32
SWA_KV_HEADS = 4
SWA_HEAD_DIM = 64
WINDOW = 128
REL_BUCKETS = 32
REL_MAX_DIST = 128
BLOCK_Q = 128
EPS = 1e-6
NEG = -1e30

MLA_QK = MLA_NOPE + MLA_ROPE
SWA_GROUP = SWA_HEADS // SWA_KV_HEADS
SWA_WIDTH = SWA_HEADS * SWA_HEAD_DIM
SWA_KV_WIDTH = SWA_KV_HEADS * SWA_HEAD_DIM
MLA_WIDTH = MLA_HEADS * MLA_V

LANES = 128
MLA_HEAD_PAD = 2 * LANES
ROPE_PAD = LANES

U_CKV = MLA_Q_RANK
U_KR = U_CKV + MLA_KV_RANK
U_MLA = U_KR + ROPE_PAD
IN_SWA = U_KR + MLA_ROPE
S_K = SWA_WIDTH
S_V = S_K + SWA_KV_WIDTH

LOG2E = math.log2(math.e)
KEY_BLOCK = 256

V7X_SCOPED_VMEM_BYTES = 60000 * 1024


def _tile(n, pref, align=8):
    if n <= pref:
        return n
    t = (pref // align) * align
    while t >= align:
        if n % t == 0:
            return t
        t -= align
    return n


def _nbytes(shape, dtype):
    n = 1
    for s in shape:
        n *= s
    return n * jnp.dtype(dtype).itemsize


def _cparams(semantics, pipelined_bytes, temp_bytes=0):
    need = 2 * pipelined_bytes + temp_bytes + (4 << 20)
    return pltpu.CompilerParams(
        dimension_semantics=semantics,
        vmem_limit_bytes=int(min(max(need, 16 << 20), V7X_SCOPED_VMEM_BYTES)),
    )


def _rms_rows(x, g, n=None):
    n = x.shape[-1] if n is None else n
    r = lax.rsqrt(jnp.sum(x * x, axis=-1, keepdims=True) / float(n) + EPS)
    return x * r * g


def _norm_kernel(x_ref, g_ref, o_ref):
    o_ref[...] = _rms_rows(x_ref[...], g_ref[...]).astype(o_ref.dtype)


def _norm(x, g, out_dtype=BF16):
    rows, d = x.shape
    tr = _tile(rows, 512)
    blk = _nbytes((tr, d), F32) + _nbytes((tr, d), out_dtype)
    return pl.pallas_call(
        _norm_kernel,
        grid=(rows // tr,),
        in_specs=[pl.BlockSpec((tr, d), lambda i: (i, 0)), pl.BlockSpec((1, d), lambda i: (0, 0))],
        out_specs=pl.BlockSpec((tr, d), lambda i: (i, 0)),
        out_shape=jax.ShapeDtypeStruct((rows, d), out_dtype),
        compiler_params=_cparams(("parallel",), blk, _nbytes((tr, d), F32)),
        name="rmsnorm",
    )(x, g.reshape(1, d))


def _mm_kernel(a_ref, w_ref, o_ref):
    o_ref[...] = jnp.dot(a_ref[...], w_ref[...], preferred_element_type=F32).astype(o_ref.dtype)


def _mm_res_kernel(a_ref, w_ref, r_ref, o_ref):
    o_ref[...] = r_ref[...] + jnp.dot(a_ref[...], w_ref[...], preferred_element_type=F32)


def _mm(a, w, res=None, *, tm, tn, out_dtype=F32, name="matmul"):
    m, k = a.shape
    n = w.shape[1]
    tm, tn = _tile(m, tm), _tile(n, tn, LANES)
    blk = _nbytes((tm, k), a.dtype) + _nbytes((k, tn), w.dtype) + _nbytes((tm, tn), out_dtype)
    in_specs = [pl.BlockSpec((tm, k), lambda j, i: (i, 0)), pl.BlockSpec((k, tn), lambda j, i: (0, j))]
    args = [a, w]
    kern = _mm_kernel
    if res is not None:
        in_specs.append(pl.BlockSpec((tm, tn), lambda j, i: (i, j)))
        args.append(res)
        blk += _nbytes((tm, tn), F32)
        kern = _mm_res_kernel
    return pl.pallas_call(
        kern,
        grid=(n // tn, m // tm),
        in_specs=in_specs,
        out_specs=pl.BlockSpec((tm, tn), lambda j, i: (i, j)),
        out_shape=jax.ShapeDtypeStruct((m, n), out_dtype),
        compiler_params=_cparams(("parallel", "arbitrary"), blk, 2 * _nbytes((tm, tn), F32)),
        name=name,
    )(*args)


def _row_ssq(x):
    sq = x * x
    part = sq[:, :LANES]
    for c in range(1, x.shape[1] // LANES):
        part = part + sq[:, c * LANES:(c + 1) * LANES]
    return part


def _inv_rms(ssq_block, n):
    return lax.rsqrt(jnp.sum(ssq_block, axis=-1, keepdims=True) / float(n) + EPS)


def _mm_res_stats_kernel(a_ref, w_ref, r_ref, o_ref, ob_ref, ssq_ref):
    h = r_ref[...] + jnp.dot(a_ref[...], w_ref[...], preferred_element_type=F32)
    o_ref[...] = h
    ob_ref[...] = h.astype(ob_ref.dtype)
    ssq_ref[...] = _row_ssq(h)


def _mm_res_stats(a, w, res, *, tm, tn, name):
    m, k = a.shape
    n = w.shape[1]
    tm, tn = _tile(m, tm), _tile(n, tn, LANES)
    blk = (_nbytes((tm, k), a.dtype) + _nbytes((k, tn), w.dtype) + 2 * _nbytes((tm, tn), F32) + _nbytes((tm, tn), BF16)
           + _nbytes((tm, LANES), F32))
    return pl.pallas_call(
        _mm_res_stats_kernel,
        grid=(n // tn, m // tm),
        in_specs=[
            pl.BlockSpec((tm, k), lambda j, i: (i, 0)),
            pl.BlockSpec((k, tn), lambda j, i: (0, j)),
            pl.BlockSpec((tm, tn), lambda j, i: (i, j)),
        ],
        out_specs=[
            pl.BlockSpec((tm, tn), lambda j, i: (i, j)),
            pl.BlockSpec((tm, tn), lambda j, i: (i, j)),
            pl.BlockSpec((tm, LANES), lambda j, i: (i, j)),
        ],
        out_shape=[jax.ShapeDtypeStruct((m, n), F32), jax.ShapeDtypeStruct((m, n), BF16),
                   jax.ShapeDtypeStruct((m, (n // tn) * LANES), F32)],
        compiler_params=_cparams(("parallel", "arbitrary"), blk, 3 * _nbytes((tm, tn), F32)),
        name=name,
    )(a, w, res)


def _proj_t_kernel(a_ref, ssq_ref, g_ref, wt_ref, o_ref, r_ref):
    @pl.when(pl.program_id(1) == 0)
    def _():
        r_ref[...] = jnp.broadcast_to(_inv_rms(ssq_ref[...], a_ref.shape[1]), r_ref.shape)

    wb = (wt_ref[...] * g_ref[...]).astype(BF16)
    acc = lax.dot_general(a_ref[...], wb, (((1,), (1,)), ((), ())), preferred_element_type=F32)
    o_ref[...] = acc * r_ref[:, :1]


def _proj_t(a, ssq, g, wt, *, n_first, second_start, n_second, tm, tn, name):
    m, k = a.shape
    tm = _tile(m, tm)
    t1 = n_first // tn
    nt = t1 + n_second // tn
    shift = second_start - n_first
    wt_map = lambda i, j: (pl.multiple_of(j * tn + jnp.where(j >= t1, shift, 0), 8), 0)
    once = pl.Buffered(1)
    blk = _nbytes((tn, k), F32) + _nbytes((tm, tn), F32)
    return pl.pallas_call(
        _proj_t_kernel,
        grid=(m // tm, nt),
        in_specs=[
            pl.BlockSpec((tm, k), lambda i, j: (i, 0), pipeline_mode=once),
            pl.BlockSpec((tm, ssq.shape[1]), lambda i, j: (i, 0), pipeline_mode=once),
            pl.BlockSpec((1, k), lambda i, j: (0, 0)),
            pl.BlockSpec((pl.Element(tn), pl.Element(k)), wt_map),
        ],
        out_specs=pl.BlockSpec((tm, tn), lambda i, j: (i, j)),
        out_shape=jax.ShapeDtypeStruct((m, nt * tn), F32),
        scratch_shapes=[pltpu.VMEM((tm, LANES), F32)],
        compiler_params=_cparams(("parallel", "arbitrary"), blk,
                                 _nbytes((tm, k), BF16) + _nbytes((tm, ssq.shape[1]), F32) + _nbytes((tn, k), BF16)
                                 + 2 * _nbytes((tm, tn), F32)),
        name=name,
    )(a, ssq, g.reshape(1, k), wt)


def _proj2_kernel(a_ref, b_ref, ssqa_ref, ssqb_ref, gt_ref, w_ref, r_ref, o_ref):
    ka = a_ref.shape[1]
    wb = (w_ref[...] * gt_ref[...]).astype(BF16)
    ya = jnp.dot(a_ref[...], wb[:ka], preferred_element_type=F32) * _inv_rms(ssqa_ref[...], ka)
    yb = jnp.dot(b_ref[...], wb[ka:], preferred_element_type=F32) * _inv_rms(ssqb_ref[...], b_ref.shape[1])
    o_ref[...] = r_ref[...] + ya + yb


def _proj2(a, b, ssqa, ssqb, ga, gb, w, res, *, tm, tn, name):
    m, ka = a.shape
    kb = b.shape[1]
    n = w.shape[1]
    tm, tn = _tile(m, tm), _tile(n, tn, LANES)
    gtab = jnp.broadcast_to(jnp.concatenate([ga, gb])[:, None], (ka + kb, tn))
    once = pl.Buffered(1)
    blk = _nbytes((ka + kb, tn), F32) * 2 + 2 * _nbytes((tm, tn), F32) + _nbytes((tm, ssqa.shape[1] + LANES), F32)
    return pl.pallas_call(
        _proj2_kernel,
        grid=(m // tm, n // tn),
        in_specs=[
            pl.BlockSpec((tm, ka), lambda i, j: (i, 0), pipeline_mode=once),
            pl.BlockSpec((tm, kb), lambda i, j: (i, 0), pipeline_mode=once),
            pl.BlockSpec((tm, ssqa.shape[1]), lambda i, j: (i, 0)),
            pl.BlockSpec((tm, LANES), lambda i, j: (i, 0)),
            pl.BlockSpec((ka + kb, tn), lambda i, j: (0, 0), pipeline_mode=once),
            pl.BlockSpec((ka + kb, tn), lambda i, j: (0, j)),
            pl.BlockSpec((tm, tn), lambda i, j: (i, j)),
        ],
        out_specs=pl.BlockSpec((tm, tn), lambda i, j: (i, j)),
        out_shape=jax.ShapeDtypeStruct((m, n), F32),
        compiler_params=_cparams(("parallel", "arbitrary"), blk,
                                 _nbytes((tm, ka + kb), BF16) + _nbytes((ka + kb, tn), BF16) + 3 * _nbytes((tm, tn), F32)),
        name=name,
    )(a, b, ssqa, ssqb, gtab, w, res)


def _mm_w32_kernel(a_ref, w_ref, *rest, has_res):
    if has_res:
        r_ref, o_ref, wb_ref = rest
    else:
        o_ref, wb_ref = rest

    @pl.when(pl.program_id(1) == 0)
    def _():
        wb_ref[...] = w_ref[...].astype(BF16)

    acc = jnp.dot(a_ref[...], wb_ref[...], preferred_element_type=F32)
    if has_res:
        acc = r_ref[...] + acc
    o_ref[...] = acc.astype(o_ref.dtype)


def _mm_w32(a, w, res=None, *, tm, tn, n=None, out_dtype=F32, name="matmul"):
    m, k = a.shape
    n = w.shape[1] if n is None else n
    tm, tn = _tile(m, tm), _tile(n, tn, LANES)
    blk = _nbytes((tm, k), a.dtype) + _nbytes((k, tn), F32) + _nbytes((tm, tn), out_dtype)
    in_specs = [pl.BlockSpec((tm, k), lambda j, i: (i, 0)), pl.BlockSpec((k, tn), lambda j, i: (0, j))]
    args = [a, w]
    if res is not None:
        in_specs.append(pl.BlockSpec((tm, tn), lambda j, i: (i, j)))
        args.append(res)
        blk += _nbytes((tm, tn), F32)
    return pl.pallas_call(
        functools.partial(_mm_w32_kernel, has_res=res is not None),
        grid=(n // tn, m // tm),
        in_specs=in_specs,
        out_specs=pl.BlockSpec((tm, tn), lambda j, i: (i, j)),
        out_shape=jax.ShapeDtypeStruct((m, n), out_dtype),
        scratch_shapes=[pltpu.VMEM((k, tn), BF16)],
        compiler_params=_cparams(("parallel", "arbitrary"), blk, _nbytes((k, tn), BF16) + 2 * _nbytes((tm, tn), F32)),
        name=name,
    )(*args)


def _gated_kernel(x_ref, wg_ref, wu_ref, wd_ref, o_ref, wdb_ref):
    @pl.when(pl.program_id(0) == 0)
    def _():
        wdb_ref[...] = wd_ref[...].astype(BF16)

    x = x_ref[...]
    g = jnp.dot(x, wg_ref[...].astype(BF16), preferred_element_type=F32)
    u = jnp.dot(x, wu_ref[...].astype(BF16), preferred_element_type=F32)
    o_ref[...] = (0.25 * g * (1.0 + jnp.tanh(0.5 * g)) * u).astype(o_ref.dtype)


def _gated(x, wg, wu, wd, *, tm, tn):
    m, k = x.shape
    n = wg.shape[1]
    d_out = wd.shape[1]
    tm, tn = _tile(m, tm), _tile(n, tn, LANES)
    nj = n // tn
    blk = (2 * _nbytes((k, tn), F32) + _nbytes((tm, tn), BF16)
           + _nbytes((tn, d_out), F32) + _nbytes((tn, d_out), BF16))
    wd_map = lambda i, j: (jnp.where(i == 0, j, nj - 1), 0)
    return pl.pallas_call(
        _gated_kernel,
        grid=(m // tm, nj),
        in_specs=[
            pl.BlockSpec((tm, k), lambda i, j: (i, 0), pipeline_mode=pl.Buffered(1)),
            pl.BlockSpec((k, tn), lambda i, j: (0, j)),
            pl.BlockSpec((k, tn), lambda i, j: (0, j)),
            pl.BlockSpec((tn, d_out), wd_map),
        ],
        out_specs=[pl.BlockSpec((tm, tn), lambda i, j: (i, j)), pl.BlockSpec((tn, d_out), wd_map)],
        out_shape=[jax.ShapeDtypeStruct((m, n), BF16), jax.ShapeDtypeStruct((n, d_out), BF16)],
        compiler_params=_cparams(("arbitrary", "arbitrary"), blk,
                                 _nbytes((tm, k), BF16) + 2 * _nbytes((k, tn), BF16) + 3 * _nbytes((tm, tn), F32)),
        name="ffn_gate_up",
    )(x, wg, wu, wd)


def _swiglu_half(h, xn, wg, wu, wd, stats=False):
    act, wd_bf = _gated(xn, wg, wu, wd, tm=2048, tn=256)
    if stats:
        return _mm_res_stats(act, wd_bf, h, tm=512, tn=512, name="ffn_down_stats")
    return _mm(act, wd_bf, h, tm=512, tn=512, name="ffn_down")


def _rope_chunk(y, cos, s1, s2):
    return y * cos + pltpu.roll(y, ROPE_PAD - MLA_ROPE // 2, 1) * s1 + pltpu.roll(y, MLA_ROPE // 2, 1) * s2


def _mla_q_kernel(u_ref, ga_ref, wt_ref, gt_ref, cos_ref, sin_ref, o_ref):
    cn = _rms_rows(u_ref[:, :MLA_Q_RANK], ga_ref[...]).astype(BF16)
    qt = lax.dot_general(wt_ref[...], cn, (((1,), (1,)), ((), ())), preferred_element_type=F32)
    cos, sin, gain = cos_ref[...], sin_ref[...], gt_ref[...]
    half = MLA_ROPE // 2
    zeros = jnp.zeros((MLA_HEAD_PAD - MLA_QK, qt.shape[1]), o_ref.dtype)
    for h in range(MLA_HEADS):
        lo = h * MLA_HEAD_PAD
        c = qt[lo:lo + MLA_HEAD_PAD, :]
        y = c * lax.rsqrt(jnp.sum(c * c, axis=0, keepdims=True) / float(MLA_QK) + EPS) * gain
        x1, x2 = y[MLA_NOPE:MLA_NOPE + half, :], y[MLA_NOPE + half:MLA_QK, :]
        o_ref[lo:lo + MLA_NOPE, :] = y[:MLA_NOPE, :].astype(o_ref.dtype)
        o_ref[lo + MLA_NOPE:lo + MLA_NOPE + half, :] = (x1 * cos - x2 * sin).astype(o_ref.dtype)
        o_ref[lo + MLA_NOPE + half:lo + MLA_QK, :] = (x2 * cos + x1 * sin).astype(o_ref.dtype)
        o_ref[lo + MLA_QK:lo + MLA_HEAD_PAD, :] = zeros


def _mla_kv_kernel(u_ref, ga_ref, wk_ref, wvt_ref, gn_ref, gr_ref, cos_ref, s1_ref, s2_ref, k_ref, vt_ref):
    cn = _rms_rows(u_ref[:, U_CKV:U_KR], ga_ref[...]).astype(BF16)
    kv = jnp.dot(cn, wk_ref[...], preferred_element_type=F32)
    vt = lax.dot_general(wvt_ref[...], cn, (((1,), (1,)), ((), ())), preferred_element_type=F32)
    vt_ref[...] = vt.astype(vt_ref.dtype)
    kr = u_ref[:, U_KR:U_MLA]
    kr = jnp.where(lax.broadcasted_iota(jnp.int32, kr.shape, 1) < MLA_ROPE, kr, 0.0)
    ss_r = jnp.sum(kr * kr, axis=-1, keepdims=True)
    cos, s1, s2 = cos_ref[...], s1_ref[...], s2_ref[...]
    gn, gr = gn_ref[...], gr_ref[...]
    for h in range(MLA_HEADS):
        kn = kv[:, h * MLA_NOPE:(h + 1) * MLA_NOPE]
        r = lax.rsqrt((jnp.sum(kn * kn, axis=-1, keepdims=True) + ss_r) / float(MLA_QK) + EPS)
        lo = h * MLA_HEAD_PAD
        k_ref[:, lo:lo + MLA_NOPE] = (kn * r * gn).astype(k_ref.dtype)
        k_ref[:, lo + MLA_NOPE:lo + MLA_HEAD_PAD] = _rope_chunk(kr * r * gr, cos, s1, s2).astype(k_ref.dtype)


def _mla_prep_kernel(u_ref, gaq_ref, wqt_ref, gt_ref, cost_ref, sint_ref, gakv_ref, wk_ref, wvt_ref, gn_ref, gr_ref,
                     cos_ref, s1_ref, s2_ref, qt_ref, k_ref, vt_ref):
    _mla_q_kernel(u_ref, gaq_ref, wqt_ref, gt_ref, cost_ref, sint_ref, qt_ref)
    _mla_kv_kernel(u_ref, gakv_ref, wk_ref, wvt_ref, gn_ref, gr_ref, cos_ref, s1_ref, s2_ref, k_ref, vt_ref)


def _rope_tables(seq):
    half = MLA_ROPE // 2
    inv = ROPE_THETA ** (-jnp.arange(half, dtype=F32) / half)
    ang = jnp.arange(seq).astype(F32)[:, None] * inv[None, :]
    cos, sin = jnp.cos(ang), jnp.sin(ang)
    z = jnp.zeros_like(cos)
    pad = jnp.zeros((seq, ROPE_PAD - MLA_ROPE), F32)
    return (jnp.concatenate([cos, cos, pad], 1), jnp.concatenate([-sin, z, pad], 1),
            jnp.concatenate([z, sin, pad], 1), cos.T, sin.T)


def _mla_prep(u, seq, q_a_norm, kv_a_norm, w_uq, w_ukv, q_norm, k_norm):
    t = u.shape[0]
    tm = _tile(seq, KEY_BLOCK)
    spb = seq // tm
    cos, s1, s2, cos_t, sin_t = _rope_tables(seq)
    tab_spec = pl.BlockSpec((tm, ROPE_PAD), lambda i: (i % spb, 0))
    tab_t_spec = pl.BlockSpec((MLA_ROPE // 2, tm), lambda i: (0, i % spb))
    qw = MLA_HEADS * MLA_HEAD_PAD
    vw = MLA_HEADS * MLA_V

    wq_t = jnp.pad(w_uq.reshape(MLA_Q_RANK, MLA_HEADS, MLA_QK), ((0, 0), (0, 0), (0, MLA_HEAD_PAD - MLA_QK)))
    wq_t = wq_t.reshape(MLA_Q_RANK, qw).T.astype(BF16)
    gain_t = jnp.concatenate([q_norm * (MLA_QK ** -0.5 * LOG2E), jnp.zeros((MLA_HEAD_PAD - MLA_QK,), F32)])
    gain_t = jnp.broadcast_to(gain_t[:, None], (MLA_HEAD_PAD, tm))
    wkv = w_ukv.reshape(MLA_KV_RANK, MLA_HEADS, MLA_NOPE + MLA_V)
    wk = wkv[:, :, :MLA_NOPE].reshape(MLA_KV_RANK, -1).astype(BF16)
    wv_t = wkv[:, :, MLA_NOPE:].reshape(MLA_KV_RANK, -1).T.astype(BF16)
    blk = (_nbytes((tm, U_MLA), F32) + _nbytes(wq_t.shape, BF16) + _nbytes((qw, tm), BF16)
           + _nbytes((MLA_HEAD_PAD, tm), F32) + 2 * _nbytes((MLA_ROPE // 2, tm), F32)
           + _nbytes(wk.shape, BF16) + _nbytes(wv_t.shape, BF16) + _nbytes((tm, qw), BF16)
           + _nbytes((vw, tm), BF16) + 3 * _nbytes((tm, ROPE_PAD), F32))
    const = lambda shape: pl.BlockSpec(shape, lambda i: (0, 0))
    return pl.pallas_call(
        _mla_prep_kernel,
        grid=(t // tm,),
        in_specs=[
            pl.BlockSpec((tm, U_MLA), lambda i: (i, 0)),
            const((1, MLA_Q_RANK)), const(wq_t.shape), const((MLA_HEAD_PAD, tm)), tab_t_spec, tab_t_spec,
            const((1, MLA_KV_RANK)), const(wk.shape), const(wv_t.shape), const((1, MLA_NOPE)), const((1, ROPE_PAD)),
            tab_spec, tab_spec, tab_spec,
        ],
        out_specs=[pl.BlockSpec((qw, tm), lambda i: (0, i)), pl.BlockSpec((tm, qw), lambda i: (i, 0)),
                   pl.BlockSpec((None, vw, tm), lambda i: (i, 0, 0))],
        out_shape=[jax.ShapeDtypeStruct((qw, t), BF16), jax.ShapeDtypeStruct((t, qw), BF16),
                   jax.ShapeDtypeStruct((t // tm, vw, tm), BF16)],
        compiler_params=_cparams(("parallel",), blk, 4 * _nbytes((qw, tm), F32)),
        name="mla_prep",
    )(u, q_a_norm.reshape(1, -1), wq_t, gain_t, cos_t, sin_t,
      kv_a_norm.reshape(1, -1), wk, wv_t, k_norm[:MLA_NOPE].reshape(1, -1),
      jnp.concatenate([k_norm[MLA_NOPE:], jnp.zeros((ROPE_PAD - MLA_ROPE,), F32)]).reshape(1, -1), cos, s1, s2)


def _mla_attn_kernel(qt_ref, k_ref, vt_ref, o_ref, ssq_ref, sa_ref, sb_ref, *, tq, kb):
    per = tq // kb
    nq = qt_ref.shape[1] // tq
    nh = qt_ref.shape[0] // MLA_HEAD_PAD

    def scores(qi, blk, s_ref):
        for hh in range(nh):
            qt = qt_ref[hh * MLA_HEAD_PAD:(hh + 1) * MLA_HEAD_PAD, qi * tq:(qi + 1) * tq]
            for d in range(per):
                start = (blk * per + d) * kb
                kblk = k_ref[start:start + kb, hh * MLA_HEAD_PAD:(hh + 1) * MLA_HEAD_PAD]
                s = jnp.dot(kblk, qt, preferred_element_type=F32)
                if blk == qi:
                    key = d * kb + lax.broadcasted_iota(jnp.int32, s.shape, 0)
                    qry = lax.broadcasted_iota(jnp.int32, s.shape, 1)
                    s = jnp.where(key <= qry, s, NEG)
                s_ref[hh, d * kb:(d + 1) * kb, :] = s

    def softmax_pv(blk, s_ref, carries):
        out = []
        for hh in range(nh):
            m, l, acc = carries[hh]
            ss = [s_ref[hh, d * kb:(d + 1) * kb, :] for d in range(per)]
            m_new = m
            for s in ss:
                m_new = jnp.maximum(m_new, jnp.max(s, axis=0, keepdims=True))
            alpha = jnp.exp2(m - m_new)
            l = alpha * l
            acc = alpha * acc
            for d, s in enumerate(ss):
                p = jnp.exp2(s - m_new)
                l = l + jnp.sum(p, axis=0, keepdims=True)
                vt = vt_ref[blk * per + d, hh * MLA_V:(hh + 1) * MLA_V, :]
                acc = acc + jnp.dot(vt, p.astype(BF16), preferred_element_type=F32)
            out.append((m_new, l, acc))
        return tuple(out)

    items = [(qi, blk) for qi in range(nq) for blk in range(qi + 1)]
    bufs = (sa_ref, sb_ref)
    scores(*items[0], bufs[0])
    carries = None
    for idx, (qi, blk) in enumerate(items):
        if idx + 1 < len(items):
            scores(*items[idx + 1], bufs[(idx + 1) % 2])
        if blk == 0:
            carries = tuple((jnp.full((1, tq), NEG, F32), jnp.zeros((1, tq), F32), jnp.zeros((MLA_V, tq), F32))
                            for _ in range(nh))
        carries = softmax_pv(blk, bufs[idx % 2], carries)
        if blk == qi:
            ssq = None
            for hh in range(nh):
                _, l, acc = carries[hh]
                o = (acc * (1.0 / l)).T
                o_ref[qi * tq:(qi + 1) * tq, hh * MLA_V:(hh + 1) * MLA_V] = o.astype(o_ref.dtype)
                ssq = _row_ssq(o) if ssq is None else ssq + _row_ssq(o)
            ssq_ref[qi * tq:(qi + 1) * tq, :] = ssq


def _mla_attention(q_t, k, v_t, batch, seq):
    t = k.shape[0]
    kb = v_t.shape[2]
    tq = _tile(seq, 512)
    nkb = seq // kb
    nh = 2
    blk = nh * (2 * _nbytes((seq, MLA_HEAD_PAD), BF16) + _nbytes((seq, MLA_V), BF16) + _nbytes((seq, MLA_V), F32))
    return pl.pallas_call(
        functools.partial(_mla_attn_kernel, tq=tq, kb=kb),
        grid=(batch, MLA_HEADS // nh),
        in_specs=[
            pl.BlockSpec((nh * MLA_HEAD_PAD, seq), lambda b, h: (h, b)),
            pl.BlockSpec((seq, nh * MLA_HEAD_PAD), lambda b, h: (b, h)),
            pl.BlockSpec((nkb, nh * MLA_V, kb), lambda b, h: (b, h, 0)),
        ],
        out_specs=[pl.BlockSpec((seq, nh * MLA_V), lambda b, h: (b, h)), pl.BlockSpec((seq, LANES), lambda b, h: (b, h))],
        out_shape=[jax.ShapeDtypeStruct((t, MLA_WIDTH), BF16),
                   jax.ShapeDtypeStruct((t, (MLA_HEADS // nh) * LANES), F32)],
        scratch_shapes=[pltpu.VMEM((nh, tq, tq), F32), pltpu.VMEM((nh, tq, tq), F32)],
        compiler_params=_cparams(("parallel", "parallel"), blk, 10 * nh * _nbytes((tq, tq), F32)),
        name="mla_attention",
    )(q_t, k, v_t)


def _t5_bucket(dist):
    max_exact = REL_BUCKETS // 2
    d = jnp.maximum(dist, 1).astype(F32)
    large = max_exact + (jnp.log(d / max_exact) / math.log(REL_MAX_DIST / max_exact)
                         * (REL_BUCKETS - max_exact)).astype(jnp.int32)
    large = jnp.minimum(large, REL_BUCKETS - 1)
    return jnp.where(dist < max_exact, dist, large)


def _swa_kernel(sink_ref, q0_ref, q1_ref, q2_ref, q3_ref, kp_ref, kc_ref, vp_ref, vc_ref, gfold_ref, bias_ref,
                o_ref, ssq_ref):
    n = pl.program_id(1)
    nk, hd, kvw = 2 * BLOCK_Q, SWA_HEAD_DIM, SWA_KV_WIDTH
    key = lax.broadcasted_iota(jnp.int32, (nk, BLOCK_Q), 0)
    qry = lax.broadcasted_iota(jnp.int32, (nk, BLOCK_Q), 1)
    dist = qry + BLOCK_Q - key
    valid = (dist >= 0) & (dist < WINDOW) & ((key >= BLOCK_Q) | (n > 0))

    k2 = jnp.concatenate([kp_ref[...], kc_ref[...]], axis=0)
    sq = k2 * k2
    sq_hi = sq.astype(BF16)
    sq_lo = (sq - sq_hi.astype(F32)).astype(BF16)
    shift = hd.bit_length() - 1
    ri = lax.broadcasted_iota(jnp.int32, (kvw, kvw), 0) >> shift
    ci = lax.broadcasted_iota(jnp.int32, (kvw, kvw), 1) >> shift
    ind = jnp.where(ri == ci, 1.0, 0.0).astype(BF16)
    ss = jnp.dot(sq_hi, ind, preferred_element_type=F32) + jnp.dot(sq_lo, ind, preferred_element_type=F32)
    k2n = (k2 * lax.rsqrt(ss / float(hd) + EPS) * gfold_ref[...]).astype(BF16)
    v2t = jnp.concatenate([vp_ref[...], vc_ref[...]], axis=0).T.astype(BF16)

    qt = jnp.concatenate([q0_ref[...], q1_ref[...], q2_ref[...], q3_ref[...]], axis=1).T
    gw = SWA_GROUP * BLOCK_Q
    rows = []
    for kh in range(SWA_KV_HEADS):
        pieces = []
        if kh > 0:
            pieces.append(jnp.zeros((hd, kh * gw), BF16))
        for g in range(SWA_GROUP):
            j = kh * SWA_GROUP + g
            c = qt[j * hd:(j + 1) * hd, :]
            pieces.append((c * lax.rsqrt(jnp.sum(c * c, axis=0, keepdims=True) / float(hd) + EPS)).astype(BF16))
        if kh < SWA_KV_HEADS - 1:
            pieces.append(jnp.zeros((hd, (SWA_KV_HEADS - 1 - kh) * gw), BF16))
        rows.append(jnp.concatenate(pieces, axis=1))
    q_bd = jnp.concatenate(rows, axis=0)
    s_all = jnp.dot(k2n, q_bd, preferred_element_type=F32)

    es, invs = [], []
    for j in range(SWA_HEADS):
        lo = j * BLOCK_Q
        s = jnp.where(valid, s_all[:, lo:lo + BLOCK_Q] + bias_ref[:, lo:lo + BLOCK_Q], NEG)
        sink = sink_ref[j]
        m = jnp.maximum(jnp.max(s, axis=0, keepdims=True), sink)
        e = jnp.exp2(s - m)
        invs.append(1.0 / (jnp.sum(e, axis=0, keepdims=True) + jnp.exp2(sink - m)))
        es.append(e.astype(BF16))
    e_all = jnp.concatenate(es, axis=1)
    o_all = jnp.dot(v2t, e_all, preferred_element_type=F32)
    outs = []
    for j in range(SWA_HEADS):
        kh = j // SWA_GROUP
        outs.append(o_all[kh * hd:(kh + 1) * hd, j * BLOCK_Q:(j + 1) * BLOCK_Q] * invs[j])
    o = jnp.concatenate(outs, axis=0).T
    o_ref[...] = o.astype(o_ref.dtype)
    ssq_ref[...] = _row_ssq(o)


def _swa_attention(u, batch, seq, q_norm, k_norm, sinks, rel_bias):
    t = u.shape[0]
    nb = seq // BLOCK_Q
    nk = 2 * BLOCK_Q
    by_dist = rel_bias.astype(F32)[_t5_bucket(jnp.arange(WINDOW))].T
    row = jnp.concatenate([jnp.zeros((SWA_HEADS, 1), F32), by_dist], 1)
    bias = jnp.broadcast_to(row[:, None, :], (SWA_HEADS, nk, BLOCK_Q + 1)).reshape(SWA_HEADS, -1)
    bias = bias[:, :nk * BLOCK_Q].reshape(SWA_HEADS, nk, BLOCK_Q)
    bias = bias.transpose(1, 0, 2).reshape(nk, SWA_HEADS * BLOCK_Q) * LOG2E
    sinks = sinks * LOG2E
    gfold = jnp.tile(k_norm * q_norm * (SWA_HEAD_DIM ** -0.5 * LOG2E), SWA_KV_HEADS).reshape(1, SWA_KV_WIDTH)
    cur = lambda c: pl.BlockSpec((BLOCK_Q, SWA_KV_WIDTH), lambda b, n: (b * nb + n, c))
    prev = lambda c: pl.BlockSpec((BLOCK_Q, SWA_KV_WIDTH), lambda b, n: (b * nb + jnp.maximum(n - 1, 0), c))
    ck, cv = (U_MLA + S_K) // SWA_KV_WIDTH, (U_MLA + S_V) // SWA_KV_WIDTH
    gw = SWA_GROUP * SWA_HEAD_DIM
    q_specs = [pl.BlockSpec((BLOCK_Q, gw), functools.partial(lambda b, n, c: (b * nb + n, c), c=U_MLA // gw + kh))
               for kh in range(SWA_KV_HEADS)]
    blk = (_nbytes((BLOCK_Q, SWA_WIDTH), F32) + _nbytes((BLOCK_Q, SWA_WIDTH), BF16) + _nbytes((BLOCK_Q, LANES), F32)
           + 4 * _nbytes((BLOCK_Q, SWA_KV_WIDTH), F32) + _nbytes(bias.shape, F32))
    return pl.pallas_call(
        _swa_kernel,
        grid=(batch, nb),
        in_specs=[pl.BlockSpec(memory_space=pltpu.SMEM)] + q_specs + [
            prev(ck), cur(ck), prev(cv), cur(cv),
            pl.BlockSpec((1, SWA_KV_WIDTH), lambda b, n: (0, 0)),
            pl.BlockSpec(bias.shape, lambda b, n: (0, 0)),
        ],
        out_specs=[pl.BlockSpec((BLOCK_Q, SWA_WIDTH), lambda b, n: (b * nb + n, 0)),
                   pl.BlockSpec((BLOCK_Q, LANES), lambda b, n: (b * nb + n, 0))],
        out_shape=[jax.ShapeDtypeStruct((t, SWA_WIDTH), BF16), jax.ShapeDtypeStruct((t, LANES), F32)],
        compiler_params=_cparams(("parallel", "arbitrary"), blk, 4 * _nbytes((nk, SWA_HEADS * BLOCK_Q), F32)),
        name="swa_attention",
    )(sinks, u, u, u, u, u, u, u, u, gfold, bias)


def _mem_block_kernel(h_ref, kv_ref, gn_ref, wq_ref, wo_ref, gq_ref, gk_ref, gnext_ref, h_out_ref, xn_out_ref):
    h = h_ref[...]
    xn = _rms_rows(h, gn_ref[...]).astype(BF16)
    qm = jnp.dot(xn, wq_ref[...], preferred_element_type=F32)
    gq = gq_ref[...] * (MEM_HEAD_DIM ** -0.5)
    gk = gk_ref[...]
    outs = []
    for hd in range(MEM_HEADS):
        q = _rms_rows(qm[:, hd * MEM_HEAD_DIM:(hd + 1) * MEM_HEAD_DIM], gq).astype(BF16)
        lo = 2 * hd * MEM_HEAD_DIM
        k = _rms_rows(kv_ref[:, lo:lo + MEM_HEAD_DIM], gk).astype(BF16)
        v = kv_ref[:, lo + MEM_HEAD_DIM:lo + 2 * MEM_HEAD_DIM].astype(BF16)
        s = lax.dot_general(q, k, (((1,), (1,)), ((), ())), preferred_element_type=F32)
        e = jnp.exp(s - jnp.max(s, axis=-1, keepdims=True))
        p = (e / jnp.sum(e, axis=-1, keepdims=True)).astype(BF16)
        outs.append(jnp.dot(p, v, preferred_element_type=F32).astype(BF16))
    o = jnp.concatenate(outs, axis=-1)
    h_new = h + jnp.dot(o, wo_ref[...], preferred_element_type=F32)
    h_out_ref[...] = h_new
    xn_out_ref[...] = _rms_rows(h_new, gnext_ref[...]).astype(xn_out_ref.dtype)


def _mem_block(h, kvm, batch, seq, mem_len, norm_g, w_q, w_o, q_norm, k_norm, next_norm_g):
    t, d = h.shape
    w = w_q.shape[1]
    tm = _tile(seq, 256)
    nq = seq // tm
    once = pl.Buffered(1)
    blk = 2 * _nbytes((tm, d), F32) + _nbytes((tm, d), BF16) + _nbytes((mem_len, 2 * w), F32)
    temp = 2 * _nbytes((d, w), BF16) + 3 * _nbytes((tm, d), F32)
    return pl.pallas_call(
        _mem_block_kernel,
        grid=(batch, nq),
        in_specs=[
            pl.BlockSpec((tm, d), lambda b, i: (b * nq + i, 0)),
            pl.BlockSpec((mem_len, 2 * w), lambda b, i: (b, 0)),
            pl.BlockSpec((1, d), lambda b, i: (0, 0)),
            pl.BlockSpec((d, w), lambda b, i: (0, 0), pipeline_mode=once),
            pl.BlockSpec((w, d), lambda b, i: (0, 0), pipeline_mode=once),
            pl.BlockSpec((1, MEM_HEAD_DIM), lambda b, i: (0, 0)),
            pl.BlockSpec((1, MEM_HEAD_DIM), lambda b, i: (0, 0)),
            pl.BlockSpec((1, d), lambda b, i: (0, 0)),
        ],
        out_specs=[pl.BlockSpec((tm, d), lambda b, i: (b * nq + i, 0)), pl.BlockSpec((tm, d), lambda b, i: (b * nq + i, 0))],
        out_shape=[jax.ShapeDtypeStruct((t, d), F32), jax.ShapeDtypeStruct((t, d), BF16)],
        compiler_params=_cparams(("parallel", "arbitrary"), blk, temp),
        name="mem_block",
    )(h, kvm, norm_g.reshape(1, d), w_q.astype(BF16), w_o.astype(BF16), q_norm.reshape(1, -1), k_norm.reshape(1, -1),
      next_norm_g.reshape(1, d))


def _layer(h, mem, rel_bias, p, batch, seq):
    h, hb, ssq = _swiglu_half(h, _norm(h, p['ffn_a_norm']), p['ffn_a_gate'], p['ffn_a_up'], p['ffn_a_down'],
                              stats=True)

    w_in_t = p['w_in'].T
    u = _proj_t(hb, ssq, p['mix_norm'], w_in_t, n_first=U_MLA, second_start=IN_SWA,
                n_second=w_in_t.shape[0] - IN_SWA, tm=2048, tn=512, name="in_proj")

    q_t, k, v_t = _mla_prep(u, seq, p['mla_q_a_norm'], p['mla_kv_a_norm'], p['mla_w_uq'], p['mla_w_ukv'],
                            p['mla_q_norm'], p['mla_k_norm'])
    o_a, ssq_a = _mla_attention(q_t, k, v_t, batch, seq)
    ssq_a = ssq_a.reshape(ssq_a.shape[0], -1, LANES).sum(axis=1)
    o_b, ssq_b = _swa_attention(u, batch, seq, p['swa_q_norm'], p['swa_k_norm'], p['swa_sinks'], rel_bias)
    h = _proj2(o_a, o_b, ssq_a, ssq_b, p['out_norm_mla'], p['out_norm_swa'], p['w_out'], h,
               tm=2048, tn=256, name="out_proj")

    mem_len = mem.shape[0] // batch
    kvm = _mm_w32(_norm(mem, p['mem_norm']), p['mem_w_kv'], tm=512, tn=512, name="mem_kv_proj")
    h, xn = _mem_block(h, kvm, batch, seq, mem_len, p['mem_attn_norm'], p['mem_w_q'], p['mem_w_o'],
                       p['mem_q_norm'], p['mem_k_norm'], p['ffn_b_norm'])

    return _swiglu_half(h, xn, p['ffn_b_gate'], p['ffn_b_up'], p['ffn_b_down'])


_PARAM_NAMES = (
    'ffn_a_norm', 'ffn_a_gate', 'ffn_a_up', 'ffn_a_down', 'mix_norm', 'w_in',
    'mla_q_a_norm', 'mla_kv_a_norm', 'mla_w_uq', 'mla_w_ukv', 'mla_q_norm', 'mla_k_norm',
    'swa_q_norm', 'swa_k_norm', 'swa_sinks', 'out_norm_mla', 'out_norm_swa', 'w_out',
    'mem_attn_norm', 'mem_norm', 'mem_w_q', 'mem_w_kv', 'mem_q_norm', 'mem_k_norm', 'mem_w_o',
    'ffn_b_norm', 'ffn_b_gate', 'ffn_b_up', 'ffn_b_down',
)


def kernel(x, mem, rel_bias, ffn_a_norm, ffn_a_gate, ffn_a_up, ffn_a_down, mix_norm, w_in, mla_q_a_norm, mla_kv_a_norm, mla_w_uq, mla_w_ukv, mla_q_norm, mla_k_norm, swa_q_norm, swa_k_norm, swa_sinks, out_norm_mla, out_norm_swa, w_out, mem_attn_norm, mem_norm, mem_w_q, mem_w_kv, mem_q_norm, mem_k_norm, mem_w_o, ffn_b_norm, ffn_b_gate, ffn_b_up, ffn_b_down):
    stacked = (ffn_a_norm, ffn_a_gate, ffn_a_up, ffn_a_down, mix_norm, w_in, mla_q_a_norm, mla_kv_a_norm,
               mla_w_uq, mla_w_ukv, mla_q_norm, mla_k_norm, swa_q_norm, swa_k_norm, swa_sinks, out_norm_mla,
               out_norm_swa, w_out, mem_attn_norm, mem_norm, mem_w_q, mem_w_kv, mem_q_norm, mem_k_norm, mem_w_o,
               ffn_b_norm, ffn_b_gate, ffn_b_up, ffn_b_down)
    batch, seq, d = x.shape
    h = x.reshape(batch * seq, d)
    mem2 = mem.reshape(-1, d)
    for layer in range(ffn_a_norm.shape[0]):
        p = {name: arr[layer] for name, arr in zip(_PARAM_NAMES, stacked)}
        h = _layer(h, mem2, rel_bias, p, batch, seq)
    return h.reshape(batch, seq, d)
```

```python
import functools
import math

import jax
import jax.numpy as jnp
from jax import lax
from jax.experimental import pallas as pl
from jax.experimental.pallas import tpu as pltpu

F32 = jnp.float32
BF16 = jnp.bfloat16

MEM_HEADS = 4
MEM_HEAD_DIM = 128
MLA_HEADS = 16
MLA_Q_RANK = 896
MLA_KV_RANK = 512
MLA_NOPE = 128
MLA_ROPE = 64
MLA_V = 128
ROPE_THETA = 10000.0
SWA_HEADS = 32
SWA_KV_HEADS = 4
SWA_HEAD_DIM = 64
WINDOW = 128
REL_BUCKETS = 32
REL_MAX_DIST = 128
BLOCK_Q = 128
EPS = 1e-6
NEG = -1e30

MLA_QK = MLA_NOPE + MLA_ROPE
SWA_GROUP = SWA_HEADS // SWA_KV_HEADS
SWA_WIDTH = SWA_HEADS * SWA_HEAD_DIM
SWA_KV_WIDTH = SWA_KV_HEADS * SWA_HEAD_DIM
MLA_WIDTH = MLA_HEADS * MLA_V

LANES = 128
MLA_HEAD_PAD = 2 * LANES
ROPE_PAD = LANES

U_CKV = MLA_Q_RANK
U_KR = U_CKV + MLA_KV_RANK
U_MLA = U_KR + ROPE_PAD
IN_SWA = U_KR + MLA_ROPE
S_K = SWA_WIDTH
S_V = S_K + SWA_KV_WIDTH

LOG2E = math.log2(math.e)
KEY_BLOCK = 256

V7X_SCOPED_VMEM_BYTES = 60000 * 1024


def _tile(n, pref, align=8):
    if n <= pref:
        return n
    t = (pref // align) * align
    while t >= align:
        if n % t == 0:
            return t
        t -= align
    return n


def _nbytes(shape, dtype):
    n = 1
    for s in shape:
        n *= s
    return n * jnp.dtype(dtype).itemsize


def _cparams(semantics, pipelined_bytes, temp_bytes=0):
    need = 2 * pipelined_bytes + temp_bytes + (4 << 20)
    return pltpu.CompilerParams(
        dimension_semantics=semantics,
        vmem_limit_bytes=int(min(max(need, 16 << 20), V7X_SCOPED_VMEM_BYTES)),
    )


def _rms_rows(x, g, n=None):
    n = x.shape[-1] if n is None else n
    r = lax.rsqrt(jnp.sum(x * x, axis=-1, keepdims=True) / float(n) + EPS)
    return x * r * g


def _norm_kernel(x_ref, g_ref, o_ref):
    o_ref[...] = _rms_rows(x_ref[...], g_ref[...]).astype(o_ref.dtype)


def _norm(x, g, out_dtype=BF16):
    rows, d = x.shape
    tr = _tile(rows, 512)
    blk = _nbytes((tr, d), F32) + _nbytes((tr, d), out_dtype)
    return pl.pallas_call(
        _norm_kernel,
        grid=(rows // tr,),
        in_specs=[pl.BlockSpec((tr, d), lambda i: (i, 0)), pl.BlockSpec((1, d), lambda i: (0, 0))],
        out_specs=pl.BlockSpec((tr, d), lambda i: (i, 0)),
        out_shape=jax.ShapeDtypeStruct((rows, d), out_dtype),
        compiler_params=_cparams(("parallel",), blk, _nbytes((tr, d), F32)),
        name="rmsnorm",
    )(x, g.reshape(1, d))


def _mm_kernel(a_ref, w_ref, o_ref):
    o_ref[...] = jnp.dot(a_ref[...], w_ref[...], preferred_element_type=F32).astype(o_ref.dtype)


def _mm_res_kernel(a_ref, w_ref, r_ref, o_ref):
    o_ref[...] = r_ref[...] + jnp.dot(a_ref[...], w_ref[...], preferred_element_type=F32)


def _mm(a, w, res=None, *, tm, tn, out_dtype=F32, name="matmul"):
    m, k = a.shape
    n = w.shape[1]
    tm, tn = _tile(m, tm), _tile(n, tn, LANES)
    blk = _nbytes((tm, k), a.dtype) + _nbytes((k, tn), w.dtype) + _nbytes((tm, tn), out_dtype)
    in_specs = [pl.BlockSpec((tm, k), lambda j, i: (i, 0)), pl.BlockSpec((k, tn), lambda j, i: (0, j))]
    args = [a, w]
    kern = _mm_kernel
    if res is not None:
        in_specs.append(pl.BlockSpec((tm, tn), lambda j, i: (i, j)))
        args.append(res)
        blk += _nbytes((tm, tn), F32)
        kern = _mm_res_kernel
    return pl.pallas_call(
        kern,
        grid=(n // tn, m // tm),
        in_specs=in_specs,
        out_specs=pl.BlockSpec((tm, tn), lambda j, i: (i, j)),
        out_shape=jax.ShapeDtypeStruct((m, n), out_dtype),
        compiler_params=_cparams(("parallel", "arbitrary"), blk, 2 * _nbytes((tm, tn), F32)),
        name=name,
    )(*args)


def _row_ssq(x):
    sq = x * x
    part = sq[:, :LANES]
    for c in range(1, x.shape[1] // LANES):
        part = part + sq[:, c * LANES:(c + 1) * LANES]
    return part


def _inv_rms(ssq_block, n):
    return lax.rsqrt(jnp.sum(ssq_block, axis=-1, keepdims=True) / float(n) + EPS)


def _mm_res_stats_kernel(a_ref, w_ref, r_ref, o_ref, ob_ref, ssq_ref):
    h = r_ref[...] + jnp.dot(a_ref[...], w_ref[...], preferred_element_type=F32)
    o_ref[...] = h
    ob_ref[...] = h.astype(ob_ref.dtype)
    ssq_ref[...] = _row_ssq(h)


def _mm_res_stats(a, w, res, *, tm, tn, name):
    m, k = a.shape
    n = w.shape[1]
    tm, tn = _tile(m, tm), _tile(n, tn, LANES)
    blk = (_nbytes((tm, k), a.dtype) + _nbytes((k, tn), w.dtype) + 2 * _nbytes((tm, tn), F32) + _nbytes((tm, tn), BF16)
           + _nbytes((tm, LANES), F32))
    return pl.pallas_call(
        _mm_res_stats_kernel,
        grid=(n // tn, m // tm),
        in_specs=[
            pl.BlockSpec((tm, k), lambda j, i: (i, 0)),
            pl.BlockSpec((k, tn), lambda j, i: (0, j)),
            pl.BlockSpec((tm, tn), lambda j, i: (i, j)),
        ],
        out_specs=[
            pl.BlockSpec((tm, tn), lambda j, i: (i, j)),
            pl.BlockSpec((tm, tn), lambda j, i: (i, j)),
            pl.BlockSpec((tm, LANES), lambda j, i: (i, j)),
        ],
        out_shape=[jax.ShapeDtypeStruct((m, n), F32), jax.ShapeDtypeStruct((m, n), BF16),
                   jax.ShapeDtypeStruct((m, (n // tn) * LANES), F32)],
        compiler_params=_cparams(("parallel", "arbitrary"), blk, 3 * _nbytes((tm, tn), F32)),
        name=name,
    )(a, w, res)


def _proj_t_kernel(a_ref, ssq_ref, g_ref, wt_ref, o_ref, r_ref):
    @pl.when(pl.program_id(1) == 0)
    def _():
        r_ref[...] = jnp.broadcast_to(_inv_rms(ssq_ref[...], a_ref.shape[1]), r_ref.shape)

    wb = (wt_ref[...] * g_ref[...]).astype(BF16)
    acc = lax.dot_general(a_ref[...], wb, (((1,), (1,)), ((), ())), preferred_element_type=F32)
    o_ref[...] = acc * r_ref[:, :1]


def _proj_t(a, ssq, g, wt, *, n_first, second_start, n_second, tm, tn, name):
    m, k = a.shape
    tm = _tile(m, tm)
    t1 = n_first // tn
    nt = t1 + n_second // tn
    shift = second_start - n_first
    wt_map = lambda i, j: (pl.multiple_of(j * tn + jnp.where(j >= t1, shift, 0), 8), 0)
    once = pl.Buffered(1)
    blk = _nbytes((tn, k), F32) + _nbytes((tm, tn), F32)
    return pl.pallas_call(
        _proj_t_kernel,
        grid=(m // tm, nt),
        in_specs=[
            pl.BlockSpec((tm, k), lambda i, j: (i, 0), pipeline_mode=once),
            pl.BlockSpec((tm, ssq.shape[1]), lambda i, j: (i, 0), pipeline_mode=once),
            pl.BlockSpec((1, k), lambda i, j: (0, 0)),
            pl.BlockSpec((pl.Element(tn), pl.Element(k)), wt_map),
        ],
        out_specs=pl.BlockSpec((tm, tn), lambda i, j: (i, j)),
        out_shape=jax.ShapeDtypeStruct((m, nt * tn), F32),
        scratch_shapes=[pltpu.VMEM((tm, LANES), F32)],
        compiler_params=_cparams(("parallel", "arbitrary"), blk,
                                 _nbytes((tm, k), BF16) + _nbytes((tm, ssq.shape[1]), F32) + _nbytes((tn, k), BF16)
                                 + 2 * _nbytes((tm, tn), F32)),
        name=name,
    )(a, ssq, g.reshape(1, k), wt)


def _proj2_kernel(a_ref, b_ref, ssqa_ref, ssqb_ref, gt_ref, w_ref, r_ref, o_ref):
    ka = a_ref.shape[1]
    wb = (w_ref[...] * gt_ref[...]).astype(BF16)
    ya = jnp.dot(a_ref[...], wb[:ka], preferred_element_type=F32) * _inv_rms(ssqa_ref[...], ka)
    yb = jnp.dot(b_ref[...], wb[ka:], preferred_element_type=F32) * _inv_rms(ssqb_ref[...], b_ref.shape[1])
    o_ref[...] = r_ref[...] + ya + yb


def _proj2(a, b, ssqa, ssqb, ga, gb, w, res, *, tm, tn, name):
    m, ka = a.shape
    kb = b.shape[1]
    n = w.shape[1]
    tm, tn = _tile(m, tm), _tile(n, tn, LANES)
    gtab = jnp.broadcast_to(jnp.concatenate([ga, gb])[:, None], (ka + kb, tn))
    once = pl.Buffered(1)
    blk = _nbytes((ka + kb, tn), F32) * 2 + 2 * _nbytes((tm, tn), F32) + _nbytes((tm, ssqa.shape[1] + LANES), F32)
    return pl.pallas_call(
        _proj2_kernel,
        grid=(m // tm, n // tn),
        in_specs=[
            pl.BlockSpec((tm, ka), lambda i, j: (i, 0), pipeline_mode=once),
            pl.BlockSpec((tm, kb), lambda i, j: (i, 0), pipeline_mode=once),
            pl.BlockSpec((tm, ssqa.shape[1]), lambda i, j: (i, 0)),
            pl.BlockSpec((tm, LANES), lambda i, j: (i, 0)),
            pl.BlockSpec((ka + kb, tn), lambda i, j: (0, 0), pipeline_mode=once),
            pl.BlockSpec((ka + kb, tn), lambda i, j: (0, j)),
            pl.BlockSpec((tm, tn), lambda i, j: (i, j)),
        ],
        out_specs=pl.BlockSpec((tm, tn), lambda i, j: (i, j)),
        out_shape=jax.ShapeDtypeStruct((m, n), F32),
        compiler_params=_cparams(("parallel", "arbitrary"), blk,
                                 _nbytes((tm, ka + kb), BF16) + _nbytes((ka + kb, tn), BF16) + 3 * _nbytes((tm, tn), F32)),
        name=name,
    )(a, b, ssqa, ssqb, gtab, w, res)


def _mm_w32_kernel(a_ref, w_ref, *rest, has_res):
    if has_res:
        r_ref, o_ref, wb_ref = rest
    else:
        o_ref, wb_ref = rest

    @pl.when(pl.program_id(1) == 0)
    def _():
        wb_ref[...] = w_ref[...].astype(BF16)

    acc = jnp.dot(a_ref[...], wb_ref[...], preferred_element_type=F32)
    if has_res:
        acc = r_ref[...] + acc
    o_ref[...] = acc.astype(o_ref.dtype)


def _mm_w32(a, w, res=None, *, tm, tn, n=None, out_dtype=F32, name="matmul"):
    m, k = a.shape
    n = w.shape[1] if n is None else n
    tm, tn = _tile(m, tm), _tile(n, tn, LANES)
    blk = _nbytes((tm, k), a.dtype) + _nbytes((k, tn), F32) + _nbytes((tm, tn), out_dtype)
    in_specs = [pl.BlockSpec((tm, k), lambda j, i: (i, 0)), pl.BlockSpec((k, tn), lambda j, i: (0, j))]
    args = [a, w]
    if res is not None:
        in_specs.append(pl.BlockSpec((tm, tn), lambda j, i: (i, j)))
        args.append(res)
        blk += _nbytes((tm, tn), F32)
    return pl.pallas_call(
        functools.partial(_mm_w32_kernel, has_res=res is not None),
        grid=(n // tn, m // tm),
        in_specs=in_specs,
        out_specs=pl.BlockSpec((tm, tn), lambda j, i: (i, j)),
        out_shape=jax.ShapeDtypeStruct((m, n), out_dtype),
        scratch_shapes=[pltpu.VMEM((k, tn), BF16)],
        compiler_params=_cparams(("parallel", "arbitrary"), blk, _nbytes((k, tn), BF16) + 2 * _nbytes((tm, tn), F32)),
        name=name,
    )(*args)


def _gated_kernel(x_ref, wg_ref, wu_ref, wd_ref, o_ref, wdb_ref):
    @pl.when(pl.program_id(0) == 0)
    def _():
        wdb_ref[...] = wd_ref[...].astype(BF16)

    x = x_ref[...]
    g = jnp.dot(x, wg_ref[...].astype(BF16), preferred_element_type=F32)
    u = jnp.dot(x, wu_ref[...].astype(BF16), preferred_element_type=F32)
    o_ref[...] = (0.25 * g * (1.0 + jnp.tanh(0.5 * g)) * u).astype(o_ref.dtype)


def _gated(x, wg, wu, wd, *, tm, tn):
    m, k = x.shape
    n = wg.shape[1]
    d_out = wd.shape[1]
    tm, tn = _tile(m, tm), _tile(n, tn, LANES)
    nj = n // tn
    blk = (2 * _nbytes((k, tn), F32) + _nbytes((tm, tn), BF16)
           + _nbytes((tn, d_out), F32) + _nbytes((tn, d_out), BF16))
    wd_map = lambda i, j: (jnp.where(i == 0, j, nj - 1), 0)
    return pl.pallas_call(
        _gated_kernel,
        grid=(m // tm, nj),
        in_specs=[
            pl.BlockSpec((tm, k), lambda i, j: (i, 0), pipeline_mode=pl.Buffered(1)),
            pl.BlockSpec((k, tn), lambda i, j: (0, j)),
            pl.BlockSpec((k, tn), lambda i, j: (0, j)),
            pl.BlockSpec((tn, d_out), wd_map),
        ],
        out_specs=[pl.BlockSpec((tm, tn), lambda i, j: (i, j)), pl.BlockSpec((tn, d_out), wd_map)],
        out_shape=[jax.ShapeDtypeStruct((m, n), BF16), jax.ShapeDtypeStruct((n, d_out), BF16)],
        compiler_params=_cparams(("arbitrary", "arbitrary"), blk,
                                 _nbytes((tm, k), BF16) + 2 * _nbytes((k, tn), BF16) + 3 * _nbytes((tm, tn), F32)),
        name="ffn_gate_up",
    )(x, wg, wu, wd)


def _swiglu_half(h, xn, wg, wu, wd, stats=False):
    act, wd_bf = _gated(xn, wg, wu, wd, tm=2048, tn=256)
    if stats:
        return _mm_res_stats(act, wd_bf, h, tm=512, tn=512, name="ffn_down_stats")
    return _mm(act, wd_bf, h, tm=512, tn=512, name="ffn_down")


def _rope_chunk(y, cos, s1, s2):
    return y * cos + pltpu.roll(y, ROPE_PAD - MLA_ROPE // 2, 1) * s1 + pltpu.roll(y, MLA_ROPE // 2, 1) * s2


def _mla_q_kernel(u_ref, ga_ref, wt_ref, gt_ref, cos_ref, sin_ref, o_ref):
    cn = _rms_rows(u_ref[:, :MLA_Q_RANK], ga_ref[...]).astype(BF16)
    qt = lax.dot_general(wt_ref[...], cn, (((1,), (1,)), ((), ())), preferred_element_type=F32)
    cos, sin, gain = cos_ref[...], sin_ref[...], gt_ref[...]
    half = MLA_ROPE // 2
    zeros = jnp.zeros((MLA_HEAD_PAD - MLA_QK, qt.shape[1]), o_ref.dtype)
    for h in range(MLA_HEADS):
        lo = h * MLA_HEAD_PAD
        c = qt[lo:lo + MLA_HEAD_PAD, :]
        y = c * lax.rsqrt(jnp.sum(c * c, axis=0, keepdims=True) / float(MLA_QK) + EPS) * gain
        x1, x2 = y[MLA_NOPE:MLA_NOPE + half, :], y[MLA_NOPE + half:MLA_QK, :]
        o_ref[lo:lo + MLA_NOPE, :] = y[:MLA_NOPE, :].astype(o_ref.dtype)
        o_ref[lo + MLA_NOPE:lo + MLA_NOPE + half, :] = (x1 * cos - x2 * sin).astype(o_ref.dtype)
        o_ref[lo + MLA_NOPE + half:lo + MLA_QK, :] = (x2 * cos + x1 * sin).astype(o_ref.dtype)
        o_ref[lo + MLA_QK:lo + MLA_HEAD_PAD, :] = zeros


def _mla_kv_kernel(u_ref, ga_ref, wk_ref, wvt_ref, gn_ref, gr_ref, cos_ref, s1_ref, s2_ref, k_ref, vt_ref):
    cn = _rms_rows(u_ref[:, U_CKV:U_KR], ga_ref[...]).astype(BF16)
    kv = jnp.dot(cn, wk_ref[...], preferred_element_type=F32)
    vt = lax.dot_general(wvt_ref[...], cn, (((1,), (1,)), ((), ())), preferred_element_type=F32)
    vt_ref[...] = vt.astype(vt_ref.dtype)
    kr = u_ref[:, U_KR:U_MLA]
    kr = jnp.where(lax.broadcasted_iota(jnp.int32, kr.shape, 1) < MLA_ROPE, kr, 0.0)
    ss_r = jnp.sum(kr * kr, axis=-1, keepdims=True)
    gn = gn_ref[...]
    kr_rot = _rope_chunk(kr * gr_ref[...], cos_ref[...], s1_ref[...], s2_ref[...])
    for h in range(MLA_HEADS):
        kn = kv[:, h * MLA_NOPE:(h + 1) * MLA_NOPE]
        r = lax.rsqrt((jnp.sum(kn * kn, axis=-1, keepdims=True) + ss_r) / float(MLA_QK) + EPS)
        lo = h * MLA_HEAD_PAD
        k_ref[:, lo:lo + MLA_NOPE] = (kn * r * gn).astype(k_ref.dtype)
        k_ref[:, lo + MLA_NOPE:lo + MLA_HEAD_PAD] = (kr_rot * r).astype(k_ref.dtype)


def _mla_prep_kernel(u_ref, gaq_ref, wqt_ref, gt_ref, cost_ref, sint_ref, gakv_ref, wk_ref, wvt_ref, gn_ref, gr_ref,
                     cos_ref, s1_ref, s2_ref, qt_ref, k_ref, vt_ref):
    _mla_q_kernel(u_ref, gaq_ref, wqt_ref, gt_ref, cost_ref, sint_ref, qt_ref)
    _mla_kv_kernel(u_ref, gakv_ref, wk_ref, wvt_ref, gn_ref, gr_ref, cos_ref, s1_ref, s2_ref, k_ref, vt_ref)


def _rope_tables(seq):
    half = MLA_ROPE // 2
    inv = ROPE_THETA ** (-jnp.arange(half, dtype=F32) / half)
    ang = jnp.arange(seq).astype(F32)[:, None] * inv[None, :]
    cos, sin = jnp.cos(ang), jnp.sin(ang)
    z = jnp.zeros_like(cos)
    pad = jnp.zeros((seq, ROPE_PAD - MLA_ROPE), F32)
    return (jnp.concatenate([cos, cos, pad], 1), jnp.concatenate([-sin, z, pad], 1),
            jnp.concatenate([z, sin, pad], 1), cos.T, sin.T)


def _mla_prep(u, seq, q_a_norm, kv_a_norm, w_uq, w_ukv, q_norm, k_norm):
    t = u.shape[0]
    tm = _tile(seq, KEY_BLOCK)
    spb = seq // tm
    cos, s1, s2, cos_t, sin_t = _rope_tables(seq)
    tab_spec = pl.BlockSpec((tm, ROPE_PAD), lambda i: (i % spb, 0))
    tab_t_spec = pl.BlockSpec((MLA_ROPE // 2, tm), lambda i: (0, i % spb))
    qw = MLA_HEADS * MLA_HEAD_PAD
    vw = MLA_HEADS * MLA_V

    wq_t = jnp.pad(w_uq.reshape(MLA_Q_RANK, MLA_HEADS, MLA_QK), ((0, 0), (0, 0), (0, MLA_HEAD_PAD - MLA_QK)))
    wq_t = wq_t.reshape(MLA_Q_RANK, qw).T.astype(BF16)
    gain_t = jnp.concatenate([q_norm * (MLA_QK ** -0.5 * LOG2E), jnp.zeros((MLA_HEAD_PAD - MLA_QK,), F32)])
    gain_t = jnp.broadcast_to(gain_t[:, None], (MLA_HEAD_PAD, tm))
    wkv = w_ukv.reshape(MLA_KV_RANK, MLA_HEADS, MLA_NOPE + MLA_V)
    wk = wkv[:, :, :MLA_NOPE].reshape(MLA_KV_RANK, -1).astype(BF16)
    wv_t = wkv[:, :, MLA_NOPE:].reshape(MLA_KV_RANK, -1).T.astype(BF16)
    blk = (_nbytes((tm, U_MLA), F32) + _nbytes(wq_t.shape, BF16) + _nbytes((qw, tm), BF16)
           + _nbytes((MLA_HEAD_PAD, tm), F32) + 2 * _nbytes((MLA_ROPE // 2, tm), F32)
           + _nbytes(wk.shape, BF16) + _nbytes(wv_t.shape, BF16) + _nbytes((tm, qw), BF16)
           + _nbytes((vw, tm), BF16) + 3 * _nbytes((tm, ROPE_PAD), F32))
    const = lambda shape: pl.BlockSpec(shape, lambda i: (0, 0))
    return pl.pallas_call(
        _mla_prep_kernel,
        grid=(t // tm,),
        in_specs=[
            pl.BlockSpec((tm, U_MLA), lambda i: (i, 0)),
            const((1, MLA_Q_RANK)), const(wq_t.shape), const((MLA_HEAD_PAD, tm)), tab_t_spec, tab_t_spec,
            const((1, MLA_KV_RANK)), const(wk.shape), const(wv_t.shape), const((1, MLA_NOPE)), const((1, ROPE_PAD)),
            tab_spec, tab_spec, tab_spec,
        ],
        out_specs=[pl.BlockSpec((qw, tm), lambda i: (0, i)), pl.BlockSpec((tm, qw), lambda i: (i, 0)),
                   pl.BlockSpec((None, vw, tm), lambda i: (i, 0, 0))],
        out_shape=[jax.ShapeDtypeStruct((qw, t), BF16), jax.ShapeDtypeStruct((t, qw), BF16),
                   jax.ShapeDtypeStruct((t // tm, vw, tm), BF16)],
        compiler_params=_cparams(("parallel",), blk, 4 * _nbytes((qw, tm), F32)),
        name="mla_prep",
    )(u, q_a_norm.reshape(1, -1), wq_t, gain_t, cos_t, sin_t,
      kv_a_norm.reshape(1, -1), wk, wv_t, k_norm[:MLA_NOPE].reshape(1, -1),
      jnp.concatenate([k_norm[MLA_NOPE:], jnp.zeros((ROPE_PAD - MLA_ROPE,), F32)]).reshape(1, -1), cos, s1, s2)


def _mla_attn_kernel(qt_ref, k_ref, vt_ref, o_ref, ssq_ref, sa_ref, sb_ref, *, tq, kb):
    per = tq // kb
    nq = qt_ref.shape[1] // tq
    nh = qt_ref.shape[0] // MLA_HEAD_PAD

    def scores(qi, blk, s_ref):
        for hh in range(nh):
            qt = qt_ref[hh * MLA_HEAD_PAD:(hh + 1) * MLA_HEAD_PAD, qi * tq:(qi + 1) * tq]
            for d in range(per):
                start = (blk * per + d) * kb
                kblk = k_ref[start:start + kb, hh * MLA_HEAD_PAD:(hh + 1) * MLA_HEAD_PAD]
                s = jnp.dot(kblk, qt, preferred_element_type=F32)
                if blk == qi:
                    key = d * kb + lax.broadcasted_iota(jnp.int32, s.shape, 0)
                    qry = lax.broadcasted_iota(jnp.int32, s.shape, 1)
                    s = jnp.where(key <= qry, s, NEG)
                s_ref[hh, d * kb:(d + 1) * kb, :] = s

    def softmax_pv(blk, s_ref, carries):
        out = []
        for hh in range(nh):
            m, l, acc = carries[hh]
            ss = [s_ref[hh, d * kb:(d + 1) * kb, :] for d in range(per)]
            m_new = m
            for s in ss:
                m_new = jnp.maximum(m_new, jnp.max(s, axis=0, keepdims=True))
            alpha = jnp.exp2(m - m_new)
            l = alpha * l
            acc = alpha * acc
            for d, s in enumerate(ss):
                p = jnp.exp2(s - m_new)
                l = l + jnp.sum(p, axis=0, keepdims=True)
                vt = vt_ref[blk * per + d, hh * MLA_V:(hh + 1) * MLA_V, :]
                acc = acc + jnp.dot(vt, p.astype(BF16), preferred_element_type=F32)
            out.append((m_new, l, acc))
        return tuple(out)

    items = [(qi, blk) for qi in range(nq) for blk in range(qi + 1)]
    bufs = (sa_ref, sb_ref)
    scores(*items[0], bufs[0])
    carries = None
    for idx, (qi, blk) in enumerate(items):
        if idx + 1 < len(items):
            scores(*items[idx + 1], bufs[(idx + 1) % 2])
        if blk == 0:
            carries = tuple((jnp.full((1, tq), NEG, F32), jnp.zeros((1, tq), F32), jnp.zeros((MLA_V, tq), F32))
                            for _ in range(nh))
        carries = softmax_pv(blk, bufs[idx % 2], carries)
        if blk == qi:
            ssq = None
            for hh in range(nh):
                _, l, acc = carries[hh]
                o = (acc * (1.0 / l)).T
                o_ref[qi * tq:(qi + 1) * tq, hh * MLA_V:(hh + 1) * MLA_V] = o.astype(o_ref.dtype)
                ssq = _row_ssq(o) if ssq is None else ssq + _row_ssq(o)
            ssq_ref[qi * tq:(qi + 1) * tq, :] = ssq


def _mla_attention(q_t, k, v_t, batch, seq):
    t = k.shape[0]
    kb = v_t.shape[2]
    tq = _tile(seq, 512)
    nkb = seq // kb
    nh = 2
    blk = nh * (2 * _nbytes((seq, MLA_HEAD_PAD), BF16) + _nbytes((seq, MLA_V), BF16) + _nbytes((seq, MLA_V), F32))
    return pl.pallas_call(
        functools.partial(_mla_attn_kernel, tq=tq, kb=kb),
        grid=(batch, MLA_HEADS // nh),
        in_specs=[
            pl.BlockSpec((nh * MLA_HEAD_PAD, seq), lambda b, h: (h, b)),
            pl.BlockSpec((seq, nh * MLA_HEAD_PAD), lambda b, h: (b, h)),
            pl.BlockSpec((nkb, nh * MLA_V, kb), lambda b, h: (b, h, 0)),
        ],
        out_specs=[pl.BlockSpec((seq, nh * MLA_V), lambda b, h: (b, h)), pl.BlockSpec((seq, LANES), lambda b, h: (b, h))],
        out_shape=[jax.ShapeDtypeStruct((t, MLA_WIDTH), BF16),
                   jax.ShapeDtypeStruct((t, (MLA_HEADS // nh) * LANES), F32)],
        scratch_shapes=[pltpu.VMEM((nh, tq, tq), F32), pltpu.VMEM((nh, tq, tq), F32)],
        compiler_params=_cparams(("parallel", "parallel"), blk, 10 * nh * _nbytes((tq, tq), F32)),
        name="mla_attention",
    )(q_t, k, v_t)


def _t5_bucket(dist):
    max_exact = REL_BUCKETS // 2
    d = jnp.maximum(dist, 1).astype(F32)
    large = max_exact + (jnp.log(d / max_exact) / math.log(REL_MAX_DIST / max_exact)
                         * (REL_BUCKETS - max_exact)).astype(jnp.int32)
    large = jnp.minimum(large, REL_BUCKETS - 1)
    return jnp.where(dist < max_exact, dist, large)


def _swa_kernel(sink_ref, q0_ref, q1_ref, q2_ref, q3_ref, kp_ref, kc_ref, vp_ref, vc_ref, gfold_ref, bias_ref,
                o_ref, ssq_ref):
    n = pl.program_id(1)
    nk, hd, kvw = 2 * BLOCK_Q, SWA_HEAD_DIM, SWA_KV_WIDTH
    key = lax.broadcasted_iota(jnp.int32, (nk, BLOCK_Q), 0)
    qry = lax.broadcasted_iota(jnp.int32, (nk, BLOCK_Q), 1)
    dist = qry + BLOCK_Q - key
    valid = (dist >= 0) & (dist < WINDOW) & ((key >= BLOCK_Q) | (n > 0))

    k2 = jnp.concatenate([kp_ref[...], kc_ref[...]], axis=0)
    sq = k2 * k2
    sq_hi = sq.astype(BF16)
    sq_lo = (sq - sq_hi.astype(F32)).astype(BF16)
    shift = hd.bit_length() - 1
    ri = lax.broadcasted_iota(jnp.int32, (kvw, kvw), 0) >> shift
    ci = lax.broadcasted_iota(jnp.int32, (kvw, kvw), 1) >> shift
    ind = jnp.where(ri == ci, 1.0, 0.0).astype(BF16)
    ss = jnp.dot(sq_hi, ind, preferred_element_type=F32) + jnp.dot(sq_lo, ind, preferred_element_type=F32)
    k2n = (k2 * lax.rsqrt(ss / float(hd) + EPS) * gfold_ref[...]).astype(BF16)
    v2t = jnp.concatenate([vp_ref[...], vc_ref[...]], axis=0).T.astype(BF16)

    qt = jnp.concatenate([q0_ref[...], q1_ref[...], q2_ref[...], q3_ref[...]], axis=1).T
    gw = SWA_GROUP * BLOCK_Q
    rows = []
    for kh in range(SWA_KV_HEADS):
        pieces = []
        if kh > 0:
            pieces.append(jnp.zeros((hd, kh * gw), BF16))
        for g in range(SWA_GROUP):
            j = kh * SWA_GROUP + g
            c = qt[j * hd:(j + 1) * hd, :]
            pieces.append((c * lax.rsqrt(jnp.sum(c * c, axis=0, keepdims=True) / float(hd) + EPS)).astype(BF16))
        if kh < SWA_KV_HEADS - 1:
            pieces.append(jnp.zeros((hd, (SWA_KV_HEADS - 1 - kh) * gw), BF16))
        rows.append(jnp.concatenate(pieces, axis=1))
    q_bd = jnp.concatenate(rows, axis=0)
    s_all = jnp.dot(k2n, q_bd, preferred_element_type=F32)

    es, invs = [], []
    for j in range(SWA_HEADS):
        lo = j * BLOCK_Q
        s = jnp.where(valid, s_all[:, lo:lo + BLOCK_Q] + bias_ref[:, lo:lo + BLOCK_Q], NEG)
        sink = sink_ref[j]
        m = jnp.maximum(jnp.max(s, axis=0, keepdims=True), sink)
        e = jnp.exp2(s - m)
        invs.append(1.0 / (jnp.sum(e, axis=0, keepdims=True) + jnp.exp2(sink - m)))
        es.append(e.astype(BF16))
    e_all = jnp.concatenate(es, axis=1)
    o_all = jnp.dot(v2t, e_all, preferred_element_type=F32)
    outs = []
    for j in range(SWA_HEADS):
        kh = j // SWA_GROUP
        outs.append(o_all[kh * hd:(kh + 1) * hd, j * BLOCK_Q:(j + 1) * BLOCK_Q] * invs[j])
    o = jnp.concatenate(outs, axis=0).T
    o_ref[...] = o.astype(o_ref.dtype)
    ssq_ref[...] = _row_ssq(o)


def _swa_attention(u, batch, seq, q_norm, k_norm, sinks, rel_bias):
    t = u.shape[0]
    nb = seq // BLOCK_Q
    nk = 2 * BLOCK_Q
    by_dist = rel_bias.astype(F32)[_t5_bucket(jnp.arange(WINDOW))].T
    row = jnp.concatenate([jnp.zeros((SWA_HEADS, 1), F32), by_dist], 1)
    bias = jnp.broadcast_to(row[:, None, :], (SWA_HEADS, nk, BLOCK_Q + 1)).reshape(SWA_HEADS, -1)
    bias = bias[:, :nk * BLOCK_Q].reshape(SWA_HEADS, nk, BLOCK_Q)
    bias = bias.transpose(1, 0, 2).reshape(nk, SWA_HEADS * BLOCK_Q) * LOG2E
    sinks = sinks * LOG2E
    gfold = jnp.tile(k_norm * q_norm * (SWA_HEAD_DIM ** -0.5 * LOG2E), SWA_KV_HEADS).reshape(1, SWA_KV_WIDTH)
    cur = lambda c: pl.BlockSpec((BLOCK_Q, SWA_KV_WIDTH), lambda b, n: (b * nb + n, c))
    prev = lambda c: pl.BlockSpec((BLOCK_Q, SWA_KV_WIDTH), lambda b, n: (b * nb + jnp.maximum(n - 1, 0), c))
    ck, cv = (U_MLA + S_K) // SWA_KV_WIDTH, (U_MLA + S_V) // SWA_KV_WIDTH
    gw = SWA_GROUP * SWA_HEAD_DIM
    q_specs = [pl.BlockSpec((BLOCK_Q, gw), functools.partial(lambda b, n, c: (b * nb + n, c), c=U_MLA // gw + kh))
               for kh in range(SWA_KV_HEADS)]
    blk = (_nbytes((BLOCK_Q, SWA_WIDTH), F32) + _nbytes((BLOCK_Q, SWA_WIDTH), BF16) + _nbytes((BLOCK_Q, LANES), F32)
           + 4 * _nbytes((BLOCK_Q, SWA_KV_WIDTH), F32) + _nbytes(bias.shape, F32))
    return pl.pallas_call(
        _swa_kernel,
        grid=(batch, nb),
        in_specs=[pl.BlockSpec(memory_space=pltpu.SMEM)] + q_specs + [
            prev(ck), cur(ck), prev(cv), cur(cv),
            pl.BlockSpec((1, SWA_KV_WIDTH), lambda b, n: (0, 0)),
            pl.BlockSpec(bias.shape, lambda b, n: (0, 0)),
        ],
        out_specs=[pl.BlockSpec((BLOCK_Q, SWA_WIDTH), lambda b, n: (b * nb + n, 0)),
                   pl.BlockSpec((BLOCK_Q, LANES), lambda b, n: (b * nb + n, 0))],
        out_shape=[jax.ShapeDtypeStruct((t, SWA_WIDTH), BF16), jax.ShapeDtypeStruct((t, LANES), F32)],
        compiler_params=_cparams(("parallel", "arbitrary"), blk, 4 * _nbytes((nk, SWA_HEADS * BLOCK_Q), F32)),
        name="swa_attention",
    )(sinks, u, u, u, u, u, u, u, u, gfold, bias)


def _mem_block_kernel(h_ref, kv_ref, gn_ref, wq_ref, wo_ref, gq_ref, gk_ref, gnext_ref, h_out_ref, xn_out_ref):
    h = h_ref[...]
    xn = _rms_rows(h, gn_ref[...]).astype(BF16)
    qm = jnp.dot(xn, wq_ref[...], preferred_element_type=F32)
    gq = gq_ref[...] * (MEM_HEAD_DIM ** -0.5)
    gk = gk_ref[...]
    outs = []
    for hd in range(MEM_HEADS):
        q = _rms_rows(qm[:, hd * MEM_HEAD_DIM:(hd + 1) * MEM_HEAD_DIM], gq).astype(BF16)
        lo = 2 * hd * MEM_HEAD_DIM
        k = _rms_rows(kv_ref[:, lo:lo + MEM_HEAD_DIM], gk).astype(BF16)
        v = kv_ref[:, lo + MEM_HEAD_DIM:lo + 2 * MEM_HEAD_DIM].astype(BF16)
        s = lax.dot_general(q, k, (((1,), (1,)), ((), ())), preferred_element_type=F32)
        e = jnp.exp(s - jnp.max(s, axis=-1, keepdims=True))
        p = (e / jnp.sum(e, axis=-1, keepdims=True)).astype(BF16)
        outs.append(jnp.dot(p, v, preferred_element_type=F32).astype(BF16))
    o = jnp.concatenate(outs, axis=-1)
    h_new = h + jnp.dot(o, wo_ref[...], preferred_element_type=F32)
    h_out_ref[...] = h_new
    xn_out_ref[...] = _rms_rows(h_new, gnext_ref[...]).astype(xn_out_ref.dtype)


def _mem_block(h, kvm, batch, seq, mem_len, norm_g, w_q, w_o, q_norm, k_norm, next_norm_g):
    t, d = h.shape
    w = w_q.shape[1]
    tm = _tile(seq, 256)
    nq = seq // tm
    once = pl.Buffered(1)
    blk = 2 * _nbytes((tm, d), F32) + _nbytes((tm, d), BF16) + _nbytes((mem_len, 2 * w), F32)
    temp = 2 * _nbytes((d, w), BF16) + 3 * _nbytes((tm, d), F32)
    return pl.pallas_call(
        _mem_block_kernel,
        grid=(batch, nq),
        in_specs=[
            pl.BlockSpec((tm, d), lambda b, i: (b * nq + i, 0)),
            pl.BlockSpec((mem_len, 2 * w), lambda b, i: (b, 0)),
            pl.BlockSpec((1, d), lambda b, i: (0, 0)),
            pl.BlockSpec((d, w), lambda b, i: (0, 0), pipeline_mode=once),
            pl.BlockSpec((w, d), lambda b, i: (0, 0), pipeline_mode=once),
            pl.BlockSpec((1, MEM_HEAD_DIM), lambda b, i: (0, 0)),
            pl.BlockSpec((1, MEM_HEAD_DIM), lambda b, i: (0, 0)),
            pl.BlockSpec((1, d), lambda b, i: (0, 0)),
        ],
        out_specs=[pl.BlockSpec((tm, d), lambda b, i: (b * nq + i, 0)), pl.BlockSpec((tm, d), lambda b, i: (b * nq + i, 0))],
        out_shape=[jax.ShapeDtypeStruct((t, d), F32), jax.ShapeDtypeStruct((t, d), BF16)],
        compiler_params=_cparams(("parallel", "arbitrary"), blk, temp),
        name="mem_block",
    )(h, kvm, norm_g.reshape(1, d), w_q.astype(BF16), w_o.astype(BF16), q_norm.reshape(1, -1), k_norm.reshape(1, -1),
      next_norm_g.reshape(1, d))


def _layer(h, mem, rel_bias, p, batch, seq):
    h, hb, ssq = _swiglu_half(h, _norm(h, p['ffn_a_norm']), p['ffn_a_gate'], p['ffn_a_up'], p['ffn_a_down'],
                              stats=True)

    w_in_t = p['w_in'].T
    u = _proj_t(hb, ssq, p['mix_norm'], w_in_t, n_first=U_MLA, second_start=IN_SWA,
                n_second=w_in_t.shape[0] - IN_SWA, tm=2048, tn=512, name="in_proj")

    q_t, k, v_t = _mla_prep(u, seq, p['mla_q_a_norm'], p['mla_kv_a_norm'], p['mla_w_uq'], p['mla_w_ukv'],
                            p['mla_q_norm'], p['mla_k_norm'])
    o_a, ssq_a = _mla_attention(q_t, k, v_t, batch, seq)
    ssq_a = ssq_a.reshape(ssq_a.shape[0], -1, LANES).sum(axis=1)
    o_b, ssq_b = _swa_attention(u, batch, seq, p['swa_q_norm'], p['swa_k_norm'], p['swa_sinks'], rel_bias)
    h = _proj2(o_a, o_b, ssq_a, ssq_b, p['out_norm_mla'], p['out_norm_swa'], p['w_out'], h,
               tm=2048, tn=256, name="out_proj")

    mem_len = mem.shape[0] // batch
    kvm = _mm_w32(_norm(mem, p['mem_norm']), p['mem_w_kv'], tm=512, tn=512, name="mem_kv_proj")
    h, xn = _mem_block(h, kvm, batch, seq, mem_len, p['mem_attn_norm'], p['mem_w_q'], p['mem_w_o'],
                       p['mem_q_norm'], p['mem_k_norm'], p['ffn_b_norm'])

    return _swiglu_half(h, xn, p['ffn_b_gate'], p['ffn_b_up'], p['ffn_b_down'])


_PARAM_NAMES = (
    'ffn_a_norm', 'ffn_a_gate', 'ffn_a_up', 'ffn_a_down', 'mix_norm', 'w_in',
    'mla_q_a_norm', 'mla_kv_a_norm', 'mla_w_uq', 'mla_w_ukv', 'mla_q_norm', 'mla_k_norm',
    'swa_q_norm', 'swa_k_norm', 'swa_sinks', 'out_norm_mla', 'out_norm_swa', 'w_out',
    'mem_attn_norm', 'mem_norm', 'mem_w_q', 'mem_w_kv', 'mem_q_norm', 'mem_k_norm', 'mem_w_o',
    'ffn_b_norm', 'ffn_b_gate', 'ffn_b_up', 'ffn_b_down',
)


def kernel(x, mem, rel_bias, ffn_a_norm, ffn_a_gate, ffn_a_up, ffn_a_down, mix_norm, w_in, mla_q_a_norm, mla_kv_a_norm, mla_w_uq, mla_w_ukv, mla_q_norm, mla_k_norm, swa_q_norm, swa_k_norm, swa_sinks, out_norm_mla, out_norm_swa, w_out, mem_attn_norm, mem_norm, mem_w_q, mem_w_kv, mem_q_norm, mem_k_norm, mem_w_o, ffn_b_norm, ffn_b_gate, ffn_b_up, ffn_b_down):
    stacked = (ffn_a_norm, ffn_a_gate, ffn_a_up, ffn_a_down, mix_norm, w_in, mla_q_a_norm, mla_kv_a_norm,
               mla_w_uq, mla_w_ukv, mla_q_norm, mla_k_norm, swa_q_norm, swa_k_norm, swa_sinks, out_norm_mla,
               out_norm_swa, w_out, mem_attn_norm, mem_norm, mem_w_q, mem_w_kv, mem_q_norm, mem_k_norm, mem_w_o,
               ffn_b_norm, ffn_b_gate, ffn_b_up, ffn_b_down)
    batch, seq, d = x.shape
    h = x.reshape(batch * seq, d)
    mem2 = mem.reshape(-1, d)
    for layer in range(ffn_a_norm.shape[0]):
        p = {name: arr[layer] for name, arr in zip(_PARAM_NAMES, stacked)}
        h = _layer(h, mem2, rel_bias, p, batch, seq)
    return h.reshape(batch, seq, d)
```

```python
import functools
import math

import jax
import jax.numpy as jnp
from jax import lax
from jax.experimental import pallas as pl
from jax.experimental.pallas import tpu as pltpu

F32 = jnp.float32
BF16 = jnp.bfloat16

MEM_HEADS = 4
MEM_HEAD_DIM = 128
MLA_HEADS = 16
MLA_Q_RANK = 896
MLA_KV_RANK = 512
MLA_NOPE = 128
MLA_ROPE = 64
MLA_V = 128
ROPE_THETA = 10000.0
SWA_HEADS = 32
SWA_KV_HEADS = 4
SWA_HEAD_DIM = 64
WINDOW = 128
REL_BUCKETS = 32
REL_MAX_DIST = 128
BLOCK_Q = 128
EPS = 1e-6
NEG = -1e30

MLA_QK = MLA_NOPE + MLA_ROPE
SWA_GROUP = SWA_HEADS // SWA_KV_HEADS
SWA_WIDTH = SWA_HEADS * SWA_HEAD_DIM
SWA_KV_WIDTH = SWA_KV_HEADS * SWA_HEAD_DIM
MLA_WIDTH = MLA_HEADS * MLA_V

LANES = 128
MLA_HEAD_PAD = 2 * LANES
ROPE_PAD = LANES

U_CKV = MLA_Q_RANK
U_KR = U_CKV + MLA_KV_RANK
U_MLA = U_KR + ROPE_PAD
IN_SWA = U_KR + MLA_ROPE
S_K = SWA_WIDTH
S_V = S_K + SWA_KV_WIDTH

LOG2E = math.log2(math.e)
KEY_BLOCK = 256

V7X_SCOPED_VMEM_BYTES = 60000 * 1024


def _tile(n, pref, align=8):
    if n <= pref:
        return n
    t = (pref // align) * align
    while t >= align:
        if n % t == 0:
            return t
        t -= align
    return n


def _nbytes(shape, dtype):
    n = 1
    for s in shape:
        n *= s
    return n * jnp.dtype(dtype).itemsize


def _cparams(semantics, pipelined_bytes, temp_bytes=0):
    need = 2 * pipelined_bytes + temp_bytes + (4 << 20)
    return pltpu.CompilerParams(
        dimension_semantics=semantics,
        vmem_limit_bytes=int(min(max(need, 16 << 20), V7X_SCOPED_VMEM_BYTES)),
    )


def _rms_rows(x, g, n=None):
    n = x.shape[-1] if n is None else n
    r = lax.rsqrt(jnp.sum(x * x, axis=-1, keepdims=True) / float(n) + EPS)
    return x * r * g


def _norm_kernel(x_ref, g_ref, o_ref):
    o_ref[...] = _rms_rows(x_ref[...], g_ref[...]).astype(o_ref.dtype)


def _norm(x, g, out_dtype=BF16):
    rows, d = x.shape
    tr = _tile(rows, 512)
    blk = _nbytes((tr, d), F32) + _nbytes((tr, d), out_dtype)
    return pl.pallas_call(
        _norm_kernel,
        grid=(rows // tr,),
        in_specs=[pl.BlockSpec((tr, d), lambda i: (i, 0)), pl.BlockSpec((1, d), lambda i: (0, 0))],
        out_specs=pl.BlockSpec((tr, d), lambda i: (i, 0)),
        out_shape=jax.ShapeDtypeStruct((rows, d), out_dtype),
        compiler_params=_cparams(("parallel",), blk, _nbytes((tr, d), F32)),
        name="rmsnorm",
    )(x, g.reshape(1, d))


def _mm_kernel(a_ref, w_ref, o_ref):
    o_ref[...] = jnp.dot(a_ref[...], w_ref[...], preferred_element_type=F32).astype(o_ref.dtype)


def _mm_res_kernel(a_ref, w_ref, r_ref, o_ref):
    o_ref[...] = r_ref[...] + jnp.dot(a_ref[...], w_ref[...], preferred_element_type=F32)


def _mm(a, w, res=None, *, tm, tn, out_dtype=F32, name="matmul"):
    m, k = a.shape
    n = w.shape[1]
    tm, tn = _tile(m, tm), _tile(n, tn, LANES)
    blk = _nbytes((tm, k), a.dtype) + _nbytes((k, tn), w.dtype) + _nbytes((tm, tn), out_dtype)
    in_specs = [pl.BlockSpec((tm, k), lambda j, i: (i, 0)), pl.BlockSpec((k, tn), lambda j, i: (0, j))]
    args = [a, w]
    kern = _mm_kernel
    if res is not None:
        in_specs.append(pl.BlockSpec((tm, tn), lambda j, i: (i, j)))
        args.append(res)
        blk += _nbytes((tm, tn), F32)
        kern = _mm_res_kernel
    return pl.pallas_call(
        kern,
        grid=(n // tn, m // tm),
        in_specs=in_specs,
        out_specs=pl.BlockSpec((tm, tn), lambda j, i: (i, j)),
        out_shape=jax.ShapeDtypeStruct((m, n), out_dtype),
        compiler_params=_cparams(("parallel", "arbitrary"), blk, 2 * _nbytes((tm, tn), F32)),
        name=name,
    )(*args)


def _row_ssq(x):
    sq = x * x
    part = sq[:, :LANES]
    for c in range(1, x.shape[1] // LANES):
        part = part + sq[:, c * LANES:(c + 1) * LANES]
    return part


def _inv_rms(ssq_block, n):
    return lax.rsqrt(jnp.sum(ssq_block, axis=-1, keepdims=True) / float(n) + EPS)


def _mm_res_stats_kernel(a_ref, w_ref, r_ref, o_ref, ob_ref, ssq_ref):
    h = r_ref[...] + jnp.dot(a_ref[...], w_ref[...], preferred_element_type=F32)
    o_ref[...] = h
    ob_ref[...] = h.astype(ob_ref.dtype)
    ssq_ref[...] = _row_ssq(h)


def _mm_res_stats(a, w, res, *, tm, tn, name):
    m, k = a.shape
    n = w.shape[1]
    tm, tn = _tile(m, tm), _tile(n, tn, LANES)
    blk = (_nbytes((tm, k), a.dtype) + _nbytes((k, tn), w.dtype) + 2 * _nbytes((tm, tn), F32) + _nbytes((tm, tn), BF16)
           + _nbytes((tm, LANES), F32))
    return pl.pallas_call(
        _mm_res_stats_kernel,
        grid=(n // tn, m // tm),
        in_specs=[
            pl.BlockSpec((tm, k), lambda j, i: (i, 0)),
            pl.BlockSpec((k, tn), lambda j, i: (0, j)),
            pl.BlockSpec((tm, tn), lambda j, i: (i, j)),
        ],
        out_specs=[
            pl.BlockSpec((tm, tn), lambda j, i: (i, j)),
            pl.BlockSpec((tm, tn), lambda j, i: (i, j)),
            pl.BlockSpec((tm, LANES), lambda j, i: (i, j)),
        ],
        out_shape=[jax.ShapeDtypeStruct((m, n), F32), jax.ShapeDtypeStruct((m, n), BF16),
                   jax.ShapeDtypeStruct((m, (n // tn) * LANES), F32)],
        compiler_params=_cparams(("parallel", "arbitrary"), blk, 3 * _nbytes((tm, tn), F32)),
        name=name,
    )(a, w, res)


def _proj_t_kernel(a_ref, ssq_ref, g_ref, wt_ref, o_ref, r_ref):
    @pl.when(pl.program_id(1) == 0)
    def _():
        r_ref[...] = jnp.broadcast_to(_inv_rms(ssq_ref[...], a_ref.shape[1]), r_ref.shape)

    wb = (wt_ref[...] * g_ref[...]).astype(BF16)
    acc = lax.dot_general(a_ref[...], wb, (((1,), (1,)), ((), ())), preferred_element_type=F32)
    o_ref[...] = acc * r_ref[:, :1]


def _proj_t(a, ssq, g, wt, *, n_first, second_start, n_second, tm, tn, name):
    m, k = a.shape
    tm = _tile(m, tm)
    t1 = n_first // tn
    nt = t1 + n_second // tn
    shift = second_start - n_first
    wt_map = lambda i, j: (pl.multiple_of(j * tn + jnp.where(j >= t1, shift, 0), 8), 0)
    once = pl.Buffered(1)
    blk = _nbytes((tn, k), F32) + _nbytes((tm, tn), F32)
    return pl.pallas_call(
        _proj_t_kernel,
        grid=(m // tm, nt),
        in_specs=[
            pl.BlockSpec((tm, k), lambda i, j: (i, 0), pipeline_mode=once),
            pl.BlockSpec((tm, ssq.shape[1]), lambda i, j: (i, 0), pipeline_mode=once),
            pl.BlockSpec((1, k), lambda i, j: (0, 0)),
            pl.BlockSpec((pl.Element(tn), pl.Element(k)), wt_map),
        ],
        out_specs=pl.BlockSpec((tm, tn), lambda i, j: (i, j)),
        out_shape=jax.ShapeDtypeStruct((m, nt * tn), F32),
        scratch_shapes=[pltpu.VMEM((tm, LANES), F32)],
        compiler_params=_cparams(("parallel", "arbitrary"), blk,
                                 _nbytes((tm, k), BF16) + _nbytes((tm, ssq.shape[1]), F32) + _nbytes((tn, k), BF16)
                                 + 2 * _nbytes((tm, tn), F32)),
        name=name,
    )(a, ssq, g.reshape(1, k), wt)


def _proj2_kernel(a_ref, b_ref, ssqa_ref, ssqb_ref, gt_ref, w_ref, r_ref, o_ref):
    ka = a_ref.shape[1]
    wb = (w_ref[...] * gt_ref[...]).astype(BF16)
    ya = jnp.dot(a_ref[...], wb[:ka], preferred_element_type=F32) * _inv_rms(ssqa_ref[...], ka)
    yb = jnp.dot(b_ref[...], wb[ka:], preferred_element_type=F32) * _inv_rms(ssqb_ref[...], b_ref.shape[1])
    o_ref[...] = r_ref[...] + ya + yb


def _proj2(a, b, ssqa, ssqb, ga, gb, w, res, *, tm, tn, name):
    m, ka = a.shape
    kb = b.shape[1]
    n = w.shape[1]
    tm, tn = _tile(m, tm), _tile(n, tn, LANES)
    gtab = jnp.broadcast_to(jnp.concatenate([ga, gb])[:, None], (ka + kb, tn))
    once = pl.Buffered(1)
    blk = _nbytes((ka + kb, tn), F32) * 2 + 2 * _nbytes((tm, tn), F32) + _nbytes((tm, ssqa.shape[1] + LANES), F32)
    return pl.pallas_call(
        _proj2_kernel,
        grid=(m // tm, n // tn),
        in_specs=[
            pl.BlockSpec((tm, ka), lambda i, j: (i, 0), pipeline_mode=once),
            pl.BlockSpec((tm, kb), lambda i, j: (i, 0), pipeline_mode=once),
            pl.BlockSpec((tm, ssqa.shape[1]), lambda i, j: (i, 0)),
            pl.BlockSpec((tm, LANES), lambda i, j: (i, 0)),
            pl.BlockSpec((ka + kb, tn), lambda i, j: (0, 0), pipeline_mode=once),
            pl.BlockSpec((ka + kb, tn), lambda i, j: (0, j)),
            pl.BlockSpec((tm, tn), lambda i, j: (i, j)),
        ],
        out_specs=pl.BlockSpec((tm, tn), lambda i, j: (i, j)),
        out_shape=jax.ShapeDtypeStruct((m, n), F32),
        compiler_params=_cparams(("parallel", "arbitrary"), blk,
                                 _nbytes((tm, ka + kb), BF16) + _nbytes((ka + kb, tn), BF16) + 3 * _nbytes((tm, tn), F32)),
        name=name,
    )(a, b, ssqa, ssqb, gtab, w, res)


def _mm_w32_kernel(a_ref, w_ref, *rest, has_res):
    if has_res:
        r_ref, o_ref, wb_ref = rest
    else:
        o_ref, wb_ref = rest

    @pl.when(pl.program_id(1) == 0)
    def _():
        wb_ref[...] = w_ref[...].astype(BF16)

    acc = jnp.dot(a_ref[...], wb_ref[...], preferred_element_type=F32)
    if has_res:
        acc = r_ref[...] + acc
    o_ref[...] = acc.astype(o_ref.dtype)


def _mm_w32(a, w, res=None, *, tm, tn, n=None, out_dtype=F32, name="matmul"):
    m, k = a.shape
    n = w.shape[1] if n is None else n
    tm, tn = _tile(m, tm), _tile(n, tn, LANES)
    blk = _nbytes((tm, k), a.dtype) + _nbytes((k, tn), F32) + _nbytes((tm, tn), out_dtype)
    in_specs = [pl.BlockSpec((tm, k), lambda j, i: (i, 0)), pl.BlockSpec((k, tn), lambda j, i: (0, j))]
    args = [a, w]
    if res is not None:
        in_specs.append(pl.BlockSpec((tm, tn), lambda j, i: (i, j)))
        args.append(res)
        blk += _nbytes((tm, tn), F32)
    return pl.pallas_call(
        functools.partial(_mm_w32_kernel, has_res=res is not None),
        grid=(n // tn, m // tm),
        in_specs=in_specs,
        out_specs=pl.BlockSpec((tm, tn), lambda j, i: (i, j)),
        out_shape=jax.ShapeDtypeStruct((m, n), out_dtype),
        scratch_shapes=[pltpu.VMEM((k, tn), BF16)],
        compiler_params=_cparams(("parallel", "arbitrary"), blk, _nbytes((k, tn), BF16) + 2 * _nbytes((tm, tn), F32)),
        name=name,
    )(*args)


def _gated_kernel(x_ref, wg_ref, wu_ref, wd_ref, o_ref, wdb_ref):
    @pl.when(pl.program_id(0) == 0)
    def _():
        wdb_ref[...] = wd_ref[...].astype(BF16)

    x = x_ref[...]
    g = jnp.dot(x, wg_ref[...].astype(BF16), preferred_element_type=F32)
    u = jnp.dot(x, wu_ref[...].astype(BF16), preferred_element_type=F32)
    o_ref[...] = (0.25 * g * (1.0 + jnp.tanh(0.5 * g)) * u).astype(o_ref.dtype)


def _gated(x, wg, wu, wd, *, tm, tn):
    m, k = x.shape
    n = wg.shape[1]
    d_out = wd.shape[1]
    tm, tn = _tile(m, tm), _tile(n, tn, LANES)
    nj = n // tn
    blk = (2 * _nbytes((k, tn), F32) + _nbytes((tm, tn), BF16)
           + _nbytes((tn, d_out), F32) + _nbytes((tn, d_out), BF16))
    wd_map = lambda i, j: (jnp.where(i == 0, j, nj - 1), 0)
    return pl.pallas_call(
        _gated_kernel,
        grid=(m // tm, nj),
        in_specs=[
            pl.BlockSpec((tm, k), lambda i, j: (i, 0), pipeline_mode=pl.Buffered(1)),
            pl.BlockSpec((k, tn), lambda i, j: (0, j)),
            pl.BlockSpec((k, tn), lambda i, j: (0, j)),
            pl.BlockSpec((tn, d_out), wd_map),
        ],
        out_specs=[pl.BlockSpec((tm, tn), lambda i, j: (i, j)), pl.BlockSpec((tn, d_out), wd_map)],
        out_shape=[jax.ShapeDtypeStruct((m, n), BF16), jax.ShapeDtypeStruct((n, d_out), BF16)],
        compiler_params=_cparams(("arbitrary", "arbitrary"), blk,
                                 _nbytes((tm, k), BF16) + 2 * _nbytes((k, tn), BF16) + 3 * _nbytes((tm, tn), F32)),
        name="ffn_gate_up",
    )(x, wg, wu, wd)


def _swiglu_half(h, xn, wg, wu, wd, stats=False):
    act, wd_bf = _gated(xn, wg, wu, wd, tm=2048, tn=256)
    if stats:
        return _mm_res_stats(act, wd_bf, h, tm=512, tn=512, name="ffn_down_stats")
    return _mm(act, wd_bf, h, tm=512, tn=512, name="ffn_down")


def _rope_chunk(y, cos, s1, s2):
    return y * cos + pltpu.roll(y, ROPE_PAD - MLA_ROPE // 2, 1) * s1 + pltpu.roll(y, MLA_ROPE // 2, 1) * s2


def _mla_q_kernel(u_ref, ga_ref, wt_ref, gt_ref, cos_ref, sin_ref, o_ref):
    cn = _rms_rows(u_ref[:, :MLA_Q_RANK], ga_ref[...]).astype(BF16)
    qt = lax.dot_general(wt_ref[...], cn, (((1,), (1,)), ((), ())), preferred_element_type=F32)
    cos, sin, gain = cos_ref[...], sin_ref[...], gt_ref[...]
    half = MLA_ROPE // 2
    zeros = jnp.zeros((MLA_HEAD_PAD - MLA_QK, qt.shape[1]), o_ref.dtype)
    for h in range(MLA_HEADS):
        lo = h * MLA_HEAD_PAD
        c = qt[lo:lo + MLA_HEAD_PAD, :]
        y = c * lax.rsqrt(jnp.sum(c * c, axis=0, keepdims=True) / float(MLA_QK) + EPS) * gain
        x1, x2 = y[MLA_NOPE:MLA_NOPE + half, :], y[MLA_NOPE + half:MLA_QK, :]
        o_ref[lo:lo + MLA_NOPE, :] = y[:MLA_NOPE, :].astype(o_ref.dtype)
        o_ref[lo + MLA_NOPE:lo + MLA_NOPE + half, :] = (x1 * cos - x2 * sin).astype(o_ref.dtype)
        o_ref[lo + MLA_NOPE + half:lo + MLA_QK, :] = (x2 * cos + x1 * sin).astype(o_ref.dtype)
        o_ref[lo + MLA_QK:lo + MLA_HEAD_PAD, :] = zeros


def _mla_kv_kernel(u_ref, ga_ref, wk_ref, wvt_ref, gn_ref, gr_ref, cos_ref, s1_ref, s2_ref, k_ref, vt_ref):
    cn = _rms_rows(u_ref[:, U_CKV:U_KR], ga_ref[...]).astype(BF16)
    kv = jnp.dot(cn, wk_ref[...], preferred_element_type=F32)
    vt = lax.dot_general(wvt_ref[...], cn, (((1,), (1,)), ((), ())), preferred_element_type=F32)
    vt_ref[...] = vt.astype(vt_ref.dtype)
    kr = u_ref[:, U_KR:U_MLA]
    kr = jnp.where(lax.broadcasted_iota(jnp.int32, kr.shape, 1) < MLA_ROPE, kr, 0.0)
    ss_r = jnp.sum(kr * kr, axis=-1, keepdims=True)
    gn = gn_ref[...]
    kr_rot = _rope_chunk(kr * gr_ref[...], cos_ref[...], s1_ref[...], s2_ref[...])
    for h in range(MLA_HEADS):
        kn = kv[:, h * MLA_NOPE:(h + 1) * MLA_NOPE]
        r = lax.rsqrt((jnp.sum(kn * kn, axis=-1, keepdims=True) + ss_r) / float(MLA_QK) + EPS)
        lo = h * MLA_HEAD_PAD
        k_ref[:, lo:lo + MLA_NOPE] = (kn * r * gn).astype(k_ref.dtype)
        k_ref[:, lo + MLA_NOPE:lo + MLA_HEAD_PAD] = (kr_rot * r).astype(k_ref.dtype)


def _mla_prep_kernel(u_ref, gaq_ref, wqt_ref, gt_ref, cost_ref, sint_ref, gakv_ref, wk_ref, wvt_ref, gn_ref, gr_ref,
                     cos_ref, s1_ref, s2_ref, qt_ref, k_ref, vt_ref):
    _mla_q_kernel(u_ref, gaq_ref, wqt_ref, gt_ref, cost_ref, sint_ref, qt_ref)
    _mla_kv_kernel(u_ref, gakv_ref, wk_ref, wvt_ref, gn_ref, gr_ref, cos_ref, s1_ref, s2_ref, k_ref, vt_ref)


def _rope_tables(seq):
    half = MLA_ROPE // 2
    inv = ROPE_THETA ** (-jnp.arange(half, dtype=F32) / half)
    ang = jnp.arange(seq).astype(F32)[:, None] * inv[None, :]
    cos, sin = jnp.cos(ang), jnp.sin(ang)
    z = jnp.zeros_like(cos)
    pad = jnp.zeros((seq, ROPE_PAD - MLA_ROPE), F32)
    return (jnp.concatenate([cos, cos, pad], 1), jnp.concatenate([-sin, z, pad], 1),
            jnp.concatenate([z, sin, pad], 1), cos.T, sin.T)


def _mla_prep(u, seq, q_a_norm, kv_a_norm, w_uq, w_ukv, q_norm, k_norm):
    t = u.shape[0]
    tm = _tile(seq, KEY_BLOCK)
    spb = seq // tm
    cos, s1, s2, cos_t, sin_t = _rope_tables(seq)
    tab_spec = pl.BlockSpec((tm, ROPE_PAD), lambda i: (i % spb, 0))
    tab_t_spec = pl.BlockSpec((MLA_ROPE // 2, tm), lambda i: (0, i % spb))
    qw = MLA_HEADS * MLA_HEAD_PAD
    vw = MLA_HEADS * MLA_V

    wq_t = jnp.pad(w_uq.reshape(MLA_Q_RANK, MLA_HEADS, MLA_QK), ((0, 0), (0, 0), (0, MLA_HEAD_PAD - MLA_QK)))
    wq_t = wq_t.reshape(MLA_Q_RANK, qw).T.astype(BF16)
    gain_t = jnp.concatenate([q_norm * (MLA_QK ** -0.5 * LOG2E), jnp.zeros((MLA_HEAD_PAD - MLA_QK,), F32)])
    gain_t = jnp.broadcast_to(gain_t[:, None], (MLA_HEAD_PAD, tm))
    wkv = w_ukv.reshape(MLA_KV_RANK, MLA_HEADS, MLA_NOPE + MLA_V)
    wk = wkv[:, :, :MLA_NOPE].reshape(MLA_KV_RANK, -1).astype(BF16)
    wv_t = wkv[:, :, MLA_NOPE:].reshape(MLA_KV_RANK, -1).T.astype(BF16)
    blk = (_nbytes((tm, U_MLA), F32) + _nbytes(wq_t.shape, BF16) + _nbytes((qw, tm), BF16)
           + _nbytes((MLA_HEAD_PAD, tm), F32) + 2 * _nbytes((MLA_ROPE // 2, tm), F32)
           + _nbytes(wk.shape, BF16) + _nbytes(wv_t.shape, BF16) + _nbytes((tm, qw), BF16)
           + _nbytes((vw, tm), BF16) + 3 * _nbytes((tm, ROPE_PAD), F32))
    const = lambda shape: pl.BlockSpec(shape, lambda i: (0, 0))
    return pl.pallas_call(
        _mla_prep_kernel,
        grid=(t // tm,),
        in_specs=[
            pl.BlockSpec((tm, U_MLA), lambda i: (i, 0)),
            const((1, MLA_Q_RANK)), const(wq_t.shape), const((MLA_HEAD_PAD, tm)), tab_t_spec, tab_t_spec,
            const((1, MLA_KV_RANK)), const(wk.shape), const(wv_t.shape), const((1, MLA_NOPE)), const((1, ROPE_PAD)),
            tab_spec, tab_spec, tab_spec,
        ],
        out_specs=[pl.BlockSpec((qw, tm), lambda i: (0, i)), pl.BlockSpec((tm, qw), lambda i: (i, 0)),
                   pl.BlockSpec((None, vw, tm), lambda i: (i, 0, 0))],
        out_shape=[jax.ShapeDtypeStruct((qw, t), BF16), jax.ShapeDtypeStruct((t, qw), BF16),
                   jax.ShapeDtypeStruct((t // tm, vw, tm), BF16)],
        compiler_params=_cparams(("parallel",), blk, 4 * _nbytes((qw, tm), F32)),
        name="mla_prep",
    )(u, q_a_norm.reshape(1, -1), wq_t, gain_t, cos_t, sin_t,
      kv_a_norm.reshape(1, -1), wk, wv_t, k_norm[:MLA_NOPE].reshape(1, -1),
      jnp.concatenate([k_norm[MLA_NOPE:], jnp.zeros((ROPE_PAD - MLA_ROPE,), F32)]).reshape(1, -1), cos, s1, s2)


def _mla_attn_kernel(qt_ref, k_ref, vt_ref, o_ref, ssq_ref, sa_ref, sb_ref, *, tq, kb):
    per = tq // kb
    nq = qt_ref.shape[1] // tq
    nh = qt_ref.shape[0] // MLA_HEAD_PAD

    def scores(qi, blk, s_ref):
        for hh in range(nh):
            qt = qt_ref[hh * MLA_HEAD_PAD:(hh + 1) * MLA_HEAD_PAD, qi * tq:(qi + 1) * tq]
            for d in range(per):
                start = (blk * per + d) * kb
                kblk = k_ref[start:start + kb, hh * MLA_HEAD_PAD:(hh + 1) * MLA_HEAD_PAD]
                s = jnp.dot(kblk, qt, preferred_element_type=F32)
                if blk == qi:
                    key = d * kb + lax.broadcasted_iota(jnp.int32, s.shape, 0)
                    qry = lax.broadcasted_iota(jnp.int32, s.shape, 1)
                    s = jnp.where(key <= qry, s, NEG)
                s_ref[hh, d * kb:(d + 1) * kb, :] = s

    def softmax_pv(blk, s_ref, carries):
        out = []
        for hh in range(nh):
            m, l, acc = carries[hh]
            ss = [s_ref[hh, d * kb:(d + 1) * kb, :] for d in range(per)]
            m_new = m
            for s in ss:
                m_new = jnp.maximum(m_new, jnp.max(s, axis=0, keepdims=True))
            alpha = jnp.exp2(m - m_new)
            l = alpha * l
            acc = alpha * acc
            for d, s in enumerate(ss):
                p = jnp.exp2(s - m_new)
                l = l + jnp.sum(p, axis=0, keepdims=True)
                vt = vt_ref[blk * per + d, hh * MLA_V:(hh + 1) * MLA_V, :]
                acc = acc + jnp.dot(vt, p.astype(BF16), preferred_element_type=F32)
            out.append((m_new, l, acc))
        return tuple(out)

    items = [(qi, blk) for qi in range(nq) for blk in range(qi + 1)]
    bufs = (sa_ref, sb_ref)
    scores(*items[0], bufs[0])
    carries = None
    for idx, (qi, blk) in enumerate(items):
        if idx + 1 < len(items):
            scores(*items[idx + 1], bufs[(idx + 1) % 2])
        if blk == 0:
            carries = tuple((jnp.full((1, tq), NEG, F32), jnp.zeros((1, tq), F32), jnp.zeros((MLA_V, tq), F32))
                            for _ in range(nh))
        carries = softmax_pv(blk, bufs[idx % 2], carries)
        if blk == qi:
            ssq = None
            for hh in range(nh):
                _, l, acc = carries[hh]
                o = (acc * (1.0 / l)).T
                o_ref[qi * tq:(qi + 1) * tq, hh * MLA_V:(hh + 1) * MLA_V] = o.astype(o_ref.dtype)
                ssq = _row_ssq(o) if ssq is None else ssq + _row_ssq(o)
            ssq_ref[qi * tq:(qi + 1) * tq, :] = ssq


def _mla_attention(q_t, k, v_t, batch, seq):
    t = k.shape[0]
    kb = v_t.shape[2]
    tq = _tile(seq, 512)
    nkb = seq // kb
    nh = 2
    blk = nh * (2 * _nbytes((seq, MLA_HEAD_PAD), BF16) + _nbytes((seq, MLA_V), BF16) + _nbytes((seq, MLA_V), F32))
    return pl.pallas_call(
        functools.partial(_mla_attn_kernel, tq=tq, kb=kb),
        grid=(batch, MLA_HEADS // nh),
        in_specs=[
            pl.BlockSpec((nh * MLA_HEAD_PAD, seq), lambda b, h: (h, b)),
            pl.BlockSpec((seq, nh * MLA_HEAD_PAD), lambda b, h: (b, h)),
            pl.BlockSpec((nkb, nh * MLA_V, kb), lambda b, h: (b, h, 0)),
        ],
        out_specs=[pl.BlockSpec((seq, nh * MLA_V), lambda b, h: (b, h)), pl.BlockSpec((seq, LANES), lambda b, h: (b, h))],
        out_shape=[jax.ShapeDtypeStruct((t, MLA_WIDTH), BF16),
                   jax.ShapeDtypeStruct((t, (MLA_HEADS // nh) * LANES), F32)],
        scratch_shapes=[pltpu.VMEM((nh, tq, tq), F32), pltpu.VMEM((nh, tq, tq), F32)],
        compiler_params=_cparams(("parallel", "parallel"), blk, 10 * nh * _nbytes((tq, tq), F32)),
        name="mla_attention",
    )(q_t, k, v_t)


def _t5_bucket(dist):
    max_exact = REL_BUCKETS // 2
    d = jnp.maximum(dist, 1).astype(F32)
    large = max_exact + (jnp.log(d / max_exact) / math.log(REL_MAX_DIST / max_exact)
                         * (REL_BUCKETS - max_exact)).astype(jnp.int32)
    large = jnp.minimum(large, REL_BUCKETS - 1)
    return jnp.where(dist < max_exact, dist, large)


def _swa_kernel(sink_ref, q0_ref, q1_ref, q2_ref, q3_ref, kp_ref, kc_ref, vp_ref, vc_ref, gfold_ref, bias_ref,
                o_ref, ssq_ref):
    n = pl.program_id(1)
    nk, hd, kvw = 2 * BLOCK_Q, SWA_HEAD_DIM, SWA_KV_WIDTH
    key = lax.broadcasted_iota(jnp.int32, (nk, BLOCK_Q), 0)
    qry = lax.broadcasted_iota(jnp.int32, (nk, BLOCK_Q), 1)
    dist = qry + BLOCK_Q - key
    valid = (dist >= 0) & (dist < WINDOW) & ((key >= BLOCK_Q) | (n > 0))

    k2 = jnp.concatenate([kp_ref[...], kc_ref[...]], axis=0)
    sq = k2 * k2
    sq_hi = sq.astype(BF16)
    sq_lo = (sq - sq_hi.astype(F32)).astype(BF16)
    shift = hd.bit_length() - 1
    ri = lax.broadcasted_iota(jnp.int32, (kvw, kvw), 0) >> shift
    ci = lax.broadcasted_iota(jnp.int32, (kvw, kvw), 1) >> shift
    ind = jnp.where(ri == ci, 1.0, 0.0).astype(BF16)
    ss = jnp.dot(sq_hi, ind, preferred_element_type=F32) + jnp.dot(sq_lo, ind, preferred_element_type=F32)
    k2n = (k2 * lax.rsqrt(ss / float(hd) + EPS) * gfold_ref[...]).astype(BF16)
    v2t = jnp.concatenate([vp_ref[...], vc_ref[...]], axis=0).T.astype(BF16)

    qt = jnp.concatenate([q0_ref[...], q1_ref[...], q2_ref[...], q3_ref[...]], axis=1).T
    gw = SWA_GROUP * BLOCK_Q
    rows = []
    for kh in range(SWA_KV_HEADS):
        pieces = []
        if kh > 0:
            pieces.append(jnp.zeros((hd, kh * gw), BF16))
        for g in range(SWA_GROUP):
            j = kh * SWA_GROUP + g
            c = qt[j * hd:(j + 1) * hd, :]
            pieces.append((c * lax.rsqrt(jnp.sum(c * c, axis=0, keepdims=True) / float(hd) + EPS)).astype(BF16))
        if kh < SWA_KV_HEADS - 1:
            pieces.append(jnp.zeros((hd, (SWA_KV_HEADS - 1 - kh) * gw), BF16))
        rows.append(jnp.concatenate(pieces, axis=1))
    q_bd = jnp.concatenate(rows, axis=0)
    s_all = jnp.dot(k2n, q_bd, preferred_element_type=F32)

    es, invs = [], []
    for j in range(SWA_HEADS):
        lo = j * BLOCK_Q
        s = jnp.where(valid, s_all[:, lo:lo + BLOCK_Q] + bias_ref[:, lo:lo + BLOCK_Q], NEG)
        sink = sink_ref[j]
        m = jnp.maximum(jnp.max(s, axis=0, keepdims=True), sink)
        e = jnp.exp2(s - m)
        invs.append(1.0 / (jnp.sum(e, axis=0, keepdims=True) + jnp.exp2(sink - m)))
        es.append(e.astype(BF16))
    e_all = jnp.concatenate(es, axis=1)
    o_all = jnp.dot(v2t, e_all, preferred_element_type=F32)
    outs = []
    for j in range(SWA_HEADS):
        kh = j // SWA_GROUP
        outs.append(o_all[kh * hd:(kh + 1) * hd, j * BLOCK_Q:(j + 1) * BLOCK_Q] * invs[j])
    o = jnp.concatenate(outs, axis=0).T
    o_ref[...] = o.astype(o_ref.dtype)
    ssq_ref[...] = _row_ssq(o)


def _swa_attention(u, batch, seq, q_norm, k_norm, sinks, rel_bias):
    t = u.shape[0]
    nb = seq // BLOCK_Q
    nk = 2 * BLOCK_Q
    by_dist = rel_bias.astype(F32)[_t5_bucket(jnp.arange(WINDOW))].T
    row = jnp.concatenate([jnp.zeros((SWA_HEADS, 1), F32), by_dist], 1)
    bias = jnp.broadcast_to(row[:, None, :], (SWA_HEADS, nk, BLOCK_Q + 1)).reshape(SWA_HEADS, -1)
    bias = bias[:, :nk * BLOCK_Q].reshape(SWA_HEADS, nk, BLOCK_Q)
    bias = bias.transpose(1, 0, 2).reshape(nk, SWA_HEADS * BLOCK_Q) * LOG2E
    sinks = sinks * LOG2E
    gfold = jnp.tile(k_norm * q_norm * (SWA_HEAD_DIM ** -0.5 * LOG2E), SWA_KV_HEADS).reshape(1, SWA_KV_WIDTH)
    cur = lambda c: pl.BlockSpec((BLOCK_Q, SWA_KV_WIDTH), lambda b, n: (b * nb + n, c))
    prev = lambda c: pl.BlockSpec((BLOCK_Q, SWA_KV_WIDTH), lambda b, n: (b * nb + jnp.maximum(n - 1, 0), c))
    ck, cv = (U_MLA + S_K) // SWA_KV_WIDTH, (U_MLA + S_V) // SWA_KV_WIDTH
    gw = SWA_GROUP * SWA_HEAD_DIM
    q_specs = [pl.BlockSpec((BLOCK_Q, gw), functools.partial(lambda b, n, c: (b * nb + n, c), c=U_MLA // gw + kh))
               for kh in range(SWA_KV_HEADS)]
    blk = (_nbytes((BLOCK_Q, SWA_WIDTH), F32) + _nbytes((BLOCK_Q, SWA_WIDTH), BF16) + _nbytes((BLOCK_Q, LANES), F32)
           + 4 * _nbytes((BLOCK_Q, SWA_KV_WIDTH), F32) + _nbytes(bias.shape, F32))
    return pl.pallas_call(
        _swa_kernel,
        grid=(batch, nb),
        in_specs=[pl.BlockSpec(memory_space=pltpu.SMEM)] + q_specs + [
            prev(ck), cur(ck), prev(cv), cur(cv),
            pl.BlockSpec((1, SWA_KV_WIDTH), lambda b, n: (0, 0)),
            pl.BlockSpec(bias.shape, lambda b, n: (0, 0)),
        ],
        out_specs=[pl.BlockSpec((BLOCK_Q, SWA_WIDTH), lambda b, n: (b * nb + n, 0)),
                   pl.BlockSpec((BLOCK_Q, LANES), lambda b, n: (b * nb + n, 0))],
        out_shape=[jax.ShapeDtypeStruct((t, SWA_WIDTH), BF16), jax.ShapeDtypeStruct((t, LANES), F32)],
        compiler_params=_cparams(("parallel", "arbitrary"), blk, 4 * _nbytes((nk, SWA_HEADS * BLOCK_Q), F32)),
        name="swa_attention",
    )(sinks, u, u, u, u, u, u, u, u, gfold, bias)


def _mem_block_kernel(h_ref, kv_ref, gn_ref, wq_ref, wo_ref, gq_ref, gk_ref, gnext_ref, h_out_ref, xn_out_ref):
    h = h_ref[...]
    xn = _rms_rows(h, gn_ref[...]).astype(BF16)
    qm = jnp.dot(xn, wq_ref[...], preferred_element_type=F32)
    gq = gq_ref[...] * (MEM_HEAD_DIM ** -0.5)
    gk = gk_ref[...]
    outs = []
    for hd in range(MEM_HEADS):
        q = _rms_rows(qm[:, hd * MEM_HEAD_DIM:(hd + 1) * MEM_HEAD_DIM], gq).astype(BF16)
        lo = 2 * hd * MEM_HEAD_DIM
        k = _rms_rows(kv_ref[:, lo:lo + MEM_HEAD_DIM], gk).astype(BF16)
        v = kv_ref[:, lo + MEM_HEAD_DIM:lo + 2 * MEM_HEAD_DIM].astype(BF16)
        s = lax.dot_general(q, k, (((1,), (1,)), ((), ())), preferred_element_type=F32)
        e = jnp.exp(s - jnp.max(s, axis=-1, keepdims=True))
        p = (e / jnp.sum(e, axis=-1, keepdims=True)).astype(BF16)
        outs.append(jnp.dot(p, v, preferred_element_type=F32).astype(BF16))
    o = jnp.concatenate(outs, axis=-1)
    h_new = h + jnp.dot(o, wo_ref[...], preferred_element_type=F32)
    h_out_ref[...] = h_new
    xn_out_ref[...] = _rms_rows(h_new, gnext_ref[...]).astype(xn_out_ref.dtype)


def _mem_block(h, kvm, batch, seq, mem_len, norm_g, w_q, w_o, q_norm, k_norm, next_norm_g):
    t, d = h.shape
    w = w_q.shape[1]
    tm = _tile(seq, 256)
    nq = seq // tm
    once = pl.Buffered(1)
    blk = 2 * _nbytes((tm, d), F32) + _nbytes((tm, d), BF16) + _nbytes((mem_len, 2 * w), F32)
    temp = 2 * _nbytes((d, w), BF16) + 3 * _nbytes((tm, d), F32)
    return pl.pallas_call(
        _mem_block_kernel,
        grid=(batch, nq),
        in_specs=[
            pl.BlockSpec((tm, d), lambda b, i: (b * nq + i, 0)),
            pl.BlockSpec((mem_len, 2 * w), lambda b, i: (b, 0)),
            pl.BlockSpec((1, d), lambda b, i: (0, 0)),
            pl.BlockSpec((d, w), lambda b, i: (0, 0), pipeline_mode=once),
            pl.BlockSpec((w, d), lambda b, i: (0, 0), pipeline_mode=once),
            pl.BlockSpec((1, MEM_HEAD_DIM), lambda b, i: (0, 0)),
            pl.BlockSpec((1, MEM_HEAD_DIM), lambda b, i: (0, 0)),
            pl.BlockSpec((1, d), lambda b, i: (0, 0)),
        ],
        out_specs=[pl.BlockSpec((tm, d), lambda b, i: (b * nq + i, 0)), pl.BlockSpec((tm, d), lambda b, i: (b * nq + i, 0))],
        out_shape=[jax.ShapeDtypeStruct((t, d), F32), jax.ShapeDtypeStruct((t, d), BF16)],
        compiler_params=_cparams(("parallel", "arbitrary"), blk, temp),
        name="mem_block",
    )(h, kvm, norm_g.reshape(1, d), w_q.astype(BF16), w_o.astype(BF16), q_norm.reshape(1, -1), k_norm.reshape(1, -1),
      next_norm_g.reshape(1, d))


def _layer(h, mem, rel_bias, p, batch, seq):
    h, hb, ssq = _swiglu_half(h, _norm(h, p['ffn_a_norm']), p['ffn_a_gate'], p['ffn_a_up'], p['ffn_a_down'],
                              stats=True)

    w_in_t = p['w_in'].T
    u = _proj_t(hb, ssq, p['mix_norm'], w_in_t, n_first=U_MLA, second_start=IN_SWA,
                n_second=w_in_t.shape[0] - IN_SWA, tm=2048, tn=512, name="in_proj")

    q_t, k, v_t = _mla_prep(u, seq, p['mla_q_a_norm'], p['mla_kv_a_norm'], p['mla_w_uq'], p['mla_w_ukv'],
                            p['mla_q_norm'], p['mla_k_norm'])
    o_a, ssq_a = _mla_attention(q_t, k, v_t, batch, seq)
    ssq_a = ssq_a.reshape(ssq_a.shape[0], -1, LANES).sum(axis=1)
    o_b, ssq_b = _swa_attention(u, batch, seq, p['swa_q_norm'], p['swa_k_norm'], p['swa_sinks'], rel_bias)
    h = _proj2(o_a, o_b, ssq_a, ssq_b, p['out_norm_mla'], p['out_norm_swa'], p['w_out'], h,
               tm=1024, tn=512, name="out_proj")

    mem_len = mem.shape[0] // batch
    kvm = _mm_w32(_norm(mem, p['mem_norm']), p['mem_w_kv'], tm=512, tn=512, name="mem_kv_proj")
    h, xn = _mem_block(h, kvm, batch, seq, mem_len, p['mem_attn_norm'], p['mem_w_q'], p['mem_w_o'],
                       p['mem_q_norm'], p['mem_k_norm'], p['ffn_b_norm'])

    return _swiglu_half(h, xn, p['ffn_b_gate'], p['ffn_b_up'], p['ffn_b_down'])


_PARAM_NAMES = (
    'ffn_a_norm', 'ffn_a_gate', 'ffn_a_up', 'ffn_a_down', 'mix_norm', 'w_in',
    'mla_q_a_norm', 'mla_kv_a_norm', 'mla_w_uq', 'mla_w_ukv', 'mla_q_norm', 'mla_k_norm',
    'swa_q_norm', 'swa_k_norm', 'swa_sinks', 'out_norm_mla', 'out_norm_swa', 'w_out',
    'mem_attn_norm', 'mem_norm', 'mem_w_q', 'mem_w_kv', 'mem_q_norm', 'mem_k_norm', 'mem_w_o',
    'ffn_b_norm', 'ffn_b_gate', 'ffn_b_up', 'ffn_b_down',
)


def kernel(x, mem, rel_bias, ffn_a_norm, ffn_a_gate, ffn_a_up, ffn_a_down, mix_norm, w_in, mla_q_a_norm, mla_kv_a_norm, mla_w_uq, mla_w_ukv, mla_q_norm, mla_k_norm, swa_q_norm, swa_k_norm, swa_sinks, out_norm_mla, out_norm_swa, w_out, mem_attn_norm, mem_norm, mem_w_q, mem_w_kv, mem_q_norm, mem_k_norm, mem_w_o, ffn_b_norm, ffn_b_gate, ffn_b_up, ffn_b_down):
    stacked = (ffn_a_norm, ffn_a_gate, ffn_a_up, ffn_a_down, mix_norm, w_in, mla_q_a_norm, mla_kv_a_norm,
               mla_w_uq, mla_w_ukv, mla_q_norm, mla_k_norm, swa_q_norm, swa_k_norm, swa_sinks, out_norm_mla,
               out_norm_swa, w_out, mem_attn_norm, mem_norm, mem_w_q, mem_w_kv, mem_q_norm, mem_k_norm, mem_w_o,
               ffn_b_norm, ffn_b_gate, ffn_b_up, ffn_b_down)
    batch, seq, d = x.shape
    h = x.reshape(batch * seq, d)
    mem2 = mem.reshape(-1, d)
    for layer in range(ffn_a_norm.shape[0]):
        p = {name: arr[layer] for name, arr in zip(_PARAM_NAMES, stacked)}
        h = _layer(h, mem2, rel_bias, p, batch, seq)
    return h.reshape(batch, seq, d)
```
